```python
import math
import jax
import jax.numpy as jnp
from jax import lax
import numpy as np

D_MODEL = 1024
BATCH = 4
SEQ = 4096
DEPTH = 4

GRID_W = 64
CTX_LEN = 256
N_MIXERS = 3
EPS = 1e-6

N_HEADS = 16
N_KV_HEADS = 4
HEAD_DIM = D_MODEL // N_HEADS
Q_PER_KV = N_HEADS // N_KV_HEADS
QKV_DIM = (N_HEADS + 2 * N_KV_HEADS) * HEAD_DIM
ROPE_THETA = 10000.0
Q_BLOCK = 128

POOL_WINDOWS = (2, 4, 8, 16)
POOL_GROUP = D_MODEL // len(POOL_WINDOWS)

SSM_D_INNER = 2 * D_MODEL
SSM_HEAD_DIM = 64
SSM_HEADS = SSM_D_INNER // SSM_HEAD_DIM
SSM_GROUPS = 4
SSM_HPG = SSM_HEADS // SSM_GROUPS
SSM_STATE = 128
SSM_CONV = 4
SSM_CHUNK = 128
SSM_GN = SSM_GROUPS * SSM_STATE
SSM_CONV_DIM = SSM_D_INNER + 2 * SSM_GN
SSM_IN_DIM = SSM_D_INNER + SSM_CONV_DIM + 2 * SSM_HEADS
DT_MIN = 0.001
DT_MAX = 0.1

MOE_GROUPS = 4
MOE_EXPERTS_PER_GROUP = 8
MOE_EXPERTS = MOE_GROUPS * MOE_EXPERTS_PER_GROUP
MOE_TOPK = 2
MOE_HIDDEN = 512
MOE_BLOCK = 256

N_ATTN_LAYERS = (DEPTH + 2) // N_MIXERS
N_POOL_LAYERS = (DEPTH + 1) // N_MIXERS
N_SSM_LAYERS = DEPTH // N_MIXERS

kernel_name = 'hybrid_attn_pool_ssd_hmoe_dit'


def rms_norm(x, g):
    xf = x.astype(jnp.float32)
    y = xf * lax.rsqrt(jnp.mean(xf * xf, axis=-1, keepdims=True) + EPS)
    return (y * g.astype(jnp.float32)).astype(x.dtype)


def modulate(h, shift, scale):
    return h * (1 + scale) + shift


def axial_rope_tables(n_tokens):
    rows = n_tokens // GRID_W
    row = jnp.broadcast_to(jnp.arange(rows)[:, None], (rows, GRID_W)).reshape(-1).astype(jnp.float32)
    col = jnp.broadcast_to(jnp.arange(GRID_W)[None, :], (rows, GRID_W)).reshape(-1).astype(jnp.float32)
    n_freq = HEAD_DIM // 4
    inv_freq = ROPE_THETA ** (-jnp.arange(n_freq, dtype=jnp.float32) / n_freq)
    ang = jnp.concatenate([row[:, None] * inv_freq, col[:, None] * inv_freq], axis=-1)
    return jnp.cos(ang), jnp.sin(ang)


def apply_rope(x, cos, sin):
    shape = (1, x.shape[1]) + (1,) * (x.ndim - 3) + (HEAD_DIM // 2,)
    cs = cos.reshape(shape)
    sn = sin.reshape(shape)
    xf = x.astype(jnp.float32).reshape(x.shape[:-1] + (HEAD_DIM // 2, 2))
    x0 = xf[..., 0]
    x1 = xf[..., 1]
    out = jnp.stack([x0 * cs - x1 * sn, x0 * sn + x1 * cs], axis=-1).reshape(x.shape)
    return out.astype(x.dtype)


def _attend(q, k, v):
    s = jnp.einsum('bqkgd,btkd->bkgqt', q, k, preferred_element_type=jnp.float32) * (HEAD_DIM ** -0.5)
    p = jax.nn.softmax(s, axis=-1).astype(v.dtype)
    return jnp.einsum('bkgqt,btkd->bqkgd', p, v)


def attention_mixer(h_lat, h_ctx, w_qkv, w_o, q_g, k_g, cos, sin, ctx_out):
    b, n, _ = h_lat.shape
    m = h_ctx.shape[1]
    hq = N_HEADS * HEAD_DIM
    hkv = N_KV_HEADS * HEAD_DIM
    qkv = h_lat @ w_qkv
    q_l = rms_norm(qkv[..., :hq].reshape(b, n, N_KV_HEADS, Q_PER_KV, HEAD_DIM), q_g)
    k_l = rms_norm(qkv[..., hq:hq + hkv].reshape(b, n, N_KV_HEADS, HEAD_DIM), k_g)
    v_l = qkv[..., hq + hkv:].reshape(b, n, N_KV_HEADS, HEAD_DIM)
    q_l = apply_rope(q_l, cos, sin)
    k_l = apply_rope(k_l, cos, sin)
    if ctx_out:
        qkv_c = h_ctx @ w_qkv
        q_c = rms_norm(qkv_c[..., :hq].reshape(b, m, N_KV_HEADS, Q_PER_KV, HEAD_DIM), q_g)
        kv_c = qkv_c[..., hq:]
    else:
        kv_c = h_ctx @ w_qkv[:, hq:]
    k_c = rms_norm(kv_c[..., :hkv].reshape(b, m, N_KV_HEADS, HEAD_DIM), k_g)
    v_c = kv_c[..., hkv:].reshape(b, m, N_KV_HEADS, HEAD_DIM)
    k_all = jnp.concatenate([k_c, k_l], axis=1)
    v_all = jnp.concatenate([v_c, v_l], axis=1)
    n_blk = n // Q_BLOCK
    qb = jnp.moveaxis(q_l.reshape(b, n_blk, Q_BLOCK, N_KV_HEADS, Q_PER_KV, HEAD_DIM), 1, 0)
    o_l = lax.map(lambda qblk: _attend(qblk, k_all, v_all), qb)
    o_l = jnp.moveaxis(o_l, 0, 1).reshape(b, n, hq) @ w_o
    o_c = (_attend(q_c, k_c, v_c).reshape(b, m, hq) @ w_o) if ctx_out else None
    return o_l, o_c


def pool_mixer(h, w_pool, scale):
    b, n, d = h.shape
    cs = jnp.concatenate([jnp.zeros((b, 1, d), jnp.float32), jnp.cumsum(h.astype(jnp.float32), axis=1)], axis=1)
    t = jnp.arange(n)
    outs = []
    for gi, win in enumerate(POOL_WINDOWS):
        lo = jnp.maximum(t - win // 2, 0)
        hi = jnp.minimum(t + win // 2, n)
        sl = slice(gi * POOL_GROUP, (gi + 1) * POOL_GROUP)
        cg = cs[..., sl]
        mean = (cg[:, hi] - cg[:, lo]) / (hi - lo).astype(jnp.float32)[None, :, None]
        diff = (mean - h[..., sl].astype(jnp.float32)).astype(h.dtype)
        outs.append(diff @ w_pool[gi])
    return jnp.concatenate(outs, axis=-1) * scale


def depthwise_conv(u, w, bias):
    out = lax.conv_general_dilated(u, w[:, None, :], window_strides=(1,),
                                   padding=[(SSM_CONV // 2, (SSM_CONV - 1) // 2)],
                                   dimension_numbers=('NWC', 'WIO', 'NWC'),
                                   feature_group_count=u.shape[-1])
    return out + bias


def ssd_chunked(x, dt, a, bm, cm, h0, with_y):
    b, n = x.shape[:2]
    nc = n // SSM_CHUNK
    Q = SSM_CHUNK
    xs = (x * dt[..., None]).reshape(b, nc, Q, SSM_GROUPS, SSM_HPG, SSM_HEAD_DIM)
    la = (dt * a).reshape(b, nc, Q, SSM_GROUPS, SSM_HPG)
    la_cs = jnp.cumsum(la, axis=2)
    bc = bm.reshape(b, nc, Q, SSM_GROUPS, SSM_STATE)
    cc = cm.reshape(b, nc, Q, SSM_GROUPS, SSM_STATE)
    decay_end = jnp.exp(la_cs[:, :, -1:] - la_cs)
    states = jnp.einsum('bcsgn,bcsgrp->bcgrpn', bc, xs * decay_end[..., None])
    chunk_decay = jnp.exp(la_cs[:, :, -1])

    def step(h, inp):
        st, dec = inp
        return h * dec[..., None, None] + st, h

    h_fin, h_in = lax.scan(step, h0.reshape(b, SSM_GROUPS, SSM_HPG, SSM_HEAD_DIM, SSM_STATE),
                           (jnp.moveaxis(states, 1, 0), jnp.moveaxis(chunk_decay, 1, 0)))
    h_fin = h_fin.reshape(b, SSM_HEADS, SSM_HEAD_DIM, SSM_STATE)
    if not with_y:
        return None, h_fin
    h_in = jnp.moveaxis(h_in, 0, 1)
    la_t = jnp.moveaxis(la_cs, 2, -1)
    seg = la_t[..., :, None] - la_t[..., None, :]
    scan_order = jnp.tril(jnp.ones((Q, Q), dtype=bool))
    decay = jnp.exp(jnp.where(scan_order, seg, -jnp.inf))
    scores = jnp.einsum('bclgn,bcsgn->bcgls', cc, bc)
    y_diag = jnp.einsum('bcgrls,bcsgrp->bclgrp', decay * scores[:, :, :, None], xs)
    y_off = jnp.einsum('bclgn,bcgrpn->bclgrp', cc, h_in) * jnp.exp(la_cs)[..., None]
    return (y_diag + y_off).reshape(b, n, SSM_HEADS, SSM_HEAD_DIM), h_fin


def _seq_flip(t, rev):
    return jnp.flip(t, axis=1) if rev else t


def ssd_mixer(h_lat, h_ctx, w_in, conv_w, conv_b, dt_bias, a_log, d_skip, norm_g, w_out, ctx_out):
    def project(h, with_gate):
        b, n = h.shape[:2]
        proj = h @ (w_in if with_gate else w_in[:, SSM_D_INNER:])
        z = proj[..., :SSM_D_INNER] if with_gate else None
        rest = proj[..., SSM_D_INNER:] if with_gate else proj
        xbc = jax.nn.silu(depthwise_conv(rest[..., :SSM_CONV_DIM], conv_w, conv_b)).astype(jnp.float32)
        xh = xbc[..., :SSM_D_INNER].reshape(b, n, SSM_HEADS, SSM_HEAD_DIM)
        bm = xbc[..., SSM_D_INNER:SSM_D_INNER + SSM_GN].reshape(b, n, SSM_GROUPS, SSM_STATE)
        cm = xbc[..., SSM_D_INNER + SSM_GN:].reshape(b, n, SSM_GROUPS, SSM_STATE)
        dt = jax.nn.softplus(rest[..., SSM_CONV_DIM:].astype(jnp.float32).reshape(b, n, 2, SSM_HEADS)
                             + dt_bias.astype(jnp.float32))
        return z, xh, bm, cm, dt

    z_l, x_l, b_l, c_l, dt_l = project(h_lat, True)
    z_c, x_c, b_c, c_c, dt_c = project(h_ctx, ctx_out)
    a = -jnp.exp(a_log.astype(jnp.float32))
    dsk = d_skip.astype(jnp.float32)
    h0 = jnp.zeros((h_ctx.shape[0], SSM_HEADS, SSM_HEAD_DIM, SSM_STATE), jnp.float32)
    y_l = jnp.zeros_like(x_l)
    y_c = jnp.zeros_like(x_c) if ctx_out else None
    for direction in range(2):
        rev = direction == 1
        yc_d, h_ctx_fin = ssd_chunked(_seq_flip(x_c, rev), _seq_flip(dt_c[:, :, direction], rev), a[direction],
                                      _seq_flip(b_c, rev), _seq_flip(c_c, rev), h0, ctx_out)
        yl_d, _ = ssd_chunked(_seq_flip(x_l, rev), _seq_flip(dt_l[:, :, direction], rev), a[direction],
                              _seq_flip(b_l, rev), _seq_flip(c_l, rev), h_ctx_fin, True)
        y_l = y_l + _seq_flip(yl_d, rev) + dsk[direction][:, None] * x_l
        if ctx_out:
            y_c = y_c + _seq_flip(yc_d, rev) + dsk[direction][:, None] * x_c

    def gated_out(y, z):
        b, n = y.shape[:2]
        g = y.reshape(b, n, SSM_D_INNER) * jax.nn.silu(z.astype(jnp.float32))
        g = g.reshape(b, n, SSM_GROUPS, SSM_D_INNER // SSM_GROUPS)
        g = g * lax.rsqrt(jnp.mean(g * g, axis=-1, keepdims=True) + EPS)
        g = g.reshape(b, n, SSM_D_INNER) * norm_g.astype(jnp.float32)
        return g.astype(h_lat.dtype) @ w_out

    return gated_out(y_l, z_l), (gated_out(y_c, z_c) if ctx_out else None)


def hierarchical_moe(h, w_rg, w_re, w_gu, w_down):
    T, d = h.shape
    hf = h.astype(jnp.float32)
    pg = jax.nn.softmax(hf @ w_rg.astype(jnp.float32), axis=-1)
    gate_g, grp = lax.top_k(pg, 1)
    le = (hf @ w_re.astype(jnp.float32)).reshape(T, MOE_GROUPS, MOE_EXPERTS_PER_GROUP)
    le = le[jnp.arange(T), grp[:, 0]]
    top_v, top_i = lax.top_k(le, MOE_TOPK)
    w_sel = gate_g * jax.nn.softmax(top_v, axis=-1)
    expert = grp * MOE_EXPERTS_PER_GROUP + top_i
    n_assign = T * MOE_TOPK
    e_flat = expert.reshape(-1)
    w_flat = w_sel.reshape(-1).astype(h.dtype)
    tok_flat = jnp.repeat(jnp.arange(T, dtype=jnp.int32), MOE_TOPK)
    order = jnp.argsort(e_flat)
    e_sorted = e_flat[order]
    counts = jnp.zeros((MOE_EXPERTS,), jnp.int32).at[e_flat].add(1)
    starts = jnp.cumsum(counts) - counts
    padded = (counts + MOE_BLOCK - 1) // MOE_BLOCK * MOE_BLOCK
    pends = jnp.cumsum(padded)
    pstarts = pends - padded
    dest = pstarts[e_sorted] + jnp.arange(n_assign, dtype=jnp.int32) - starts[e_sorted]
    n_blocks = -(-n_assign // MOE_BLOCK) + MOE_EXPERTS
    n_rows = n_blocks * MOE_BLOCK
    row_tok = jnp.full((n_rows,), T, jnp.int32).at[dest].set(tok_flat[order])
    row_w = jnp.zeros((n_rows,), h.dtype).at[dest].set(w_flat[order])
    block_expert = jnp.minimum(
        jnp.searchsorted(pends, jnp.arange(n_blocks, dtype=jnp.int32) * MOE_BLOCK, side='right'),
        MOE_EXPERTS - 1)
    h_pad = jnp.concatenate([h, jnp.zeros((1, d), h.dtype)], axis=0)

    def expert_block(args):
        rows, e = args
        gu = h_pad[rows] @ w_gu[e]
        return (jax.nn.silu(gu[:, :MOE_HIDDEN]) * gu[:, MOE_HIDDEN:]) @ w_down[e]

    y = lax.map(expert_block, (row_tok.reshape(n_blocks, MOE_BLOCK), block_expert))
    y = y.reshape(n_rows, d) * row_w[:, None]
    return jnp.zeros_like(h_pad).at[row_tok].add(y)[:T]


def setup_inputs(seed: int = 0) -> dict:
    key = jax.random.key(seed)
    ks = iter(jax.random.split(key, 40))
    D = D_MODEL

    def nrm(shape, scale):
        return jax.random.normal(next(ks), shape, jnp.float32) * scale

    u = jax.random.uniform(next(ks), (N_SSM_LAYERS, 2, SSM_HEADS), jnp.float32)
    dt0 = jnp.exp(u * (math.log(DT_MAX) - math.log(DT_MIN)) + math.log(DT_MIN))
    dt_bias = dt0 + jnp.log(-jnp.expm1(-dt0))
    a_log = jnp.log(jax.random.uniform(next(ks), (N_SSM_LAYERS, 2, SSM_HEADS), jnp.float32, 1.0, 16.0))
    return {
        'x': nrm((BATCH, SEQ, D), 1.0),
        'c': nrm((BATCH, D), 1.0),
        'ctx': nrm((BATCH, CTX_LEN, D), 1.0),
        'c_ctx': nrm((D,), 1.0),
        'w_ada': nrm((DEPTH, D, 6 * D), 0.5 * D ** -0.5),
        'b_ada': nrm((DEPTH, 6 * D), 0.02),
        'norm_mix_g': 1.0 + nrm((DEPTH, D), 0.05),
        'norm_ffn_g': 1.0 + nrm((DEPTH, D), 0.05),
        'final_norm_g': 1.0 + nrm((D,), 0.05),
        'attn_w_qkv': nrm((N_ATTN_LAYERS, D, QKV_DIM), D ** -0.5),
        'attn_w_o': nrm((N_ATTN_LAYERS, N_HEADS * HEAD_DIM, D), (N_HEADS * HEAD_DIM) ** -0.5),
        'attn_q_norm_g': 1.0 + nrm((N_ATTN_LAYERS, HEAD_DIM), 0.05),
        'attn_k_norm_g': 1.0 + nrm((N_ATTN_LAYERS, HEAD_DIM), 0.05),
        'pool_w': nrm((N_POOL_LAYERS, len(POOL_WINDOWS), POOL_GROUP, POOL_GROUP), POOL_GROUP ** -0.5),
        'pool_scale': 1.0 + nrm((N_POOL_LAYERS, D), 0.05),
        'ssm_w_in': nrm((N_SSM_LAYERS, D, SSM_IN_DIM), D ** -0.5),
        'ssm_conv_w': nrm((N_SSM_LAYERS, SSM_CONV, SSM_CONV_DIM), SSM_CONV ** -0.5),
        'ssm_conv_b': nrm((N_SSM_LAYERS, SSM_CONV_DIM), 0.02),
        'ssm_dt_bias': dt_bias,
        'ssm_a_log': a_log,
        'ssm_d': 1.0 + nrm((N_SSM_LAYERS, 2, SSM_HEADS), 0.05),
        'ssm_norm_g': 1.0 + nrm((N_SSM_LAYERS, SSM_D_INNER), 0.05),
        'ssm_w_out': nrm((N_SSM_LAYERS, SSM_D_INNER, D), SSM_D_INNER ** -0.5),
        'moe_w_router_group': nrm((DEPTH, D, MOE_GROUPS), D ** -0.5),
        'moe_w_router_expert': nrm((DEPTH, D, MOE_EXPERTS), D ** -0.5),
        'moe_w_gate_up': nrm((DEPTH, MOE_EXPERTS, D, 2 * MOE_HIDDEN), D ** -0.5),
        'moe_w_down': nrm((DEPTH, MOE_EXPERTS, MOE_HIDDEN, D), MOE_HIDDEN ** -0.5),
    }


def reference(x, c, ctx, c_ctx, w_ada, b_ada, norm_mix_g, norm_ffn_g, final_norm_g,
              attn_w_qkv, attn_w_o, attn_q_norm_g, attn_k_norm_g, pool_w, pool_scale,
              ssm_w_in, ssm_conv_w, ssm_conv_b, ssm_dt_bias, ssm_a_log, ssm_d, ssm_norm_g, ssm_w_out,
              moe_w_router_group, moe_w_router_expert, moe_w_gate_up, moe_w_down):
    b, n, d = x.shape
    m = ctx.shape[1]
    cos, sin = axial_rope_tables(n)
    cond_lat = jax.nn.silu(c)
    cond_ctx = jax.nn.silu(c_ctx)
    x_lat, x_ctx = x, ctx
    for i in range(DEPTH):
        kind, j = i % N_MIXERS, i // N_MIXERS
        ctx_out = i < DEPTH - 1
        ctx_in = ctx_out or kind != 1
        mod_l = (cond_lat @ w_ada[i] + b_ada[i])[:, None, :]
        sh_m, sc_m, g_m, sh_f, sc_f, g_f = jnp.split(mod_l, 6, axis=-1)
        h_l = modulate(rms_norm(x_lat, norm_mix_g[i]), sh_m, sc_m)
        if ctx_in:
            n_mod = 6 if ctx_out else 3
            mc = jnp.split(cond_ctx @ w_ada[i][:, :n_mod * d] + b_ada[i][:n_mod * d], n_mod)
            h_c = modulate(rms_norm(x_ctx, norm_mix_g[i]), mc[0], mc[1])
        if kind == 0:
            o_l, o_c = attention_mixer(h_l, h_c, attn_w_qkv[j], attn_w_o[j], attn_q_norm_g[j],
                                       attn_k_norm_g[j], cos, sin, ctx_out)
        elif kind == 1:
            o_l = pool_mixer(h_l, pool_w[j], pool_scale[j])
            o_c = pool_mixer(h_c, pool_w[j], pool_scale[j]) if ctx_out else None
        else:
            o_l, o_c = ssd_mixer(h_l, h_c, ssm_w_in[j], ssm_conv_w[j], ssm_conv_b[j], ssm_dt_bias[j],
                                 ssm_a_log[j], ssm_d[j], ssm_norm_g[j], ssm_w_out[j], ctx_out)
        x_lat = x_lat + g_m * o_l
        if ctx_out:
            x_ctx = x_ctx + mc[2] * o_c
        tokens = modulate(rms_norm(x_lat, norm_ffn_g[i]), sh_f, sc_f).reshape(b * n, d)
        if ctx_out:
            h_cf = modulate(rms_norm(x_ctx, norm_ffn_g[i]), mc[3], mc[4])
            tokens = jnp.concatenate([tokens, h_cf.reshape(b * m, d)], axis=0)
        f = hierarchical_moe(tokens, moe_w_router_group[i], moe_w_router_expert[i],
                             moe_w_gate_up[i], moe_w_down[i])
        x_lat = x_lat + g_f * f[:b * n].reshape(b, n, d)
        if ctx_out:
            x_ctx = x_ctx + mc[5] * f[b * n:].reshape(b, m, d)
    return rms_norm(x_lat, final_norm_g)
```

```python
import functools

import jax
import jax.numpy as jnp
from jax import lax
from jax.experimental import pallas as pl
from jax.experimental.pallas import tpu as pltpu

F32 = jnp.float32
BF16 = jnp.bfloat16
I32 = jnp.int32
HI = lax.Precision.HIGHEST
EPS = 1e-6
NEG_INF = float("-inf")

TS = 256
LANES = 128
GRID_W = 64
ROPE_THETA = 10000.0
N_MIXERS = 3

N_HEADS = 16
N_KV = 4
HEAD_DIM = 64
Q_PER_KV = N_HEADS // N_KV
ATT_TK = 256

POOL_WINDOWS = (2, 4, 8, 16)
POOL_PAD = 16
ROW_CHUNK = 256

SSM_HEADS = 32
SSM_HEAD_DIM = 64
SSM_GROUPS = 4
SSM_HPG = SSM_HEADS // SSM_GROUPS
SSM_STATE = 128
SSM_CONV = 4
SSM_CHUNK = 128
SSM_D_INNER = SSM_HEADS * SSM_HEAD_DIM
SSM_GN = SSM_GROUPS * SSM_STATE
SSM_CONV_DIM = SSM_D_INNER + 2 * SSM_GN
CONV_PAD = 8

MOE_GROUPS = 4
MOE_EPG = 8
MOE_EXPERTS = MOE_GROUPS * MOE_EPG
MOE_HIDDEN = 512
MOE_BLOCK = 256


def _params(*sem):
    return pltpu.CompilerParams(dimension_semantics=sem)


def _silu(v):
    return v / (1.0 + jnp.exp(-v))


def _normmod(x, g, shift, scale):
    ms = jnp.mean(x * x, axis=-1, keepdims=True)
    return (x * lax.rsqrt(ms + EPS) * g) * (1.0 + scale) + shift


def _ada_kernel(a_ref, w_ref, b_ref, o_ref):
    a = _silu(a_ref[...])
    o_ref[0] = jnp.dot(a, w_ref[0], precision=HI, preferred_element_type=F32) + b_ref[0]


def _ada(c, c_ctx, w_ada, b_ada):
    depth, d, d6 = w_ada.shape
    b = c.shape[0]
    assert b + 1 <= 8
    a = jnp.concatenate([c, c_ctx[None], jnp.zeros((8 - b - 1, d), F32)], axis=0)
    tn = 1536
    return pl.pallas_call(
        _ada_kernel,
        grid=(depth, d6 // tn),
        in_specs=[pl.BlockSpec((8, d), lambda i, j: (0, 0)),
                  pl.BlockSpec((1, d, tn), lambda i, j: (i, 0, j)),
                  pl.BlockSpec((1, 1, tn), lambda i, j: (i, 0, j))],
        out_specs=pl.BlockSpec((1, 8, tn), lambda i, j: (i, 0, j)),
        out_shape=jax.ShapeDtypeStruct((depth, 8, d6), F32),
        compiler_params=_params("arbitrary", "arbitrary"),
        name="ada",
    )(a, w_ada, b_ada.reshape(depth, 1, d6))


def _row_spec(width, t0, col=0):
    return pl.BlockSpec((None, TS, width), lambda b, s: (b, s + t0, col))


def _mod_spec(d6, t0):
    return pl.BlockSpec((None, None, 1, d6), lambda b, s: (b, jnp.minimum(s + t0, 1), 0, 0))


def _full_spec(shape):
    nd = len(shape)
    return pl.BlockSpec(shape, lambda b, s: (0,) * nd)


def _qkv_kernel(x_ref, mod_ref, g_ref, w_ref, gqk_ref, cos_ref, sin_ref, eh_ref, eht_ref,
                qT_ref, k_ref, vT_ref):
    d = x_ref.shape[-1]
    nq = N_HEADS * HEAD_DIM
    nqk = nq + N_KV * HEAD_DIM
    h = _normmod(x_ref[...], g_ref[...], mod_ref[:, 0:d], mod_ref[:, d:2 * d]).astype(BF16)
    qkv = jnp.dot(h, w_ref[...], preferred_element_type=F32)
    qk = qkv[:, :nqk]
    ss = jnp.dot(qk * qk, eh_ref[...], precision=HI, preferred_element_type=F32)
    rinv = lax.rsqrt(ss * (1.0 / HEAD_DIM) + EPS)
    qk = qk * jnp.dot(rinv, eht_ref[...], precision=HI, preferred_element_type=F32) * gqk_ref[...]
    cos = cos_ref[...]
    sin = sin_ref[...]
    lane = lax.broadcasted_iota(I32, (TS, LANES), 1)
    even = (lane & 1) == 0
    blocks = []
    for j in range(nqk // LANES):
        blk = qk[:, LANES * j:LANES * (j + 1)]
        partner = jnp.where(even, pltpu.roll(blk, LANES - 1, 1), pltpu.roll(blk, 1, 1))
        blocks.append(blk * cos + partner * sin)
    q = jnp.concatenate(blocks[:nq // LANES], axis=1)
    k = jnp.concatenate(blocks[nq // LANES:], axis=1)
    qT_ref[...] = q.T.astype(BF16)
    for kv in range(N_KV):
        k_ref[kv] = k[:, HEAD_DIM * kv:HEAD_DIM * (kv + 1)].astype(BF16)
    vT_ref[...] = qkv[:, nqk:].T.astype(BF16)


def _attn_kernel(qT_ref, k_ref, vT_ref, o_ref, m_ref, l_ref, acc_ref, *, t0, n_chunks):
    tq = qT_ref.shape[-1]
    s_idx = pl.program_id(2) + t0
    chunks = jnp.where(s_idx == 0, TS // ATT_TK, n_chunks)
    m_ref[...] = jnp.full(m_ref.shape, -1e30, F32)
    l_ref[...] = jnp.zeros(l_ref.shape, F32)
    acc_ref[...] = jnp.zeros(acc_ref.shape, F32)

    def body(c, carry):
        off = pl.multiple_of(c * ATT_TK, ATT_TK)
        kc = k_ref[pl.ds(off, ATT_TK), :]
        vc = vT_ref[:, pl.ds(off, ATT_TK)]
        for g in range(Q_PER_KV):
            rows = slice(HEAD_DIM * g, HEAD_DIM * (g + 1))
            s = jnp.dot(kc, qT_ref[rows, :], preferred_element_type=F32)
            m_old = m_ref[g:g + 1, :]
            m_new = jnp.maximum(m_old, jnp.max(s, axis=0, keepdims=True))
            p = jnp.exp(s - m_new)
            alpha = jnp.exp(m_old - m_new)
            l_ref[g:g + 1, :] = alpha * l_ref[g:g + 1, :] + jnp.sum(p, axis=0, keepdims=True)
            acc_ref[rows, :] = alpha * acc_ref[rows, :] + jnp.dot(
                vc, p.astype(BF16), preferred_element_type=F32)
            m_ref[g:g + 1, :] = m_new
        return carry

    lax.fori_loop(0, chunks, body, 0)
    for g in range(Q_PER_KV):
        rows = slice(HEAD_DIM * g, HEAD_DIM * (g + 1))
        o_ref[rows, :] = (acc_ref[rows, :] / l_ref[g:g + 1, :]).astype(BF16)


def _oproj_kernel(oT_ref, w_ref, x_ref, mod_ref, out_ref):
    d = x_ref.shape[-1]
    o = oT_ref[...].astype(F32).T.astype(BF16)
    y = jnp.dot(o, w_ref[...], preferred_element_type=F32)
    out_ref[...] = x_ref[...] + mod_ref[:, 2 * d:3 * d] * y


def _rope_tables(n, m):
    rows = n // GRID_W
    row = jnp.broadcast_to(jnp.arange(rows)[:, None], (rows, GRID_W)).reshape(-1).astype(F32)
    col = jnp.broadcast_to(jnp.arange(GRID_W)[None, :], (rows, GRID_W)).reshape(-1).astype(F32)
    n_freq = HEAD_DIM // 4
    inv_freq = ROPE_THETA ** (-jnp.arange(n_freq, dtype=F32) / n_freq)
    ang = jnp.concatenate([row[:, None] * inv_freq, col[:, None] * inv_freq], axis=-1)
    cos = jnp.repeat(jnp.cos(ang), 2, axis=1)
    sign = jnp.tile(jnp.array([-1.0, 1.0], F32), HEAD_DIM // 2)
    sin = jnp.repeat(jnp.sin(ang), 2, axis=1) * sign
    cos = jnp.concatenate([jnp.ones((m, HEAD_DIM), F32), cos], axis=0)
    sin = jnp.concatenate([jnp.zeros((m, HEAD_DIM), F32), sin], axis=0)
    reps = LANES // HEAD_DIM
    return jnp.tile(cos, (1, reps)), jnp.tile(sin, (1, reps))


def _attention_layer(xs, modtab, norm_g, w_qkv, w_o, q_g, k_g, cos_t, sin_t, ctx_out):
    B, S, D = xs.shape
    nt = S // TS
    nq = N_HEADS * HEAD_DIM
    nkv = N_KV * HEAD_DIM
    nqk = nq + nkv
    d6 = modtab.shape[-1]
    gqk = jnp.concatenate([jnp.tile(q_g, N_HEADS) * (HEAD_DIM ** -0.5), jnp.tile(k_g, N_KV)])[None]
    head_of = jnp.arange(nqk) // HEAD_DIM
    eh = (head_of[:, None] == jnp.arange(LANES)[None, :]).astype(F32)
    qT, k4, vT = pl.pallas_call(
        _qkv_kernel,
        grid=(B, nt),
        in_specs=[_row_spec(D, 0), _mod_spec(d6, 0), _full_spec((1, D)), _full_spec((D, nqk + nkv)),
                  _full_spec((1, nqk)),
                  pl.BlockSpec((TS, LANES), lambda b, s: (s, 0)),
                  pl.BlockSpec((TS, LANES), lambda b, s: (s, 0)),
                  _full_spec((nqk, LANES)), _full_spec((LANES, nqk))],
        out_specs=[pl.BlockSpec((None, nq, TS), lambda b, s: (b, 0, s)),
                   pl.BlockSpec((None, N_KV, TS, HEAD_DIM), lambda b, s: (b, 0, s, 0)),
                   pl.BlockSpec((None, nkv, TS), lambda b, s: (b, 0, s))],
        out_shape=[jax.ShapeDtypeStruct((B, nq, S), BF16),
                   jax.ShapeDtypeStruct((B, N_KV, S, HEAD_DIM), BF16),
                   jax.ShapeDtypeStruct((B, nkv, S), BF16)],
        compiler_params=_params("arbitrary", "arbitrary"),
        name="attn_qkv",
    )(xs, modtab, norm_g[None], w_qkv.astype(BF16), gqk, cos_t, sin_t, eh, eh.T)

    t0 = 0 if ctx_out else 1
    gw = Q_PER_KV * HEAD_DIM
    oT = pl.pallas_call(
        functools.partial(_attn_kernel, t0=t0, n_chunks=S // ATT_TK),
        grid=(B, N_KV, nt - t0),
        in_specs=[pl.BlockSpec((None, gw, TS), lambda b, kv, s: (b, kv, s + t0)),
                  pl.BlockSpec((None, None, S, HEAD_DIM), lambda b, kv, s: (b, kv, 0, 0)),
                  pl.BlockSpec((None, HEAD_DIM, S), lambda b, kv, s: (b, kv, 0))],
        out_specs=pl.BlockSpec((None, gw, TS), lambda b, kv, s: (b, kv, s + t0)),
        out_shape=jax.ShapeDtypeStruct((B, nq, S), BF16),
        scratch_shapes=[pltpu.VMEM((8, TS), F32), pltpu.VMEM((8, TS), F32), pltpu.VMEM((gw, TS), F32)],
        compiler_params=_params("arbitrary", "arbitrary", "arbitrary"),
        name="attn_core",
    )(qT, k4, vT)

    return pl.pallas_call(
        _oproj_kernel,
        grid=(B, nt - t0),
        in_specs=[pl.BlockSpec((None, nq, TS), lambda b, s: (b, 0, s + t0)), _full_spec((nq, D)),
                  _row_spec(D, t0), _mod_spec(d6, t0)],
        out_specs=_row_spec(D, t0),
        out_shape=jax.ShapeDtypeStruct((B, S, D), F32),
        input_output_aliases={2: 0},
        compiler_params=_params("arbitrary", "arbitrary"),
        name="attn_oproj",
    )(oT, w_o.astype(BF16), xs, modtab)


def _normmod_kernel(x_ref, mod_ref, g_ref, h_ref):
    d = x_ref.shape[-1]
    h_ref[...] = _normmod(x_ref[...], g_ref[...], mod_ref[:, 0:d], mod_ref[:, d:2 * d])


def _pool_kernel(h_ref, x_ref, w_ref, ps_ref, gate_ref, out_ref, pad_ref, *, segs):
    gi = pl.program_id(1)
    zeros = jnp.zeros((POOL_PAD, h_ref.shape[-1]), F32)
    for widx, win in enumerate(POOL_WINDOWS):
        half = win // 2

        @pl.when(gi == widx)
        def _(half=half):
            for si, (r0, n) in enumerate(segs):
                pad_ref[0:POOL_PAD, :] = zeros
                pad_ref[POOL_PAD:POOL_PAD + n, :] = h_ref[r0:r0 + n, :]
                pad_ref[POOL_PAD + n:2 * POOL_PAD + n, :] = zeros
                gate = gate_ref[si]
                for c0 in range(0, n, ROW_CHUNK):
                    base = POOL_PAD + c0
                    acc = pad_ref[base - half:base - half + ROW_CHUNK, :]
                    for j in range(-half + 1, half):
                        acc = acc + pad_ref[base + j:base + j + ROW_CHUNK, :]
                    t = c0 + lax.broadcasted_iota(I32, (ROW_CHUNK, 1), 0)
                    cnt = jnp.minimum(t + half, n) - jnp.maximum(t - half, 0)
                    diff = acc / cnt.astype(F32) - pad_ref[base:base + ROW_CHUNK, :]
                    y = jnp.dot(diff.astype(BF16), w_ref[...], preferred_element_type=F32) * ps_ref[...]
                    rows = slice(r0 + c0, r0 + c0 + ROW_CHUNK)
                    out_ref[rows, :] = x_ref[rows, :] + gate * y


def _pool_layer(xs, modtab, norm_g, w_pool, pool_scale, m):
    B, S, D = xs.shape
    nt = S // TS
    d6 = modtab.shape[-1]
    pg = D // len(POOL_WINDOWS)
    h = pl.pallas_call(
        _normmod_kernel,
        grid=(B, nt),
        in_specs=[_row_spec(D, 0), _mod_spec(d6, 0), _full_spec((1, D))],
        out_specs=_row_spec(D, 0),
        out_shape=jax.ShapeDtypeStruct((B, S, D), F32),
        compiler_params=_params("arbitrary", "arbitrary"),
        name="pool_normmod",
    )(xs, modtab, norm_g[None])
    segs = ((0, m), (m, S - m))
    seq_spec = pl.BlockSpec((None, S, pg), lambda b, g: (b, 0, g))
    return pl.pallas_call(
        functools.partial(_pool_kernel, segs=segs),
        grid=(B, len(POOL_WINDOWS)),
        in_specs=[seq_spec, seq_spec,
                  pl.BlockSpec((None, pg, pg), lambda b, g: (g, 0, 0)),
                  pl.BlockSpec((1, pg), lambda b, g: (0, g)),
                  pl.BlockSpec((None, 2, 1, pg), lambda b, g: (b, 0, 0, 2 * (D // pg) + g))],
        out_specs=seq_spec,
        out_shape=jax.ShapeDtypeStruct((B, S, D), F32),
        scratch_shapes=[pltpu.VMEM((S - m + 2 * POOL_PAD, pg), F32)],
        compiler_params=_params("arbitrary", "arbitrary"),
        name="pool_mix",
    )(h, xs, w_pool.astype(BF16), pool_scale[None], modtab)


def _ssm_in_kernel(x_ref, mod_ref, g_ref, wz_ref, wx_ref, wdt_ref, z_ref, xbc_ref, dt_ref):
    d = x_ref.shape[-1]
    hf = _normmod(x_ref[...], g_ref[...], mod_ref[:, 0:d], mod_ref[:, d:2 * d])
    h = hf.astype(BF16)
    z_ref[...] = jnp.dot(h, wz_ref[...], preferred_element_type=F32)
    xbc_ref[...] = jnp.dot(h, wx_ref[...], preferred_element_type=F32)
    dt_ref[...] = jnp.dot(hf, wdt_ref[...], precision=HI, preferred_element_type=F32)


def _conv_kernel(u_ref, w_ref, b_ref, o_ref, pad_ref, *, segs):
    zeros = jnp.zeros((CONV_PAD, u_ref.shape[-1]), F32)
    for r0, n in segs:
        pad_ref[0:CONV_PAD, :] = zeros
        pad_ref[CONV_PAD:CONV_PAD + n, :] = u_ref[r0:r0 + n, :]
        pad_ref[CONV_PAD + n:2 * CONV_PAD + n, :] = zeros
        for c0 in range(0, n, ROW_CHUNK):
            acc = b_ref[...]
            for k in range(SSM_CONV):
                lo = CONV_PAD + c0 - SSM_CONV // 2 + k
                acc = acc + w_ref[k:k + 1, :] * pad_ref[lo:lo + ROW_CHUNK, :]
            o_ref[r0 + c0:r0 + c0 + ROW_CHUNK, :] = _silu(acc)


def _expand_heads(v, lane0):
    r = v.shape[0]
    lane = lax.broadcasted_iota(I32, (r, LANES), 1)
    blocks = []
    for j in range(SSM_D_INNER // LANES):
        a = jnp.broadcast_to(v[:, lane0 + 2 * j:lane0 + 2 * j + 1], (r, LANES))
        b = jnp.broadcast_to(v[:, lane0 + 2 * j + 1:lane0 + 2 * j + 2], (r, LANES))
        blocks.append(jnp.where(lane < SSM_HEAD_DIM, a, b))
    return jnp.concatenate(blocks, axis=1)


def _ssd_kernel(xbc_ref, dt_ref, dtb_ref, a_ref, tri_ref, y_ref, h_ref, *, direction):
    q = SSM_CHUNK
    lane0 = SSM_HEADS * direction

    @pl.when(pl.program_id(1) == 0)
    def _():
        h_ref[...] = jnp.zeros(h_ref.shape, F32)

    raw = dt_ref[...] + dtb_ref[...]
    dt = jnp.maximum(raw, 0.0) + jnp.log1p(jnp.exp(-jnp.abs(raw)))
    la = dt * a_ref[...]
    cs = jnp.dot(tri_ref[...], la, precision=HI, preferred_element_type=F32)
    tot = cs[q - 1:q, :]
    if direction == 0:
        u = cs
        dec_in = jnp.exp(u)
        dec_end = jnp.exp(tot - u)
    else:
        u = la - cs
        dec_in = jnp.exp(tot + u)
        dec_end = jnp.exp(-u)
    uT = u.T
    xs = xbc_ref[:, 0:SSM_D_INNER] * _expand_heads(dt, lane0)
    xs_b = xs.astype(BF16)
    xd_b = (xs * _expand_heads(dec_end, lane0)).astype(BF16)
    din_col = _expand_heads(dec_in, lane0)
    dtot_col = _expand_heads(jnp.exp(tot), lane0)
    row = lax.broadcasted_iota(I32, (q, q), 0)
    col = lax.broadcasted_iota(I32, (q, q), 1)
    mask = (row >= col) if direction == 0 else (col >= row)
    gw = SSM_HPG * SSM_HEAD_DIM
    for g in range(SSM_GROUPS):
        b_lo = SSM_D_INNER + SSM_STATE * g
        c_lo = SSM_D_INNER + SSM_GN + SSM_STATE * g
        bT = xbc_ref[:, b_lo:b_lo + SSM_STATE].T.astype(BF16)
        cg = xbc_ref[:, c_lo:c_lo + SSM_STATE].astype(BF16)
        scores = jnp.dot(cg, bT, preferred_element_type=F32)
        hg = h_ref[g]
        cols = slice(gw * g, gw * (g + 1))
        y_off = jnp.dot(cg, hg.astype(BF16), preferred_element_type=F32) * din_col[:, cols]
        for r in range(SSM_HPG):
            hh = SSM_HPG * g + r
            hl = lane0 + hh
            seg = u[:, hl:hl + 1] - uT[hl:hl + 1, :]
            decay = jnp.exp(jnp.where(mask, seg, NEG_INF))
            hc = slice(SSM_HEAD_DIM * hh, SSM_HEAD_DIM * (hh + 1))
            y_d = jnp.dot((scores * decay).astype(BF16), xs_b[:, hc], preferred_element_type=F32)
            y_ref[:, hc] = y_d + y_off[:, SSM_HEAD_DIM * r:SSM_HEAD_DIM * (r + 1)]
        h_ref[g] = hg * dtot_col[:, cols] + jnp.dot(bT, xd_b[:, cols], preferred_element_type=F32)


def _ssm_out_kernel(y0_ref, y1_ref, xh_ref, z_ref, dsk_ref, ng_ref, w_ref, x_ref, mod_ref, out_ref):
    d = x_ref.shape[-1]
    y = y0_ref[...] + y1_ref[...] + dsk_ref[...] * xh_ref[...]
    g = y * _silu(z_ref[...])
    gsz = SSM_D_INNER // SSM_GROUPS
    parts = []
    for k in range(SSM_GROUPS):
        gk = g[:, gsz * k:gsz * (k + 1)]
        parts.append(gk * lax.rsqrt(jnp.mean(gk * gk, axis=-1, keepdims=True) + EPS))
    gn = (jnp.concatenate(parts, axis=1) * ng_ref[...]).astype(BF16)
    o = jnp.dot(gn, w_ref[...], preferred_element_type=F32)
    out_ref[...] = x_ref[...] + mod_ref[:, 2 * d:3 * d] * o


def _ssd_layer(xs, modtab, norm_g, w_in, conv_w, conv_b, dt_bias, a_log, d_skip, ssm_norm_g, w_out, m):
    B, S, D = xs.shape
    nt = S // TS
    d6 = modtab.shape[-1]
    di = SSM_D_INNER
    cd = SSM_CONV_DIM
    wz = w_in[:, :di].astype(BF16)
    wx = w_in[:, di:di + cd].astype(BF16)
    wdt = jnp.pad(w_in[:, di + cd:], ((0, 0), (0, LANES - 2 * SSM_HEADS)))
    z, xbc_raw, dt_raw = pl.pallas_call(
        _ssm_in_kernel,
        grid=(B, nt),
        in_specs=[_row_spec(D, 0), _mod_spec(d6, 0), _full_spec((1, D)), _full_spec((D, di)),
                  _full_spec((D, cd)), _full_spec((D, LANES))],
        out_specs=[_row_spec(di, 0), _row_spec(cd, 0), _row_spec(LANES, 0)],
        out_shape=[jax.ShapeDtypeStruct((B, S, di), F32), jax.ShapeDtypeStruct((B, S, cd), F32),
                   jax.ShapeDtypeStruct((B, S, LANES), F32)],
        compiler_params=_params("arbitrary", "arbitrary"),
        name="ssm_in",
    )(xs, modtab, norm_g[None], wz, wx, wdt)

    segs = ((0, m), (m, S - m))
    cw = 256
    seq_spec = pl.BlockSpec((None, S, cw), lambda b, j: (b, 0, j))
    xbc = pl.pallas_call(
        functools.partial(_conv_kernel, segs=segs),
        grid=(B, cd // cw),
        in_specs=[seq_spec, pl.BlockSpec((SSM_CONV, cw), lambda b, j: (0, j)),
                  pl.BlockSpec((1, cw), lambda b, j: (0, j))],
        out_specs=seq_spec,
        out_shape=jax.ShapeDtypeStruct((B, S, cd), F32),
        scratch_shapes=[pltpu.VMEM((S - m + 2 * CONV_PAD, cw), F32)],
        compiler_params=_params("arbitrary", "arbitrary"),
        name="ssm_conv",
    )(xbc_raw, conv_w, conv_b[None])

    q = SSM_CHUNK
    nc = S // q
    mc = m // q
    pad = LANES - 2 * SSM_HEADS
    dtb = jnp.pad(dt_bias.reshape(-1), (0, pad))[None]
    a_neg = jnp.pad(-jnp.exp(a_log.reshape(-1)), (0, pad))[None]
    tri = (jnp.arange(q)[:, None] >= jnp.arange(q)[None, :]).astype(F32)
    ys = []
    for direction in range(2):
        if direction == 0:
            cmap = lambda c: c
        else:
            cmap = lambda c: jnp.where(c < mc, mc - 1 - c, nc - 1 - (c - mc))
        ys.append(pl.pallas_call(
            functools.partial(_ssd_kernel, direction=direction),
            grid=(B, nc),
            in_specs=[pl.BlockSpec((None, q, cd), lambda b, c, cmap=cmap: (b, cmap(c), 0)),
                      pl.BlockSpec((None, q, LANES), lambda b, c, cmap=cmap: (b, cmap(c), 0)),
                      _full_spec((1, LANES)), _full_spec((1, LANES)), _full_spec((q, q))],
            out_specs=pl.BlockSpec((None, q, di), lambda b, c, cmap=cmap: (b, cmap(c), 0)),
            out_shape=jax.ShapeDtypeStruct((B, S, di), F32),
            scratch_shapes=[pltpu.VMEM((SSM_GROUPS, SSM_STATE, SSM_HPG * SSM_HEAD_DIM), F32)],
            compiler_params=_params("arbitrary", "arbitrary"),
            name=f"ssd_scan{direction}",
        )(xbc, dt_raw, dtb, a_neg, tri))

    dsk = jnp.repeat(d_skip[0] + d_skip[1], SSM_HEAD_DIM)[None]
    return pl.pallas_call(
        _ssm_out_kernel,
        grid=(B, nt),
        in_specs=[_row_spec(di, 0), _row_spec(di, 0), _row_spec(di, 0), _row_spec(di, 0),
                  _full_spec((1, di)), _full_spec((1, di)), _full_spec((di, D)),
                  _row_spec(D, 0), _mod_spec(d6, 0)],
        out_specs=_row_spec(D, 0),
        out_shape=jax.ShapeDtypeStruct((B, S, D), F32),
        input_output_aliases={7: 0},
        compiler_params=_params("arbitrary", "arbitrary"),
        name="ssm_out",
    )(ys[0], ys[1], xbc, z, dsk, ssm_norm_g[None], w_out.astype(BF16), xs, modtab)


def _router_kernel(x_ref, mod_ref, g_ref, wr_ref, tok_ref, e_ref, wt_ref):
    d = x_ref.shape[-1]
    t = _normmod(x_ref[...], g_ref[...], mod_ref[:, 3 * d:4 * d], mod_ref[:, 4 * d:5 * d])
    tok_ref[...] = t.astype(BF16)
    logits = jnp.dot(t, wr_ref[...], precision=HI, preferred_element_type=F32)
    lane = lax.broadcasted_iota(I32, logits.shape, 1).astype(F32)
    big = float(LANES)
    gl = jnp.where(lane < MOE_GROUPS, logits, NEG_INF)
    gmax = jnp.max(gl, axis=1, keepdims=True)
    gate = 1.0 / jnp.sum(jnp.exp(gl - gmax), axis=1, keepdims=True)
    grp = jnp.min(jnp.where(gl == gmax, lane, big), axis=1, keepdims=True)
    lo = MOE_GROUPS + MOE_EPG * grp
    el = jnp.where((lane >= lo) & (lane < lo + MOE_EPG), logits, NEG_INF)
    v1 = jnp.max(el, axis=1, keepdims=True)
    i1 = jnp.min(jnp.where(el == v1, lane, big), axis=1, keepdims=True)
    el2 = jnp.where(lane == i1, NEG_INF, el)
    v2 = jnp.max(el2, axis=1, keepdims=True)
    i2 = jnp.min(jnp.where(el2 == v2, lane, big), axis=1, keepdims=True)
    e2 = jnp.exp(v2 - v1)
    den = 1.0 + e2
    w1 = gate * (1.0 / den)
    w2 = gate * (e2 / den)
    e_ref[...] = jnp.where(lane == 0.0, i1 - MOE_GROUPS,
                           jnp.where(lane == 1.0, i2 - MOE_GROUPS, 0.0)).astype(I32)
    wt_ref[...] = jnp.where(lane == 0.0, w1, jnp.where(lane == 1.0, w2, 0.0))


def _expert_kernel(be_ref, nu_ref, xs_ref, wgu_ref, wd_ref, y_ref):
    i = pl.program_id(0)

    @pl.when(i < nu_ref[0])
    def _():
        gu = jnp.dot(xs_ref[...], wgu_ref[...], preferred_element_type=F32)
        hid = _silu(gu[:, :MOE_HIDDEN]) * gu[:, MOE_HIDDEN:]
        y_ref[...] = jnp.dot(hid.astype(BF16), wd_ref[...], preferred_element_type=F32)

    @pl.when(i >= nu_ref[0])
    def _():
        y_ref[...] = jnp.zeros(y_ref.shape, F32)


def _combine_kernel(x_ref, mod_ref, y0_ref, y1_ref, wt_ref, out_ref):
    d = x_ref.shape[-1]
    f = wt_ref[:, 0:1] * y0_ref[...] + wt_ref[:, 1:2] * y1_ref[...]
    out_ref[...] = x_ref[...] + mod_ref[:, 5 * d:6 * d] * f


def _moe_layer(xs, modtab, norm_g, w_rg, w_re, w_gu, w_down, ctx_out):
    B, S, D = xs.shape
    d6 = modtab.shape[-1]
    t0 = 0 if ctx_out else 1
    nt = S // TS - t0
    R = nt * TS
    wr = jnp.pad(jnp.concatenate([w_rg, w_re], axis=1), ((0, 0), (0, LANES - MOE_GROUPS - MOE_EXPERTS)))
    out_row = lambda width: pl.BlockSpec((None, TS, width), lambda b, s: (b, s, 0))
    tok, e_out, wt = pl.pallas_call(
        _router_kernel,
        grid=(B, nt),
        in_specs=[_row_spec(D, t0), _mod_spec(d6, t0), _full_spec((1, D)), _full_spec((D, LANES))],
        out_specs=[out_row(D), out_row(LANES), out_row(LANES)],
        out_shape=[jax.ShapeDtypeStruct((B, R, D), BF16), jax.ShapeDtypeStruct((B, R, LANES), I32),
                   jax.ShapeDtypeStruct((B, R, LANES), F32)],
        compiler_params=_params("arbitrary", "arbitrary"),
        name="moe_router",
    )(xs, modtab, norm_g[None], wr)

    T = B * R
    A = 2 * T
    e_flat = e_out[:, :, :2].reshape(A)
    order = jnp.argsort(e_flat)
    e_sorted = e_flat[order]
    counts = jnp.zeros((MOE_EXPERTS,), I32).at[e_flat].add(1)
    starts = jnp.cumsum(counts) - counts
    padded = (counts + MOE_BLOCK - 1) // MOE_BLOCK * MOE_BLOCK
    pends = jnp.cumsum(padded)
    pstarts = pends - padded
    dest_sorted = pstarts[e_sorted] + jnp.arange(A, dtype=I32) - starts[e_sorted]
    n_blocks = -(-A // MOE_BLOCK) + MOE_EXPERTS
    n_rows = n_blocks * MOE_BLOCK
    row_tok = jnp.zeros((n_rows,), I32).at[dest_sorted].set((order // 2).astype(I32))
    pos = jnp.zeros((A,), I32).at[order].set(dest_sorted).reshape(T, 2)
    block_expert = jnp.minimum(
        jnp.searchsorted(pends, jnp.arange(n_blocks, dtype=I32) * MOE_BLOCK, side="right"),
        MOE_EXPERTS - 1).astype(I32)
    n_used = (pends[-1:] // MOE_BLOCK).astype(I32)

    x_sorted = tok.reshape(T, D)[row_tok]
    y_sorted = pl.pallas_call(
        _expert_kernel,
        grid_spec=pltpu.PrefetchScalarGridSpec(
            num_scalar_prefetch=2,
            grid=(n_blocks,),
            in_specs=[pl.BlockSpec((MOE_BLOCK, D), lambda i, be, nu: (i, 0)),
                      pl.BlockSpec((None, D, 2 * MOE_HIDDEN), lambda i, be, nu: (be[i], 0, 0)),
                      pl.BlockSpec((None, MOE_HIDDEN, D), lambda i, be, nu: (be[i], 0, 0))],
            out_specs=pl.BlockSpec((MOE_BLOCK, D), lambda i, be, nu: (i, 0)),
        ),
        out_shape=jax.ShapeDtypeStruct((n_rows, D), F32),
        compiler_params=_params("arbitrary"),
        name="moe_experts",
    )(block_expert, n_used, x_sorted, w_gu.astype(BF16), w_down.astype(BF16))

    y0 = y_sorted[pos[:, 0]].reshape(B, R, D)
    y1 = y_sorted[pos[:, 1]].reshape(B, R, D)
    return pl.pallas_call(
        _combine_kernel,
        grid=(B, nt),
        in_specs=[_row_spec(D, t0), _mod_spec(d6, t0), out_row(D), out_row(D), out_row(LANES)],
        out_specs=_row_spec(D, t0),
        out_shape=jax.ShapeDtypeStruct((B, S, D), F32),
        input_output_aliases={0: 0},
        compiler_params=_params("arbitrary", "arbitrary"),
        name="moe_combine",
    )(xs, modtab, y0, y1, wt)


def _final_kernel(x_ref, g_ref, o_ref):
    x = x_ref[...]
    ms = jnp.mean(x * x, axis=-1, keepdims=True)
    o_ref[...] = x * lax.rsqrt(ms + EPS) * g_ref[...]


def kernel(x, c, ctx, c_ctx, w_ada, b_ada, norm_mix_g, norm_ffn_g, final_norm_g, attn_w_qkv, attn_w_o, attn_q_norm_g, attn_k_norm_g, pool_w, pool_scale, ssm_w_in, ssm_conv_w, ssm_conv_b, ssm_dt_bias, ssm_a_log, ssm_d, ssm_norm_g, ssm_w_out, moe_w_router_group, moe_w_router_expert, moe_w_gate_up, moe_w_down):
    B, n, D = x.shape
    m = ctx.shape[1]
    depth = w_ada.shape[0]
    assert m == TS and n % TS == 0 and n % GRID_W == 0
    xs = jnp.concatenate([ctx, x], axis=1)
    mods = _ada(c, c_ctx, w_ada, b_ada)
    cos_t, sin_t = _rope_tables(n, m)
    for i in range(depth):
        kind, j = i % N_MIXERS, i // N_MIXERS
        ctx_out = i < depth - 1
        modtab = jnp.stack([jnp.broadcast_to(mods[i, B], (B, 6 * D)), mods[i, :B]], axis=1)[:, :, None, :]
        if kind == 0:
            xs = _attention_layer(xs, modtab, norm_mix_g[i], attn_w_qkv[j], attn_w_o[j], attn_q_norm_g[j],
                                  attn_k_norm_g[j], cos_t, sin_t, ctx_out)
        elif kind == 1:
            assert ctx_out
            xs = _pool_layer(xs, modtab, norm_mix_g[i], pool_w[j], pool_scale[j], m)
        else:
            assert ctx_out
            xs = _ssd_layer(xs, modtab, norm_mix_g[i], ssm_w_in[j], ssm_conv_w[j], ssm_conv_b[j],
                            ssm_dt_bias[j], ssm_a_log[j], ssm_d[j], ssm_norm_g[j], ssm_w_out[j], m)
        xs = _moe_layer(xs, modtab, norm_ffn_g[i], moe_w_router_group[i], moe_w_router_expert[i],
                        moe_w_gate_up[i], moe_w_down[i], ctx_out)
    return pl.pallas_call(
        _final_kernel,
        grid=(B, n // TS),
        in_specs=[_row_spec(D, m // TS), _full_spec((1, D))],
        out_specs=pl.BlockSpec((None, TS, D), lambda b, s: (b, s, 0)),
        out_shape=jax.ShapeDtypeStruct((B, n, D), F32),
        compiler_params=_params("arbitrary", "arbitrary"),
        name="final_norm",
    )(xs, final_norm_g[None])
```

```python
import functools

import jax
import jax.numpy as jnp
from jax import lax
from jax.experimental import pallas as pl
from jax.experimental.pallas import tpu as pltpu

F32 = jnp.float32
BF16 = jnp.bfloat16
I32 = jnp.int32
HI = lax.Precision.HIGHEST
EPS = 1e-6
NEG_INF = float("-inf")
LOG2E = 1.4426950408889634

TS = 256
LANES = 128
GRID_W = 64
ROPE_THETA = 10000.0
N_MIXERS = 3

N_HEADS = 16
N_KV = 4
HEAD_DIM = 64
Q_PER_KV = N_HEADS // N_KV

POOL_WINDOWS = (2, 4, 8, 16)
POOL_PAD = 16
ROW_CHUNK = 256

SSM_HEADS = 32
SSM_HEAD_DIM = 64
SSM_GROUPS = 4
SSM_HPG = SSM_HEADS // SSM_GROUPS
SSM_STATE = 128
SSM_CONV = 4
SSM_CHUNK = 128
SSM_D_INNER = SSM_HEADS * SSM_HEAD_DIM
SSM_GN = SSM_GROUPS * SSM_STATE
SSM_CONV_DIM = SSM_D_INNER + 2 * SSM_GN
CONV_PAD = 8

MOE_GROUPS = 4
MOE_EPG = 8
MOE_EXPERTS = MOE_GROUPS * MOE_EPG
MOE_HIDDEN = 512
MOE_BLOCK = 256


def _params(*sem):
    return pltpu.CompilerParams(dimension_semantics=sem)


def _silu(v):
    return v / (1.0 + jnp.exp(-v))


def _normmod(x, g, shift, scale):
    ms = jnp.mean(x * x, axis=-1, keepdims=True)
    return (x * lax.rsqrt(ms + EPS) * g) * (1.0 + scale) + shift


def _ada_kernel(a_ref, w_ref, b_ref, o_ref):
    a = _silu(a_ref[...])
    o_ref[0] = jnp.dot(a, w_ref[0], precision=HI, preferred_element_type=F32) + b_ref[0]


def _ada(c, c_ctx, w_ada, b_ada):
    depth, d, d6 = w_ada.shape
    b = c.shape[0]
    assert b + 1 <= 8
    a = jnp.concatenate([c, c_ctx[None], jnp.zeros((8 - b - 1, d), F32)], axis=0)
    tn = 1536
    return pl.pallas_call(
        _ada_kernel,
        grid=(depth, d6 // tn),
        in_specs=[pl.BlockSpec((8, d), lambda i, j: (0, 0)),
                  pl.BlockSpec((1, d, tn), lambda i, j: (i, 0, j)),
                  pl.BlockSpec((1, 1, tn), lambda i, j: (i, 0, j))],
        out_specs=pl.BlockSpec((1, 8, tn), lambda i, j: (i, 0, j)),
        out_shape=jax.ShapeDtypeStruct((depth, 8, d6), F32),
        compiler_params=_params("arbitrary", "arbitrary"),
        name="ada",
    )(a, w_ada, b_ada.reshape(depth, 1, d6))


def _row_spec(width, t0, col=0):
    return pl.BlockSpec((None, TS, width), lambda b, s: (b, s + t0, col))


def _mod_spec(d6, t0):
    return pl.BlockSpec((None, None, 1, d6), lambda b, s: (b, jnp.minimum(s + t0, 1), 0, 0))


def _full_spec(shape):
    nd = len(shape)
    return pl.BlockSpec(shape, lambda b, s: (0,) * nd)


def _qkv_kernel(x_ref, mod_ref, g_ref, w_ref, gqk_ref, cos_ref, sin_ref, eh_ref, eht_ref,
                qT_ref, k_ref, vT_ref):
    d = x_ref.shape[-1]
    nq = N_HEADS * HEAD_DIM
    nqk = nq + N_KV * HEAD_DIM
    h = _normmod(x_ref[...], g_ref[...], mod_ref[:, 0:d], mod_ref[:, d:2 * d]).astype(BF16)
    qkv = jnp.dot(h, w_ref[...], preferred_element_type=F32)
    qk = qkv[:, :nqk]
    ss = jnp.dot(qk * qk, eh_ref[...], precision=HI, preferred_element_type=F32)
    rinv = lax.rsqrt(ss * (1.0 / HEAD_DIM) + EPS)
    qk = qk * jnp.dot(rinv, eht_ref[...], precision=HI, preferred_element_type=F32) * gqk_ref[...]
    cos = cos_ref[...]
    sin = sin_ref[...]
    lane = lax.broadcasted_iota(I32, (TS, LANES), 1)
    even = (lane & 1) == 0
    blocks = []
    for j in range(nqk // LANES):
        blk = qk[:, LANES * j:LANES * (j + 1)]
        partner = jnp.where(even, pltpu.roll(blk, LANES - 1, 1), pltpu.roll(blk, 1, 1))
        blocks.append(blk * cos + partner * sin)
    q = jnp.concatenate(blocks[:nq // LANES], axis=1)
    k = jnp.concatenate(blocks[nq // LANES:], axis=1)
    qT_ref[...] = q.T.astype(BF16)
    for kv in range(N_KV):
        k_ref[kv] = k[:, HEAD_DIM * kv:HEAD_DIM * (kv + 1)].astype(BF16)
    vT_ref[...] = qkv[:, nqk:].T.astype(BF16)


def _attn_ctx_kernel(qT_ref, k_ref, vT_ref, o_ref):
    for g in range(Q_PER_KV):
        rows = slice(HEAD_DIM * g, HEAD_DIM * (g + 1))
        s = jnp.dot(k_ref[...], qT_ref[rows, :], preferred_element_type=F32)
        p = jnp.exp2(s - jnp.max(s, axis=0, keepdims=True))
        l = jnp.sum(p, axis=0, keepdims=True)
        o = jnp.dot(vT_ref[...], p.astype(BF16), preferred_element_type=F32)
        o_ref[rows, :] = (o / l).astype(BF16)


def _attn_main_kernel(qT_ref, k_ref, vT_ref, o_ref, s0_ref, s1_ref, m0_ref, m1_ref, *, n_units, q_col0):
    for ref in (s0_ref, s1_ref, m0_ref, m1_ref):
        ref[...] = jnp.zeros(ref.shape, F32)

    def unit(i):
        i = jnp.clip(i, 0, n_units - 1)
        row = pl.multiple_of((i % Q_PER_KV) * HEAD_DIM, HEAD_DIM)
        col = pl.multiple_of((i // Q_PER_KV) * TS, TS)
        return row, col

    def step(i, s_w, m_w, s_r, m_r):
        row, col = unit(i)
        q = qT_ref[pl.ds(row, HEAD_DIM), pl.ds(q_col0 + col, TS)]
        s = jnp.dot(k_ref[...], q, preferred_element_type=F32)
        s_w[...] = s
        m_w[...] = jnp.max(s, axis=0, keepdims=True)
        row, col = unit(i - 1)
        p = jnp.exp2(s_r[...] - m_r[...])
        l = jnp.sum(p, axis=0, keepdims=True)
        o = jnp.dot(vT_ref[...], p.astype(BF16), preferred_element_type=F32)
        o_ref[pl.ds(row, HEAD_DIM), pl.ds(col, TS)] = (o / l).astype(BF16)

    def body(j, carry):
        step(2 * j, s0_ref, m0_ref, s1_ref, m1_ref)
        step(2 * j + 1, s1_ref, m1_ref, s0_ref, m0_ref)
        return carry

    lax.fori_loop(0, n_units // 2 + 1, body, 0)


def _oproj_kernel(oT_ref, w_ref, x_ref, mod_ref, out_ref):
    d = x_ref.shape[-1]
    o = oT_ref[...].astype(F32).T.astype(BF16)
    y = jnp.dot(o, w_ref[...], preferred_element_type=F32)
    out_ref[...] = x_ref[...] + mod_ref[:, 2 * d:3 * d] * y


def _rope_tables(n, m):
    rows = n // GRID_W
    row = jnp.broadcast_to(jnp.arange(rows)[:, None], (rows, GRID_W)).reshape(-1).astype(F32)
    col = jnp.broadcast_to(jnp.arange(GRID_W)[None, :], (rows, GRID_W)).reshape(-1).astype(F32)
    n_freq = HEAD_DIM // 4
    inv_freq = ROPE_THETA ** (-jnp.arange(n_freq, dtype=F32) / n_freq)
    ang = jnp.concatenate([row[:, None] * inv_freq, col[:, None] * inv_freq], axis=-1)
    cos = jnp.repeat(jnp.cos(ang), 2, axis=1)
    sign = jnp.tile(jnp.array([-1.0, 1.0], F32), HEAD_DIM // 2)
    sin = jnp.repeat(jnp.sin(ang), 2, axis=1) * sign
    cos = jnp.concatenate([jnp.ones((m, HEAD_DIM), F32), cos], axis=0)
    sin = jnp.concatenate([jnp.zeros((m, HEAD_DIM), F32), sin], axis=0)
    reps = LANES // HEAD_DIM
    return jnp.tile(cos, (1, reps)), jnp.tile(sin, (1, reps))


def _attention_layer(xs, modtab, norm_g, w_qkv, w_o, q_g, k_g, cos_t, sin_t, ctx_out):
    B, S, D = xs.shape
    nt = S // TS
    nq = N_HEADS * HEAD_DIM
    nkv = N_KV * HEAD_DIM
    nqk = nq + nkv
    d6 = modtab.shape[-1]
    q_scale = HEAD_DIM ** -0.5 * LOG2E
    gqk = jnp.concatenate([jnp.tile(q_g, N_HEADS) * q_scale, jnp.tile(k_g, N_KV)])[None]
    head_of = jnp.arange(nqk) // HEAD_DIM
    eh = (head_of[:, None] == jnp.arange(LANES)[None, :]).astype(F32)
    qT, k4, vT = pl.pallas_call(
        _qkv_kernel,
        grid=(B, nt),
        in_specs=[_row_spec(D, 0), _mod_spec(d6, 0), _full_spec((1, D)), _full_spec((D, nqk + nkv)),
                  _full_spec((1, nqk)),
                  pl.BlockSpec((TS, LANES), lambda b, s: (s, 0)),
                  pl.BlockSpec((TS, LANES), lambda b, s: (s, 0)),
                  _full_spec((nqk, LANES)), _full_spec((LANES, nqk))],
        out_specs=[pl.BlockSpec((None, nq, TS), lambda b, s: (b, 0, s)),
                   pl.BlockSpec((None, N_KV, TS, HEAD_DIM), lambda b, s: (b, 0, s, 0)),
                   pl.BlockSpec((None, nkv, TS), lambda b, s: (b, 0, s))],
        out_shape=[jax.ShapeDtypeStruct((B, nq, S), BF16),
                   jax.ShapeDtypeStruct((B, N_KV, S, HEAD_DIM), BF16),
                   jax.ShapeDtypeStruct((B, nkv, S), BF16)],
        compiler_params=_params("arbitrary", "arbitrary"),
        name="attn_qkv",
    )(xs, modtab, norm_g[None], w_qkv.astype(BF16), gqk, cos_t, sin_t, eh, eh.T)


    gw = Q_PER_KV * HEAD_DIM
    n_lat = S - TS
    w_o = w_o.astype(BF16)
    if ctx_out:
        oT_ctx = pl.pallas_call(
            _attn_ctx_kernel,
            grid=(B, N_KV),
            in_specs=[pl.BlockSpec((None, gw, TS), lambda b, kv: (b, kv, 0)),
                      pl.BlockSpec((None, None, TS, HEAD_DIM), lambda b, kv: (b, kv, 0, 0)),
                      pl.BlockSpec((None, HEAD_DIM, TS), lambda b, kv: (b, kv, 0))],
            out_specs=pl.BlockSpec((None, gw, TS), lambda b, kv: (b, kv, 0)),
            out_shape=jax.ShapeDtypeStruct((B, nq, TS), BF16),
            compiler_params=_params("arbitrary", "arbitrary"),
            name="attn_ctx",
        )(qT, k4, vT)
        xs = pl.pallas_call(
            _oproj_kernel,
            grid=(B, 1),
            in_specs=[pl.BlockSpec((None, nq, TS), lambda b, s: (b, 0, 0)), _full_spec((nq, D)),
                      _row_spec(D, 0), _mod_spec(d6, 0)],
            out_specs=_row_spec(D, 0),
            out_shape=jax.ShapeDtypeStruct((B, S, D), F32),
            input_output_aliases={2: 0},
            compiler_params=_params("arbitrary", "arbitrary"),
            name="attn_oproj_ctx",
        )(oT_ctx, w_o, xs, modtab)

    n_units = (n_lat // TS) * Q_PER_KV
    assert n_units % 2 == 0
    oT = pl.pallas_call(
        functools.partial(_attn_main_kernel, n_units=n_units, q_col0=TS),
        grid=(B, N_KV),
        in_specs=[pl.BlockSpec((None, gw, S), lambda b, kv: (b, kv, 0)),
                  pl.BlockSpec((None, None, S, HEAD_DIM), lambda b, kv: (b, kv, 0, 0)),
                  pl.BlockSpec((None, HEAD_DIM, S), lambda b, kv: (b, kv, 0))],
        out_specs=pl.BlockSpec((None, gw, n_lat), lambda b, kv: (b, kv, 0)),
        out_shape=jax.ShapeDtypeStruct((B, nq, n_lat), BF16),
        scratch_shapes=[pltpu.VMEM((S, TS), F32), pltpu.VMEM((S, TS), F32),
                        pltpu.VMEM((1, TS), F32), pltpu.VMEM((1, TS), F32)],
        compiler_params=_params("arbitrary", "arbitrary"),
        name="attn_core",
    )(qT, k4, vT)

    return pl.pallas_call(
        _oproj_kernel,
        grid=(B, n_lat // TS),
        in_specs=[pl.BlockSpec((None, nq, TS), lambda b, s: (b, 0, s)), _full_spec((nq, D)),
                  _row_spec(D, 1), _mod_spec(d6, 1)],
        out_specs=_row_spec(D, 1),
        out_shape=jax.ShapeDtypeStruct((B, S, D), F32),
        input_output_aliases={2: 0},
        compiler_params=_params("arbitrary", "arbitrary"),
        name="attn_oproj",
    )(oT, w_o, xs, modtab)


def _normmod_kernel(x_ref, mod_ref, g_ref, h_ref):
    d = x_ref.shape[-1]
    h_ref[...] = _normmod(x_ref[...], g_ref[...], mod_ref[:, 0:d], mod_ref[:, d:2 * d])


def _pool_kernel(h_ref, x_ref, w_ref, ps_ref, gate_ref, out_ref, pad_ref, *, segs):
    gi = pl.program_id(1)
    zeros = jnp.zeros((POOL_PAD, h_ref.shape[-1]), F32)
    for widx, win in enumerate(POOL_WINDOWS):
        half = win // 2

        @pl.when(gi == widx)
        def _(half=half):
            for si, (r0, n) in enumerate(segs):
                pad_ref[0:POOL_PAD, :] = zeros
                pad_ref[POOL_PAD:POOL_PAD + n, :] = h_ref[r0:r0 + n, :]
                pad_ref[POOL_PAD + n:2 * POOL_PAD + n, :] = zeros
                gate = gate_ref[si]
                for c0 in range(0, n, ROW_CHUNK):
                    base = POOL_PAD + c0
                    acc = pad_ref[base - half:base - half + ROW_CHUNK, :]
                    for j in range(-half + 1, half):
                        acc = acc + pad_ref[base + j:base + j + ROW_CHUNK, :]
                    t = c0 + lax.broadcasted_iota(I32, (ROW_CHUNK, 1), 0)
                    cnt = jnp.minimum(t + half, n) - jnp.maximum(t - half, 0)
                    diff = acc / cnt.astype(F32) - pad_ref[base:base + ROW_CHUNK, :]
                    y = jnp.dot(diff.astype(BF16), w_ref[...], preferred_element_type=F32) * ps_ref[...]
                    rows = slice(r0 + c0, r0 + c0 + ROW_CHUNK)
                    out_ref[rows, :] = x_ref[rows, :] + gate * y


def _pool_layer(xs, modtab, norm_g, w_pool, pool_scale, m):
    B, S, D = xs.shape
    nt = S // TS
    d6 = modtab.shape[-1]
    pg = D // len(POOL_WINDOWS)
    h = pl.pallas_call(
        _normmod_kernel,
        grid=(B, nt),
        in_specs=[_row_spec(D, 0), _mod_spec(d6, 0), _full_spec((1, D))],
        out_specs=_row_spec(D, 0),
        out_shape=jax.ShapeDtypeStruct((B, S, D), F32),
        compiler_params=_params("arbitrary", "arbitrary"),
        name="pool_normmod",
    )(xs, modtab, norm_g[None])
    segs = ((0, m), (m, S - m))
    seq_spec = pl.BlockSpec((None, S, pg), lambda b, g: (b, 0, g))
    return pl.pallas_call(
        functools.partial(_pool_kernel, segs=segs),
        grid=(B, len(POOL_WINDOWS)),
        in_specs=[seq_spec, seq_spec,
                  pl.BlockSpec((None, pg, pg), lambda b, g: (g, 0, 0)),
                  pl.BlockSpec((1, pg), lambda b, g: (0, g)),
                  pl.BlockSpec((None, 2, 1, pg), lambda b, g: (b, 0, 0, 2 * (D // pg) + g))],
        out_specs=seq_spec,
        out_shape=jax.ShapeDtypeStruct((B, S, D), F32),
        scratch_shapes=[pltpu.VMEM((S - m + 2 * POOL_PAD, pg), F32)],
        compiler_params=_params("arbitrary", "arbitrary"),
        name="pool_mix",
    )(h, xs, w_pool.astype(BF16), pool_scale[None], modtab)


def _ssm_in_kernel(x_ref, mod_ref, g_ref, wz_ref, wx_ref, wdt_ref, z_ref, xbc_ref, dt_ref):
    d = x_ref.shape[-1]
    hf = _normmod(x_ref[...], g_ref[...], mod_ref[:, 0:d], mod_ref[:, d:2 * d])
    h = hf.astype(BF16)
    z_ref[...] = jnp.dot(h, wz_ref[...], preferred_element_type=F32)
    xbc_ref[...] = jnp.dot(h, wx_ref[...], preferred_element_type=F32)
    dt_ref[...] = jnp.dot(hf, wdt_ref[...], precision=HI, preferred_element_type=F32)


def _conv_kernel(u_ref, w_ref, b_ref, o_ref, pad_ref, *, segs):
    zeros = jnp.zeros((CONV_PAD, u_ref.shape[-1]), F32)
    for r0, n in segs:
        pad_ref[0:CONV_PAD, :] = zeros
        pad_ref[CONV_PAD:CONV_PAD + n, :] = u_ref[r0:r0 + n, :]
        pad_ref[CONV_PAD + n:2 * CONV_PAD + n, :] = zeros
        for c0 in range(0, n, ROW_CHUNK):
            acc = b_ref[...]
            for k in range(SSM_CONV):
                lo = CONV_PAD + c0 - SSM_CONV // 2 + k
                acc = acc + w_ref[k:k + 1, :] * pad_ref[lo:lo + ROW_CHUNK, :]
            o_ref[r0 + c0:r0 + c0 + ROW_CHUNK, :] = _silu(acc)


def _expand_heads(v, lane0):
    r = v.shape[0]
    lane = lax.broadcasted_iota(I32, (r, LANES), 1)
    blocks = []
    for j in range(SSM_D_INNER // LANES):
        a = jnp.broadcast_to(v[:, lane0 + 2 * j:lane0 + 2 * j + 1], (r, LANES))
        b = jnp.broadcast_to(v[:, lane0 + 2 * j + 1:lane0 + 2 * j + 2], (r, LANES))
        blocks.append(jnp.where(lane < SSM_HEAD_DIM, a, b))
    return jnp.concatenate(blocks, axis=1)


def _ssd_kernel(xbc_ref, dt_ref, dtb_ref, a_ref, tri_ref, y_ref, h_ref, *, direction):
    q = SSM_CHUNK
    lane0 = SSM_HEADS * direction

    @pl.when(pl.program_id(1) == 0)
    def _():
        h_ref[...] = jnp.zeros(h_ref.shape, F32)

    raw = dt_ref[...] + dtb_ref[...]
    dt = jnp.maximum(raw, 0.0) + jnp.log1p(jnp.exp(-jnp.abs(raw)))
    la = dt * a_ref[...]
    cs = jnp.dot(tri_ref[...], la, precision=HI, preferred_element_type=F32)
    tot = cs[q - 1:q, :]
    if direction == 0:
        u = cs
        dec_in = jnp.exp(u)
        dec_end = jnp.exp(tot - u)
    else:
        u = la - cs
        dec_in = jnp.exp(tot + u)
        dec_end = jnp.exp(-u)
    uT = u.T
    xs = xbc_ref[:, 0:SSM_D_INNER] * _expand_heads(dt, lane0)
    xs_b = xs.astype(BF16)
    xd_b = (xs * _expand_heads(dec_end, lane0)).astype(BF16)
    din_col = _expand_heads(dec_in, lane0)
    dtot_col = _expand_heads(jnp.exp(tot), lane0)
    row = lax.broadcasted_iota(I32, (q, q), 0)
    col = lax.broadcasted_iota(I32, (q, q), 1)
    mask = (row >= col) if direction == 0 else (col >= row)
    gw = SSM_HPG * SSM_HEAD_DIM
    for g in range(SSM_GROUPS):
        b_lo = SSM_D_INNER + SSM_STATE * g
        c_lo = SSM_D_INNER + SSM_GN + SSM_STATE * g
        bT = xbc_ref[:, b_lo:b_lo + SSM_STATE].T.astype(BF16)
        cg = xbc_ref[:, c_lo:c_lo + SSM_STATE].astype(BF16)
        scores = jnp.dot(cg, bT, preferred_element_type=F32)
        hg = h_ref[g]
        cols = slice(gw * g, gw * (g + 1))
        y_off = jnp.dot(cg, hg.astype(BF16), preferred_element_type=F32) * din_col[:, cols]
        for r in range(SSM_HPG):
            hh = SSM_HPG * g + r
            hl = lane0 + hh
            seg = u[:, hl:hl + 1] - uT[hl:hl + 1, :]
            decay = jnp.exp(jnp.where(mask, seg, NEG_INF))
            hc = slice(SSM_HEAD_DIM * hh, SSM_HEAD_DIM * (hh + 1))
            y_d = jnp.dot((scores * decay).astype(BF16), xs_b[:, hc], preferred_element_type=F32)
            y_ref[:, hc] = y_d + y_off[:, SSM_HEAD_DIM * r:SSM_HEAD_DIM * (r + 1)]
        h_ref[g] = hg * dtot_col[:, cols] + jnp.dot(bT, xd_b[:, cols], preferred_element_type=F32)


def _ssm_out_kernel(y0_ref, y1_ref, xh_ref, z_ref, dsk_ref, ng_ref, w_ref, x_ref, mod_ref, out_ref):
    d = x_ref.shape[-1]
    y = y0_ref[...] + y1_ref[...] + dsk_ref[...] * xh_ref[...]
    g = y * _silu(z_ref[...])
    gsz = SSM_D_INNER // SSM_GROUPS
    parts = []
    for k in range(SSM_GROUPS):
        gk = g[:, gsz * k:gsz * (k + 1)]
        parts.append(gk * lax.rsqrt(jnp.mean(gk * gk, axis=-1, keepdims=True) + EPS))
    gn = (jnp.concatenate(parts, axis=1) * ng_ref[...]).astype(BF16)
    o = jnp.dot(gn, w_ref[...], preferred_element_type=F32)
    out_ref[...] = x_ref[...] + mod_ref[:, 2 * d:3 * d] * o


def _ssd_layer(xs, modtab, norm_g, w_in, conv_w, conv_b, dt_bias, a_log, d_skip, ssm_norm_g, w_out, m):
    B, S, D = xs.shape
    nt = S // TS
    d6 = modtab.shape[-1]
    di = SSM_D_INNER
    cd = SSM_CONV_DIM
    wz = w_in[:, :di].astype(BF16)
    wx = w_in[:, di:di + cd].astype(BF16)
    wdt = jnp.pad(w_in[:, di + cd:], ((0, 0), (0, LANES - 2 * SSM_HEADS)))
    z, xbc_raw, dt_raw = pl.pallas_call(
        _ssm_in_kernel,
        grid=(B, nt),
        in_specs=[_row_spec(D, 0), _mod_spec(d6, 0), _full_spec((1, D)), _full_spec((D, di)),
                  _full_spec((D, cd)), _full_spec((D, LANES))],
        out_specs=[_row_spec(di, 0), _row_spec(cd, 0), _row_spec(LANES, 0)],
        out_shape=[jax.ShapeDtypeStruct((B, S, di), F32), jax.ShapeDtypeStruct((B, S, cd), F32),
                   jax.ShapeDtypeStruct((B, S, LANES), F32)],
        compiler_params=_params("arbitrary", "arbitrary"),
        name="ssm_in",
    )(xs, modtab, norm_g[None], wz, wx, wdt)

    segs = ((0, m), (m, S - m))
    cw = 256
    seq_spec = pl.BlockSpec((None, S, cw), lambda b, j: (b, 0, j))
    xbc = pl.pallas_call(
        functools.partial(_conv_kernel, segs=segs),
        grid=(B, cd // cw),
        in_specs=[seq_spec, pl.BlockSpec((SSM_CONV, cw), lambda b, j: (0, j)),
                  pl.BlockSpec((1, cw), lambda b, j: (0, j))],
        out_specs=seq_spec,
        out_shape=jax.ShapeDtypeStruct((B, S, cd), F32),
        scratch_shapes=[pltpu.VMEM((S - m + 2 * CONV_PAD, cw), F32)],
        compiler_params=_params("arbitrary", "arbitrary"),
        name="ssm_conv",
    )(xbc_raw, conv_w, conv_b[None])

    q = SSM_CHUNK
    nc = S // q
    mc = m // q
    pad = LANES - 2 * SSM_HEADS
    dtb = jnp.pad(dt_bias.reshape(-1), (0, pad))[None]
    a_neg = jnp.pad(-jnp.exp(a_log.reshape(-1)), (0, pad))[None]
    tri = (jnp.arange(q)[:, None] >= jnp.arange(q)[None, :]).astype(F32)
    ys = []
    for direction in range(2):
        if direction == 0:
            cmap = lambda c: c
        else:
            cmap = lambda c: jnp.where(c < mc, mc - 1 - c, nc - 1 - (c - mc))
        ys.append(pl.pallas_call(
            functools.partial(_ssd_kernel, direction=direction),
            grid=(B, nc),
            in_specs=[pl.BlockSpec((None, q, cd), lambda b, c, cmap=cmap: (b, cmap(c), 0)),
                      pl.BlockSpec((None, q, LANES), lambda b, c, cmap=cmap: (b, cmap(c), 0)),
                      _full_spec((1, LANES)), _full_spec((1, LANES)), _full_spec((q, q))],
            out_specs=pl.BlockSpec((None, q, di), lambda b, c, cmap=cmap: (b, cmap(c), 0)),
            out_shape=jax.ShapeDtypeStruct((B, S, di), F32),
            scratch_shapes=[pltpu.VMEM((SSM_GROUPS, SSM_STATE, SSM_HPG * SSM_HEAD_DIM), F32)],
            compiler_params=_params("arbitrary", "arbitrary"),
            name=f"ssd_scan{direction}",
        )(xbc, dt_raw, dtb, a_neg, tri))

    dsk = jnp.repeat(d_skip[0] + d_skip[1], SSM_HEAD_DIM)[None]
    return pl.pallas_call(
        _ssm_out_kernel,
        grid=(B, nt),
        in_specs=[_row_spec(di, 0), _row_spec(di, 0), _row_spec(di, 0), _row_spec(di, 0),
                  _full_spec((1, di)), _full_spec((1, di)), _full_spec((di, D)),
                  _row_spec(D, 0), _mod_spec(d6, 0)],
        out_specs=_row_spec(D, 0),
        out_shape=jax.ShapeDtypeStruct((B, S, D), F32),
        input_output_aliases={7: 0},
        compiler_params=_params("arbitrary", "arbitrary"),
        name="ssm_out",
    )(ys[0], ys[1], xbc, z, dsk, ssm_norm_g[None], w_out.astype(BF16), xs, modtab)


def _router_kernel(x_ref, mod_ref, g_ref, wr_ref, ltri_ref, tok_ref, rt_ref, wt_ref, cnt_ref, run_ref):
    d = x_ref.shape[-1]

    @pl.when((pl.program_id(0) == 0) & (pl.program_id(1) == 0))
    def _():
        run_ref[...] = jnp.zeros(run_ref.shape, F32)

    t = _normmod(x_ref[...], g_ref[...], mod_ref[:, 3 * d:4 * d], mod_ref[:, 4 * d:5 * d])
    tok_ref[...] = t
    logits = jnp.dot(t, wr_ref[...], precision=HI, preferred_element_type=F32)
    lane = lax.broadcasted_iota(I32, logits.shape, 1).astype(F32)
    big = float(LANES)
    gl = jnp.where(lane < MOE_GROUPS, logits, NEG_INF)
    gmax = jnp.max(gl, axis=1, keepdims=True)
    gate = 1.0 / jnp.sum(jnp.exp(gl - gmax), axis=1, keepdims=True)
    grp = jnp.min(jnp.where(gl == gmax, lane, big), axis=1, keepdims=True)
    lo = MOE_GROUPS + MOE_EPG * grp
    el = jnp.where((lane >= lo) & (lane < lo + MOE_EPG), logits, NEG_INF)
    v1 = jnp.max(el, axis=1, keepdims=True)
    i1 = jnp.min(jnp.where(el == v1, lane, big), axis=1, keepdims=True)
    el2 = jnp.where(lane == i1, NEG_INF, el)
    v2 = jnp.max(el2, axis=1, keepdims=True)
    i2 = jnp.min(jnp.where(el2 == v2, lane, big), axis=1, keepdims=True)
    e2 = jnp.exp(v2 - v1)
    den = 1.0 + e2
    w1 = gate * (1.0 / den)
    w2 = gate * (e2 / den)
    wt_ref[...] = jnp.where(lane == 0.0, w1, jnp.where(lane == 1.0, w2, 0.0))
    oh1 = (lane == i1).astype(F32)
    oh2 = (lane == i2).astype(F32)
    oh = oh1 + oh2
    before = jnp.dot(ltri_ref[...], oh.astype(BF16), preferred_element_type=F32) + run_ref[...]
    r1 = jnp.sum(oh1 * before, axis=1, keepdims=True)
    r2 = jnp.sum(oh2 * before, axis=1, keepdims=True)
    run = run_ref[...] + jnp.sum(oh, axis=0, keepdims=True)
    run_ref[...] = run
    cnt_ref[...] = jnp.broadcast_to(run, cnt_ref.shape)
    rt_ref[...] = jnp.where(lane == 0.0, i1 - MOE_GROUPS,
                            jnp.where(lane == 1.0, i2 - MOE_GROUPS,
                                      jnp.where(lane == 2.0, r1, jnp.where(lane == 3.0, r2, 0.0)))).astype(I32)


def _row_copy(src_hbm, src_row, dst_ref, dst_row, sem):
    return pltpu.make_async_copy(src_hbm.at[pl.ds(src_row, 1)], dst_ref.at[pl.ds(dst_row, 1)], sem)


def _dispatch_kernel(dest_ref, tok_hbm, init_hbm, xs_hbm, sem):
    del init_hbm
    base = pl.program_id(0) * TS

    def body(j, carry):
        _row_copy(tok_hbm, base + j, xs_hbm, dest_ref[0, 2 * j], sem).start()
        _row_copy(tok_hbm, base + j, xs_hbm, dest_ref[0, 2 * j + 1], sem).start()
        return carry

    lax.fori_loop(0, TS, body, 0)
    pltpu.make_async_copy(tok_hbm.at[pl.ds(0, 2 * TS)], xs_hbm.at[pl.ds(0, 2 * TS)], sem).wait()


def _expert_kernel(be_ref, nu_ref, xs_ref, wgu_ref, wd_ref, y_ref):
    i = pl.program_id(0)

    @pl.when(i < nu_ref[0])
    def _():
        gu = jnp.dot(xs_ref[...].astype(BF16), wgu_ref[...], preferred_element_type=F32)
        hid = _silu(gu[:, :MOE_HIDDEN]) * gu[:, MOE_HIDDEN:]
        y_ref[...] = jnp.dot(hid.astype(BF16), wd_ref[...], preferred_element_type=F32)

    @pl.when(i >= nu_ref[0])
    def _():
        y_ref[...] = jnp.zeros(y_ref.shape, F32)


def _combine_kernel(dest_ref, x_ref, mod_ref, wt_ref, y_hbm, out_ref, buf_ref, sem):
    d = x_ref.shape[-1]

    def body(j, carry):
        _row_copy(y_hbm, dest_ref[0, 2 * j], buf_ref, j, sem).start()
        _row_copy(y_hbm, dest_ref[0, 2 * j + 1], buf_ref, TS + j, sem).start()
        return carry

    lax.fori_loop(0, TS, body, 0)
    pltpu.make_async_copy(y_hbm.at[pl.ds(0, 2 * TS)], buf_ref, sem).wait()
    f = wt_ref[:, 0:1] * buf_ref[0:TS, :] + wt_ref[:, 1:2] * buf_ref[TS:2 * TS, :]
    out_ref[...] = x_ref[...] + mod_ref[:, 5 * d:6 * d] * f


def _moe_layer(xs, modtab, norm_g, w_rg, w_re, w_gu, w_down, ctx_out):
    B, S, D = xs.shape
    d6 = modtab.shape[-1]
    t0 = 0 if ctx_out else 1
    nt = S // TS - t0
    R = nt * TS
    wr = jnp.pad(jnp.concatenate([w_rg, w_re], axis=1), ((0, 0), (0, LANES - MOE_GROUPS - MOE_EXPERTS)))
    out_row = lambda width: pl.BlockSpec((None, TS, width), lambda b, s: (b, s, 0))
    ltri = (jnp.arange(TS)[:, None] > jnp.arange(TS)[None, :]).astype(BF16)
    tok, rt, wt, cnt = pl.pallas_call(
        _router_kernel,
        grid=(B, nt),
        in_specs=[_row_spec(D, t0), _mod_spec(d6, t0), _full_spec((1, D)), _full_spec((D, LANES)),
                  _full_spec((TS, TS))],
        out_specs=[out_row(D), out_row(LANES), out_row(LANES), _full_spec((8, LANES))],
        out_shape=[jax.ShapeDtypeStruct((B, R, D), F32), jax.ShapeDtypeStruct((B, R, LANES), I32),
                   jax.ShapeDtypeStruct((B, R, LANES), F32), jax.ShapeDtypeStruct((8, LANES), F32)],
        scratch_shapes=[pltpu.VMEM((1, LANES), F32)],
        compiler_params=_params("arbitrary", "arbitrary"),
        name="moe_router",
    )(xs, modtab, norm_g[None], wr, ltri)

    T = B * R
    n_blocks = -(-2 * T // MOE_BLOCK) + MOE_EXPERTS
    n_rows = n_blocks * MOE_BLOCK
    counts = cnt[0, MOE_GROUPS:MOE_GROUPS + MOE_EXPERTS].astype(I32)
    padded = (counts + MOE_BLOCK - 1) // MOE_BLOCK * MOE_BLOCK
    pends = jnp.cumsum(padded)
    pstarts = pends - padded
    dest = (pstarts[rt[:, :, 0:2]] + rt[:, :, 2:4]).reshape(B * nt, 1, 2 * TS)
    blk_start = jnp.arange(n_blocks, dtype=I32) * MOE_BLOCK
    block_expert = jnp.minimum(jnp.sum(pends[None, :] <= blk_start[:, None], axis=1),
                               MOE_EXPERTS - 1).astype(I32)
    n_used = (pends[-1:] // MOE_BLOCK).astype(I32)

    hbm = pl.BlockSpec(memory_space=pltpu.MemorySpace.HBM)
    x_sorted = pl.pallas_call(
        _dispatch_kernel,
        grid=(B * nt,),
        in_specs=[pl.BlockSpec((None, 1, 2 * TS), lambda i: (i, 0, 0), memory_space=pltpu.SMEM), hbm, hbm],
        out_specs=hbm,
        out_shape=jax.ShapeDtypeStruct((n_rows, D), F32),
        scratch_shapes=[pltpu.SemaphoreType.DMA(())],
        input_output_aliases={2: 0},
        compiler_params=_params("arbitrary"),
        name="moe_dispatch",
    )(dest, tok.reshape(T, D), jnp.zeros((n_rows, D), F32))

    y_sorted = pl.pallas_call(
        _expert_kernel,
        grid_spec=pltpu.PrefetchScalarGridSpec(
            num_scalar_prefetch=2,
            grid=(n_blocks,),
            in_specs=[pl.BlockSpec((MOE_BLOCK, D), lambda i, be, nu: (i, 0)),
                      pl.BlockSpec((None, D, 2 * MOE_HIDDEN), lambda i, be, nu: (be[i], 0, 0)),
                      pl.BlockSpec((None, MOE_HIDDEN, D), lambda i, be, nu: (be[i], 0, 0))],
            out_specs=pl.BlockSpec((MOE_BLOCK, D), lambda i, be, nu: (i, 0)),
        ),
        out_shape=jax.ShapeDtypeStruct((n_rows, D), F32),
        compiler_params=_params("arbitrary"),
        name="moe_experts",
    )(block_expert, n_used, x_sorted, w_gu.astype(BF16), w_down.astype(BF16))

    return pl.pallas_call(
        _combine_kernel,
        grid=(B, nt),
        in_specs=[pl.BlockSpec((None, 1, 2 * TS), lambda b, s: (b * nt + s, 0, 0), memory_space=pltpu.SMEM),
                  _row_spec(D, t0), _mod_spec(d6, t0), out_row(LANES), hbm],
        out_specs=_row_spec(D, t0),
        out_shape=jax.ShapeDtypeStruct((B, S, D), F32),
        scratch_shapes=[pltpu.VMEM((2 * TS, D), F32), pltpu.SemaphoreType.DMA(())],
        input_output_aliases={1: 0},
        compiler_params=_params("arbitrary", "arbitrary"),
        name="moe_combine",
    )(dest, xs, modtab, wt, y_sorted)


def _final_kernel(x_ref, g_ref, o_ref):
    x = x_ref[...]
    ms = jnp.mean(x * x, axis=-1, keepdims=True)
    o_ref[...] = x * lax.rsqrt(ms + EPS) * g_ref[...]


def kernel(x, c, ctx, c_ctx, w_ada, b_ada, norm_mix_g, norm_ffn_g, final_norm_g, attn_w_qkv, attn_w_o, attn_q_norm_g, attn_k_norm_g, pool_w, pool_scale, ssm_w_in, ssm_conv_w, ssm_conv_b, ssm_dt_bias, ssm_a_log, ssm_d, ssm_norm_g, ssm_w_out, moe_w_router_group, moe_w_router_expert, moe_w_gate_up, moe_w_down):
    B, n, D = x.shape
    m = ctx.shape[1]
    depth = w_ada.shape[0]
    assert m == TS and n % TS == 0 and n % GRID_W == 0
    xs = jnp.concatenate([ctx, x], axis=1)
    mods = _ada(c, c_ctx, w_ada, b_ada)
    cos_t, sin_t = _rope_tables(n, m)
    for i in range(depth):
        kind, j = i % N_MIXERS, i // N_MIXERS
        ctx_out = i < depth - 1
        modtab = jnp.stack([jnp.broadcast_to(mods[i, B], (B, 6 * D)), mods[i, :B]], axis=1)[:, :, None, :]
        if kind == 0:
            xs = _attention_layer(xs, modtab, norm_mix_g[i], attn_w_qkv[j], attn_w_o[j], attn_q_norm_g[j],
                                  attn_k_norm_g[j], cos_t, sin_t, ctx_out)
        elif kind == 1:
            assert ctx_out
            xs = _pool_layer(xs, modtab, norm_mix_g[i], pool_w[j], pool_scale[j], m)
        else:
            assert ctx_out
            xs = _ssd_layer(xs, modtab, norm_mix_g[i], ssm_w_in[j], ssm_conv_w[j], ssm_conv_b[j],
                            ssm_dt_bias[j], ssm_a_log[j], ssm_d[j], ssm_norm_g[j], ssm_w_out[j], m)
        xs = _moe_layer(xs, modtab, norm_ffn_g[i], moe_w_router_group[i], moe_w_router_expert[i],
                        moe_w_gate_up[i], moe_w_down[i], ctx_out)
    return pl.pallas_call(
        _final_kernel,
        grid=(B, n // TS),
        in_specs=[_row_spec(D, m // TS), _full_spec((1, D))],
        out_specs=pl.BlockSpec((None, TS, D), lambda b, s: (b, s, 0)),
        out_shape=jax.ShapeDtypeStruct((B, n, D), F32),
        compiler_params=_params("arbitrary", "arbitrary"),
        name="final_norm",
    )(xs, final_norm_g[None])
```

```python
import functools

import jax
import jax.numpy as jnp
from jax import lax
from jax.experimental import pallas as pl
from jax.experimental.pallas import tpu as pltpu

F32 = jnp.float32
BF16 = jnp.bfloat16
I32 = jnp.int32
HI = lax.Precision.HIGHEST
EPS = 1e-6
NEG_INF = float("-inf")
LOG2E = 1.4426950408889634

TS = 256
LANES = 128
GRID_W = 64
ROPE_THETA = 10000.0
N_MIXERS = 3

N_HEADS = 16
N_KV = 4
HEAD_DIM = 64
Q_PER_KV = N_HEADS // N_KV

POOL_WINDOWS = (2, 4, 8, 16)
POOL_PAD = 16
ROW_CHUNK = 256

SSM_HEADS = 32
SSM_HEAD_DIM = 64
SSM_GROUPS = 4
SSM_HPG = SSM_HEADS // SSM_GROUPS
SSM_STATE = 128
SSM_CONV = 4
SSM_CHUNK = 128
SSM_D_INNER = SSM_HEADS * SSM_HEAD_DIM
SSM_GN = SSM_GROUPS * SSM_STATE
SSM_CONV_DIM = SSM_D_INNER + 2 * SSM_GN
CONV_PAD = 8

MOE_GROUPS = 4
MOE_EPG = 8
MOE_EXPERTS = MOE_GROUPS * MOE_EPG
MOE_HIDDEN = 512
MOE_BLOCK = 256
DMA_UNROLL = 8


def _params(*sem):
    return pltpu.CompilerParams(dimension_semantics=sem)


def _silu(v):
    return v / (1.0 + jnp.exp(-v))


def _split(v):
    hi = v.astype(BF16)
    return hi, (v - hi.astype(F32)).astype(BF16)


def _dot_split_lhs(a, e):
    hi, lo = _split(a)
    return jnp.dot(hi, e, preferred_element_type=F32) + jnp.dot(lo, e, preferred_element_type=F32)


def _dot_split(a, b_hi, b_lo):
    hi, lo = _split(a)
    return jnp.dot(hi, b_hi, preferred_element_type=F32) + (
        jnp.dot(hi, b_lo, preferred_element_type=F32) + jnp.dot(lo, b_hi, preferred_element_type=F32))


def _normmod(x, g, shift, scale):
    ms = jnp.mean(x * x, axis=-1, keepdims=True)
    return (x * lax.rsqrt(ms + EPS) * g) * (1.0 + scale) + shift


def _ada_kernel(a_ref, w_ref, b_ref, o_ref):
    a = _silu(a_ref[...])
    o_ref[0] = jnp.dot(a, w_ref[0], precision=HI, preferred_element_type=F32) + b_ref[0]


def _ada(c, c_ctx, w_ada, b_ada):
    depth, d, d6 = w_ada.shape
    b = c.shape[0]
    assert b + 1 <= 8
    a = jnp.concatenate([c, c_ctx[None], jnp.zeros((8 - b - 1, d), F32)], axis=0)
    tn = 1536
    return pl.pallas_call(
        _ada_kernel,
        grid=(depth, d6 // tn),
        in_specs=[pl.BlockSpec((8, d), lambda i, j: (0, 0)),
                  pl.BlockSpec((1, d, tn), lambda i, j: (i, 0, j)),
                  pl.BlockSpec((1, 1, tn), lambda i, j: (i, 0, j))],
        out_specs=pl.BlockSpec((1, 8, tn), lambda i, j: (i, 0, j)),
        out_shape=jax.ShapeDtypeStruct((depth, 8, d6), F32),
        compiler_params=_params("arbitrary", "arbitrary"),
        name="ada",
    )(a, w_ada, b_ada.reshape(depth, 1, d6))


def _row_spec(width, t0, col=0):
    return pl.BlockSpec((None, TS, width), lambda b, s: (b, s + t0, col))


def _mod_spec(d6, t0):
    return pl.BlockSpec((None, None, 1, d6), lambda b, s: (b, jnp.minimum(s + t0, 1), 0, 0))


def _full_spec(shape):
    nd = len(shape)
    return pl.BlockSpec(shape, lambda b, s: (0,) * nd)


def _qkv_kernel(x_ref, mod_ref, g_ref, w_ref, gqk_ref, cos_ref, sin_ref, eh_ref, eht_ref,
                qT_ref, k_ref, vT_ref):
    d = x_ref.shape[-1]
    nq = N_HEADS * HEAD_DIM
    nqk = nq + N_KV * HEAD_DIM
    h = _normmod(x_ref[...], g_ref[...], mod_ref[:, 0:d], mod_ref[:, d:2 * d]).astype(BF16)
    qkv = jnp.dot(h, w_ref[...], preferred_element_type=F32)
    qk = qkv[:, :nqk]
    ss = _dot_split_lhs(qk * qk, eh_ref[...])
    rinv = lax.rsqrt(ss * (1.0 / HEAD_DIM) + EPS)
    qk = qk * _dot_split_lhs(rinv, eht_ref[...]) * gqk_ref[...]
    cos = cos_ref[...]
    sin = sin_ref[...]
    lane = lax.broadcasted_iota(I32, (TS, LANES), 1)
    even = (lane & 1) == 0
    blocks = []
    for j in range(nqk // LANES):
        blk = qk[:, LANES * j:LANES * (j + 1)]
        partner = jnp.where(even, pltpu.roll(blk, LANES - 1, 1), pltpu.roll(blk, 1, 1))
        blocks.append(blk * cos + partner * sin)
    q = jnp.concatenate(blocks[:nq // LANES], axis=1)
    k = jnp.concatenate(blocks[nq // LANES:], axis=1)
    qT_ref[...] = q.T.astype(BF16)
    for kv in range(N_KV):
        k_ref[kv] = k[:, HEAD_DIM * kv:HEAD_DIM * (kv + 1)].astype(BF16)
    vT_ref[...] = qkv[:, nqk:].T.astype(BF16)


def _attn_ctx_kernel(qT_ref, k_ref, vT_ref, o_ref):
    for g in range(Q_PER_KV):
        rows = slice(HEAD_DIM * g, HEAD_DIM * (g + 1))
        s = jnp.dot(k_ref[...], qT_ref[rows, :], preferred_element_type=F32)
        p = jnp.exp2(s - jnp.max(s, axis=0, keepdims=True))
        l = jnp.sum(p, axis=0, keepdims=True)
        o = jnp.dot(vT_ref[...], p.astype(BF16), preferred_element_type=F32)
        o_ref[rows, :] = (o / l).astype(BF16)


def _attn_main_kernel(qT_ref, k_ref, vT_ref, o_ref, s0_ref, s1_ref, m0_ref, m1_ref, *, n_units, q_col0):
    for ref in (s0_ref, s1_ref, m0_ref, m1_ref):
        ref[...] = jnp.zeros(ref.shape, F32)

    def unit(i):
        i = jnp.clip(i, 0, n_units - 1)
        row = pl.multiple_of((i % Q_PER_KV) * HEAD_DIM, HEAD_DIM)
        col = pl.multiple_of((i // Q_PER_KV) * TS, TS)
        return row, col

    def step(i, s_w, m_w, s_r, m_r):
        row, col = unit(i)
        q = qT_ref[pl.ds(row, HEAD_DIM), pl.ds(q_col0 + col, TS)]
        s = jnp.dot(k_ref[...], q, preferred_element_type=F32)
        s_w[...] = s
        m_w[...] = jnp.max(s, axis=0, keepdims=True)
        row, col = unit(i - 1)
        p = jnp.exp2(s_r[...] - m_r[...])
        l = jnp.sum(p, axis=0, keepdims=True)
        o = jnp.dot(vT_ref[...], p.astype(BF16), preferred_element_type=F32)
        o_ref[pl.ds(row, HEAD_DIM), pl.ds(col, TS)] = (o / l).astype(BF16)

    def body(j, carry):
        step(2 * j, s0_ref, m0_ref, s1_ref, m1_ref)
        step(2 * j + 1, s1_ref, m1_ref, s0_ref, m0_ref)
        return carry

    lax.fori_loop(0, n_units // 2 + 1, body, 0)


def _oproj_kernel(oT_ref, w_ref, x_ref, mod_ref, out_ref):
    d = x_ref.shape[-1]
    o = oT_ref[...].astype(F32).T.astype(BF16)
    y = jnp.dot(o, w_ref[...], preferred_element_type=F32)
    out_ref[...] = x_ref[...] + mod_ref[:, 2 * d:3 * d] * y


def _rope_tables(n, m):
    rows = n // GRID_W
    row = jnp.broadcast_to(jnp.arange(rows)[:, None], (rows, GRID_W)).reshape(-1).astype(F32)
    col = jnp.broadcast_to(jnp.arange(GRID_W)[None, :], (rows, GRID_W)).reshape(-1).astype(F32)
    n_freq = HEAD_DIM // 4
    inv_freq = ROPE_THETA ** (-jnp.arange(n_freq, dtype=F32) / n_freq)
    ang = jnp.concatenate([row[:, None] * inv_freq, col[:, None] * inv_freq], axis=-1)
    cos = jnp.repeat(jnp.cos(ang), 2, axis=1)
    sign = jnp.tile(jnp.array([-1.0, 1.0], F32), HEAD_DIM // 2)
    sin = jnp.repeat(jnp.sin(ang), 2, axis=1) * sign
    cos = jnp.concatenate([jnp.ones((m, HEAD_DIM), F32), cos], axis=0)
    sin = jnp.concatenate([jnp.zeros((m, HEAD_DIM), F32), sin], axis=0)
    reps = LANES // HEAD_DIM
    return jnp.tile(cos, (1, reps)), jnp.tile(sin, (1, reps))


def _attention_layer(xs, modtab, norm_g, w_qkv, w_o, q_g, k_g, cos_t, sin_t, ctx_out):
    B, S, D = xs.shape
    nt = S // TS
    nq = N_HEADS * HEAD_DIM
    nkv = N_KV * HEAD_DIM
    nqk = nq + nkv
    d6 = modtab.shape[-1]
    q_scale = HEAD_DIM ** -0.5 * LOG2E
    gqk = jnp.concatenate([jnp.tile(q_g, N_HEADS) * q_scale, jnp.tile(k_g, N_KV)])[None]
    head_of = jnp.arange(nqk) // HEAD_DIM
    eh = (head_of[:, None] == jnp.arange(LANES)[None, :]).astype(BF16)
    qT, k4, vT = pl.pallas_call(
        _qkv_kernel,
        grid=(B, nt),
        in_specs=[_row_spec(D, 0), _mod_spec(d6, 0), _full_spec((1, D)), _full_spec((D, nqk + nkv)),
                  _full_spec((1, nqk)),
                  pl.BlockSpec((TS, LANES), lambda b, s: (s, 0)),
                  pl.BlockSpec((TS, LANES), lambda b, s: (s, 0)),
                  _full_spec((nqk, LANES)), _full_spec((LANES, nqk))],
        out_specs=[pl.BlockSpec((None, nq, TS), lambda b, s: (b, 0, s)),
                   pl.BlockSpec((None, N_KV, TS, HEAD_DIM), lambda b, s: (b, 0, s, 0)),
                   pl.BlockSpec((None, nkv, TS), lambda b, s: (b, 0, s))],
        out_shape=[jax.ShapeDtypeStruct((B, nq, S), BF16),
                   jax.ShapeDtypeStruct((B, N_KV, S, HEAD_DIM), BF16),
                   jax.ShapeDtypeStruct((B, nkv, S), BF16)],
        compiler_params=_params("arbitrary", "arbitrary"),
        name="attn_qkv",
    )(xs, modtab, norm_g[None], w_qkv.astype(BF16), gqk, cos_t, sin_t, eh, eh.T)


    gw = Q_PER_KV * HEAD_DIM
    n_lat = S - TS
    w_o = w_o.astype(BF16)
    if ctx_out:
        oT_ctx = pl.pallas_call(
            _attn_ctx_kernel,
            grid=(B, N_KV),
            in_specs=[pl.BlockSpec((None, gw, TS), lambda b, kv: (b, kv, 0)),
                      pl.BlockSpec((None, None, TS, HEAD_DIM), lambda b, kv: (b, kv, 0, 0)),
                      pl.BlockSpec((None, HEAD_DIM, TS), lambda b, kv: (b, kv, 0))],
            out_specs=pl.BlockSpec((None, gw, TS), lambda b, kv: (b, kv, 0)),
            out_shape=jax.ShapeDtypeStruct((B, nq, TS), BF16),
            compiler_params=_params("arbitrary", "arbitrary"),
            name="attn_ctx",
        )(qT, k4, vT)
        xs = pl.pallas_call(
            _oproj_kernel,
            grid=(B, 1),
            in_specs=[pl.BlockSpec((None, nq, TS), lambda b, s: (b, 0, 0)), _full_spec((nq, D)),
                      _row_spec(D, 0), _mod_spec(d6, 0)],
            out_specs=_row_spec(D, 0),
            out_shape=jax.ShapeDtypeStruct((B, S, D), F32),
            input_output_aliases={2: 0},
            compiler_params=_params("arbitrary", "arbitrary"),
            name="attn_oproj_ctx",
        )(oT_ctx, w_o, xs, modtab)

    n_units = (n_lat // TS) * Q_PER_KV
    assert n_units % 2 == 0
    oT = pl.pallas_call(
        functools.partial(_attn_main_kernel, n_units=n_units, q_col0=TS),
        grid=(B, N_KV),
        in_specs=[pl.BlockSpec((None, gw, S), lambda b, kv: (b, kv, 0)),
                  pl.BlockSpec((None, None, S, HEAD_DIM), lambda b, kv: (b, kv, 0, 0)),
                  pl.BlockSpec((None, HEAD_DIM, S), lambda b, kv: (b, kv, 0))],
        out_specs=pl.BlockSpec((None, gw, n_lat), lambda b, kv: (b, kv, 0)),
        out_shape=jax.ShapeDtypeStruct((B, nq, n_lat), BF16),
        scratch_shapes=[pltpu.VMEM((S, TS), F32), pltpu.VMEM((S, TS), F32),
                        pltpu.VMEM((1, TS), F32), pltpu.VMEM((1, TS), F32)],
        compiler_params=_params("arbitrary", "arbitrary"),
        name="attn_core",
    )(qT, k4, vT)

    return pl.pallas_call(
        _oproj_kernel,
        grid=(B, n_lat // TS),
        in_specs=[pl.BlockSpec((None, nq, TS), lambda b, s: (b, 0, s)), _full_spec((nq, D)),
                  _row_spec(D, 1), _mod_spec(d6, 1)],
        out_specs=_row_spec(D, 1),
        out_shape=jax.ShapeDtypeStruct((B, S, D), F32),
        input_output_aliases={2: 0},
        compiler_params=_params("arbitrary", "arbitrary"),
        name="attn_oproj",
    )(oT, w_o, xs, modtab)


def _normmod_kernel(x_ref, mod_ref, g_ref, h_ref):
    d = x_ref.shape[-1]
    h_ref[...] = _normmod(x_ref[...], g_ref[...], mod_ref[:, 0:d], mod_ref[:, d:2 * d])


def _pool_kernel(h_ref, x_ref, w_ref, ps_ref, gate_ref, out_ref, pad_ref, *, segs):
    gi = pl.program_id(1)
    zeros = jnp.zeros((POOL_PAD, h_ref.shape[-1]), F32)
    for widx, win in enumerate(POOL_WINDOWS):
        half = win // 2

        @pl.when(gi == widx)
        def _(half=half):
            for si, (r0, n) in enumerate(segs):
                pad_ref[0:POOL_PAD, :] = zeros
                pad_ref[POOL_PAD:POOL_PAD + n, :] = h_ref[r0:r0 + n, :]
                pad_ref[POOL_PAD + n:2 * POOL_PAD + n, :] = zeros
                gate = gate_ref[si]
                for c0 in range(0, n, ROW_CHUNK):
                    base = POOL_PAD + c0
                    acc = pad_ref[base - half:base - half + ROW_CHUNK, :]
                    for j in range(-half + 1, half):
                        acc = acc + pad_ref[base + j:base + j + ROW_CHUNK, :]
                    t = c0 + lax.broadcasted_iota(I32, (ROW_CHUNK, 1), 0)
                    cnt = jnp.minimum(t + half, n) - jnp.maximum(t - half, 0)
                    diff = acc / cnt.astype(F32) - pad_ref[base:base + ROW_CHUNK, :]
                    y = jnp.dot(diff.astype(BF16), w_ref[...], preferred_element_type=F32) * ps_ref[...]
                    rows = slice(r0 + c0, r0 + c0 + ROW_CHUNK)
                    out_ref[rows, :] = x_ref[rows, :] + gate * y


def _pool_layer(xs, modtab, norm_g, w_pool, pool_scale, m):
    B, S, D = xs.shape
    nt = S // TS
    d6 = modtab.shape[-1]
    pg = D // len(POOL_WINDOWS)
    h = pl.pallas_call(
        _normmod_kernel,
        grid=(B, nt),
        in_specs=[_row_spec(D, 0), _mod_spec(d6, 0), _full_spec((1, D))],
        out_specs=_row_spec(D, 0),
        out_shape=jax.ShapeDtypeStruct((B, S, D), F32),
        compiler_params=_params("arbitrary", "arbitrary"),
        name="pool_normmod",
    )(xs, modtab, norm_g[None])
    segs = ((0, m), (m, S - m))
    seq_spec = pl.BlockSpec((None, S, pg), lambda b, g: (b, 0, g))
    return pl.pallas_call(
        functools.partial(_pool_kernel, segs=segs),
        grid=(B, len(POOL_WINDOWS)),
        in_specs=[seq_spec, seq_spec,
                  pl.BlockSpec((None, pg, pg), lambda b, g: (g, 0, 0)),
                  pl.BlockSpec((1, pg), lambda b, g: (0, g)),
                  pl.BlockSpec((None, 2, 1, pg), lambda b, g: (b, 0, 0, 2 * (D // pg) + g))],
        out_specs=seq_spec,
        out_shape=jax.ShapeDtypeStruct((B, S, D), F32),
        scratch_shapes=[pltpu.VMEM((S - m + 2 * POOL_PAD, pg), F32)],
        compiler_params=_params("arbitrary", "arbitrary"),
        name="pool_mix",
    )(h, xs, w_pool.astype(BF16), pool_scale[None], modtab)


def _ssm_in_kernel(x_ref, mod_ref, g_ref, wz_ref, wx_ref, wdt_hi_ref, wdt_lo_ref, z_ref, xbc_ref, dt_ref):
    d = x_ref.shape[-1]
    hf = _normmod(x_ref[...], g_ref[...], mod_ref[:, 0:d], mod_ref[:, d:2 * d])
    h = hf.astype(BF16)
    z_ref[...] = jnp.dot(h, wz_ref[...], preferred_element_type=F32)
    xbc_ref[...] = jnp.dot(h, wx_ref[...], preferred_element_type=F32)
    dt_ref[...] = _dot_split(hf, wdt_hi_ref[...], wdt_lo_ref[...])


def _conv_kernel(u_ref, w_ref, b_ref, o_ref, pad_ref, *, segs):
    zeros = jnp.zeros((CONV_PAD, u_ref.shape[-1]), F32)
    for r0, n in segs:
        pad_ref[0:CONV_PAD, :] = zeros
        pad_ref[CONV_PAD:CONV_PAD + n, :] = u_ref[r0:r0 + n, :]
        pad_ref[CONV_PAD + n:2 * CONV_PAD + n, :] = zeros
        for c0 in range(0, n, ROW_CHUNK):
            acc = b_ref[...]
            for k in range(SSM_CONV):
                lo = CONV_PAD + c0 - SSM_CONV // 2 + k
                acc = acc + w_ref[k:k + 1, :] * pad_ref[lo:lo + ROW_CHUNK, :]
            o_ref[r0 + c0:r0 + c0 + ROW_CHUNK, :] = _silu(acc)


def _ssd_kernel(xbc_ref, dt_ref, dtb_ref, a_ref, tri_ref, sel_ref, y_ref, h_ref, *, direction):
    q = SSM_CHUNK
    lane0 = SSM_HEADS * direction

    @pl.when(pl.program_id(1) == 0)
    def _():
        h_ref[...] = jnp.zeros(h_ref.shape, F32)

    raw = dt_ref[...] + dtb_ref[...]
    dt = jnp.maximum(raw, 0.0) + jnp.log1p(jnp.exp(-jnp.abs(raw)))
    la = dt * a_ref[...]
    cs = jnp.dot(tri_ref[...], la, precision=HI, preferred_element_type=F32)
    tot = cs[q - 1:q, :]
    if direction == 0:
        u = cs
        dec_in = jnp.exp(u)
        dec_end = jnp.exp(tot - u)
    else:
        u = la - cs
        dec_in = jnp.exp(tot + u)
        dec_end = jnp.exp(-u)
    uT = u.T
    dtT = dt.T
    stacked = jnp.concatenate([dec_in, dt * dec_end, jnp.broadcast_to(jnp.exp(tot), (8, LANES))], axis=0)
    spread = _dot_split_lhs(stacked, sel_ref[...])
    din_col = spread[0:q]
    dtot_col = spread[2 * q:2 * q + 1]
    x = xbc_ref[:, 0:SSM_D_INNER]
    x_b = x.astype(BF16)
    xd_b = (x * spread[q:2 * q]).astype(BF16)
    row = lax.broadcasted_iota(I32, (q, q), 0)
    col = lax.broadcasted_iota(I32, (q, q), 1)
    mask = (row >= col) if direction == 0 else (col >= row)
    gw = SSM_HPG * SSM_HEAD_DIM
    for g in range(SSM_GROUPS):
        b_lo = SSM_D_INNER + SSM_STATE * g
        c_lo = SSM_D_INNER + SSM_GN + SSM_STATE * g
        bT = xbc_ref[:, b_lo:b_lo + SSM_STATE].T.astype(BF16)
        cg = xbc_ref[:, c_lo:c_lo + SSM_STATE].astype(BF16)
        scores = jnp.dot(cg, bT, preferred_element_type=F32)
        hg = h_ref[g]
        cols = slice(gw * g, gw * (g + 1))
        y_off = jnp.dot(cg, hg.astype(BF16), preferred_element_type=F32) * din_col[:, cols]
        for r in range(SSM_HPG):
            hh = SSM_HPG * g + r
            hl = lane0 + hh
            seg = u[:, hl:hl + 1] - uT[hl:hl + 1, :]
            decay = jnp.exp(jnp.where(mask, seg, NEG_INF))
            hc = slice(SSM_HEAD_DIM * hh, SSM_HEAD_DIM * (hh + 1))
            mix = (scores * decay * dtT[hl:hl + 1, :]).astype(BF16)
            y_d = jnp.dot(mix, x_b[:, hc], preferred_element_type=F32)
            y_ref[:, hc] = y_d + y_off[:, SSM_HEAD_DIM * r:SSM_HEAD_DIM * (r + 1)]
        h_ref[g] = hg * dtot_col[:, cols] + jnp.dot(bT, xd_b[:, cols], preferred_element_type=F32)


def _ssm_out_kernel(y0_ref, y1_ref, xh_ref, z_ref, dsk_ref, ng_ref, w_ref, x_ref, mod_ref, out_ref):
    d = x_ref.shape[-1]
    y = y0_ref[...] + y1_ref[...] + dsk_ref[...] * xh_ref[...]
    g = y * _silu(z_ref[...])
    gsz = SSM_D_INNER // SSM_GROUPS
    parts = []
    for k in range(SSM_GROUPS):
        gk = g[:, gsz * k:gsz * (k + 1)]
        parts.append(gk * lax.rsqrt(jnp.mean(gk * gk, axis=-1, keepdims=True) + EPS))
    gn = (jnp.concatenate(parts, axis=1) * ng_ref[...]).astype(BF16)
    o = jnp.dot(gn, w_ref[...], preferred_element_type=F32)
    out_ref[...] = x_ref[...] + mod_ref[:, 2 * d:3 * d] * o


def _ssd_layer(xs, modtab, norm_g, w_in, conv_w, conv_b, dt_bias, a_log, d_skip, ssm_norm_g, w_out, m):
    B, S, D = xs.shape
    nt = S // TS
    d6 = modtab.shape[-1]
    di = SSM_D_INNER
    cd = SSM_CONV_DIM
    wz = w_in[:, :di].astype(BF16)
    wx = w_in[:, di:di + cd].astype(BF16)
    wdt_hi, wdt_lo = _split(jnp.pad(w_in[:, di + cd:], ((0, 0), (0, LANES - 2 * SSM_HEADS))))
    z, xbc_raw, dt_raw = pl.pallas_call(
        _ssm_in_kernel,
        grid=(B, nt),
        in_specs=[_row_spec(D, 0), _mod_spec(d6, 0), _full_spec((1, D)), _full_spec((D, di)),
                  _full_spec((D, cd)), _full_spec((D, LANES)), _full_spec((D, LANES))],
        out_specs=[_row_spec(di, 0), _row_spec(cd, 0), _row_spec(LANES, 0)],
        out_shape=[jax.ShapeDtypeStruct((B, S, di), F32), jax.ShapeDtypeStruct((B, S, cd), F32),
                   jax.ShapeDtypeStruct((B, S, LANES), F32)],
        compiler_params=_params("arbitrary", "arbitrary"),
        name="ssm_in",
    )(xs, modtab, norm_g[None], wz, wx, wdt_hi, wdt_lo)

    segs = ((0, m), (m, S - m))
    cw = 256
    seq_spec = pl.BlockSpec((None, S, cw), lambda b, j: (b, 0, j))
    xbc = pl.pallas_call(
        functools.partial(_conv_kernel, segs=segs),
        grid=(B, cd // cw),
        in_specs=[seq_spec, pl.BlockSpec((SSM_CONV, cw), lambda b, j: (0, j)),
                  pl.BlockSpec((1, cw), lambda b, j: (0, j))],
        out_specs=seq_spec,
        out_shape=jax.ShapeDtypeStruct((B, S, cd), F32),
        scratch_shapes=[pltpu.VMEM((S - m + 2 * CONV_PAD, cw), F32)],
        compiler_params=_params("arbitrary", "arbitrary"),
        name="ssm_conv",
    )(xbc_raw, conv_w, conv_b[None])

    q = SSM_CHUNK
    nc = S // q
    mc = m // q
    pad = LANES - 2 * SSM_HEADS
    dtb = jnp.pad(dt_bias.reshape(-1), (0, pad))[None]
    a_neg = jnp.pad(-jnp.exp(a_log.reshape(-1)), (0, pad))[None]
    tri = (jnp.arange(q)[:, None] >= jnp.arange(q)[None, :]).astype(F32)
    ys = []
    for direction in range(2):
        if direction == 0:
            cmap = lambda c: c
        else:
            cmap = lambda c: jnp.where(c < mc, mc - 1 - c, nc - 1 - (c - mc))
        head_lane = SSM_HEADS * direction + jnp.arange(di) // SSM_HEAD_DIM
        sel = (jnp.arange(LANES)[:, None] == head_lane[None, :]).astype(BF16)
        ys.append(pl.pallas_call(
            functools.partial(_ssd_kernel, direction=direction),
            grid=(B, nc),
            in_specs=[pl.BlockSpec((None, q, cd), lambda b, c, cmap=cmap: (b, cmap(c), 0)),
                      pl.BlockSpec((None, q, LANES), lambda b, c, cmap=cmap: (b, cmap(c), 0)),
                      _full_spec((1, LANES)), _full_spec((1, LANES)), _full_spec((q, q)),
                      _full_spec((LANES, di))],
            out_specs=pl.BlockSpec((None, q, di), lambda b, c, cmap=cmap: (b, cmap(c), 0)),
            out_shape=jax.ShapeDtypeStruct((B, S, di), F32),
            scratch_shapes=[pltpu.VMEM((SSM_GROUPS, SSM_STATE, SSM_HPG * SSM_HEAD_DIM), F32)],
            compiler_params=_params("arbitrary", "arbitrary"),
            name=f"ssd_scan{direction}",
        )(xbc, dt_raw, dtb, a_neg, tri, sel))

    dsk = jnp.repeat(d_skip[0] + d_skip[1], SSM_HEAD_DIM)[None]
    return pl.pallas_call(
        _ssm_out_kernel,
        grid=(B, nt),
        in_specs=[_row_spec(di, 0), _row_spec(di, 0), _row_spec(di, 0), _row_spec(di, 0),
                  _full_spec((1, di)), _full_spec((1, di)), _full_spec((di, D)),
                  _row_spec(D, 0), _mod_spec(d6, 0)],
        out_specs=_row_spec(D, 0),
        out_shape=jax.ShapeDtypeStruct((B, S, D), F32),
        input_output_aliases={7: 0},
        compiler_params=_params("arbitrary", "arbitrary"),
        name="ssm_out",
    )(ys[0], ys[1], xbc, z, dsk, ssm_norm_g[None], w_out.astype(BF16), xs, modtab)


def _router_kernel(x_ref, mod_ref, g_ref, wr_hi_ref, wr_lo_ref, ltri_ref, tok_ref, meta_ref, wt_ref, cnt_ref,
                   run_ref):
    d = x_ref.shape[-1]

    @pl.when((pl.program_id(0) == 0) & (pl.program_id(1) == 0))
    def _():
        run_ref[...] = jnp.zeros(run_ref.shape, F32)

    t = _normmod(x_ref[...], g_ref[...], mod_ref[:, 3 * d:4 * d], mod_ref[:, 4 * d:5 * d])
    tok_ref[...] = t
    logits = _dot_split(t, wr_hi_ref[...], wr_lo_ref[...])
    lane = lax.broadcasted_iota(I32, logits.shape, 1).astype(F32)
    big = float(LANES)
    gl = jnp.where(lane < MOE_GROUPS, logits, NEG_INF)
    gmax = jnp.max(gl, axis=1, keepdims=True)
    gate = 1.0 / jnp.sum(jnp.exp(gl - gmax), axis=1, keepdims=True)
    grp = jnp.min(jnp.where(gl == gmax, lane, big), axis=1, keepdims=True)
    lo = MOE_GROUPS + MOE_EPG * grp
    el = jnp.where((lane >= lo) & (lane < lo + MOE_EPG), logits, NEG_INF)
    v1 = jnp.max(el, axis=1, keepdims=True)
    i1 = jnp.min(jnp.where(el == v1, lane, big), axis=1, keepdims=True)
    el2 = jnp.where(lane == i1, NEG_INF, el)
    v2 = jnp.max(el2, axis=1, keepdims=True)
    i2 = jnp.min(jnp.where(el2 == v2, lane, big), axis=1, keepdims=True)
    e2 = jnp.exp(v2 - v1)
    den = 1.0 + e2
    w1 = gate * (1.0 / den)
    w2 = gate * (e2 / den)
    wt_ref[...] = jnp.where(lane == 0.0, w1, jnp.where(lane == 1.0, w2, 0.0))
    oh1 = (lane == i1).astype(F32)
    oh2 = (lane == i2).astype(F32)
    oh = oh1 + oh2
    before = jnp.dot(ltri_ref[...], oh.astype(BF16), preferred_element_type=F32) + run_ref[...]
    r1 = jnp.sum(oh1 * before, axis=1, keepdims=True)
    r2 = jnp.sum(oh2 * before, axis=1, keepdims=True)
    run = run_ref[...] + jnp.sum(oh, axis=0, keepdims=True)
    run_ref[...] = run
    cnt_ref[...] = jnp.broadcast_to(run, cnt_ref.shape)
    cols = jnp.where(lane == 0.0, i1 - MOE_GROUPS,
                     jnp.where(lane == 1.0, i2 - MOE_GROUPS,
                               jnp.where(lane == 2.0, r1, jnp.where(lane == 3.0, r2, 0.0))))
    meta_ref[...] = cols.T[0:8, :].astype(I32)


def _row_copy(src_ref, src_row, dst_ref, dst_row, sem):
    return pltpu.make_async_copy(src_ref.at[pl.ds(src_row, 1)], dst_ref.at[pl.ds(dst_row, 1)], sem)


def _dest_rows(meta_ref, ps_ref, j):
    return ps_ref[meta_ref[0, j]] + meta_ref[2, j], ps_ref[meta_ref[1, j]] + meta_ref[3, j]


def _dispatch_kernel(meta_ref, ps_ref, tok_ref, init_hbm, xs_hbm, sem):
    del init_hbm

    def body(j, carry):
        d1, d2 = _dest_rows(meta_ref, ps_ref, j)
        _row_copy(tok_ref, j, xs_hbm, d1, sem).start()
        _row_copy(tok_ref, j, xs_hbm, d2, sem).start()
        return carry

    lax.fori_loop(0, TS, body, 0, unroll=DMA_UNROLL)
    for _ in range(2):
        pltpu.make_async_copy(tok_ref, xs_hbm.at[pl.ds(0, TS)], sem).wait()


def _expert_kernel(be_ref, nu_ref, xs_ref, wgu_ref, wd_ref, y_ref, wgu_b_ref, wd_b_ref):
    i = pl.program_id(0)

    @pl.when((i == 0) | (be_ref[i] != be_ref[jnp.maximum(i - 1, 0)]))
    def _():
        wgu_b_ref[...] = wgu_ref[...].astype(BF16)
        wd_b_ref[...] = wd_ref[...].astype(BF16)

    @pl.when(i < nu_ref[0])
    def _():
        gu = jnp.dot(xs_ref[...].astype(BF16), wgu_b_ref[...], preferred_element_type=F32)
        hid = _silu(gu[:, :MOE_HIDDEN]) * gu[:, MOE_HIDDEN:]
        y_ref[...] = jnp.dot(hid.astype(BF16), wd_b_ref[...], preferred_element_type=F32)

    @pl.when(i >= nu_ref[0])
    def _():
        y_ref[...] = jnp.zeros(y_ref.shape, F32)


def _combine_kernel(meta_ref, ps_ref, x_ref, mod_ref, wt_ref, y_hbm, out_ref, buf_ref, sem):
    d = x_ref.shape[-1]

    def body(j, carry):
        d1, d2 = _dest_rows(meta_ref, ps_ref, j)
        _row_copy(y_hbm, d1, buf_ref, j, sem).start()
        _row_copy(y_hbm, d2, buf_ref, TS + j, sem).start()
        return carry

    lax.fori_loop(0, TS, body, 0, unroll=DMA_UNROLL)
    pltpu.make_async_copy(y_hbm.at[pl.ds(0, 2 * TS)], buf_ref, sem).wait()
    f = wt_ref[:, 0:1] * buf_ref[0:TS, :] + wt_ref[:, 1:2] * buf_ref[TS:2 * TS, :]
    out_ref[...] = x_ref[...] + mod_ref[:, 5 * d:6 * d] * f


def _moe_layer(xs, modtab, norm_g, w_rg, w_re, w_gu, w_down, ctx_out):
    B, S, D = xs.shape
    d6 = modtab.shape[-1]
    t0 = 0 if ctx_out else 1
    nt = S // TS - t0
    R = nt * TS
    wr_hi, wr_lo = _split(
        jnp.pad(jnp.concatenate([w_rg, w_re], axis=1), ((0, 0), (0, LANES - MOE_GROUPS - MOE_EXPERTS))))
    out_row = lambda width: pl.BlockSpec((None, TS, width), lambda b, s: (b, s, 0))
    ltri = (jnp.arange(TS)[:, None] > jnp.arange(TS)[None, :]).astype(BF16)
    tok, meta, wt, cnt = pl.pallas_call(
        _router_kernel,
        grid=(B, nt),
        in_specs=[_row_spec(D, t0), _mod_spec(d6, t0), _full_spec((1, D)), _full_spec((D, LANES)),
                  _full_spec((D, LANES)), _full_spec((TS, TS))],
        out_specs=[out_row(D), pl.BlockSpec((None, 8, TS), lambda b, s: (b * nt + s, 0, 0)), out_row(LANES),
                   _full_spec((8, LANES))],
        out_shape=[jax.ShapeDtypeStruct((B, R, D), F32), jax.ShapeDtypeStruct((B * nt, 8, TS), I32),
                   jax.ShapeDtypeStruct((B, R, LANES), F32), jax.ShapeDtypeStruct((8, LANES), F32)],
        scratch_shapes=[pltpu.VMEM((1, LANES), F32)],
        compiler_params=_params("arbitrary", "arbitrary"),
        name="moe_router",
    )(xs, modtab, norm_g[None], wr_hi, wr_lo, ltri)

    T = B * R
    n_blocks = -(-2 * T // MOE_BLOCK) + MOE_EXPERTS
    n_rows = n_blocks * MOE_BLOCK
    counts = cnt[0, MOE_GROUPS:MOE_GROUPS + MOE_EXPERTS].astype(I32)
    padded = (counts + MOE_BLOCK - 1) // MOE_BLOCK * MOE_BLOCK
    pends = jnp.cumsum(padded)
    pstarts = pends - padded
    blk_start = jnp.arange(n_blocks, dtype=I32) * MOE_BLOCK
    block_expert = jnp.minimum(jnp.sum(pends[None, :] <= blk_start[:, None], axis=1),
                               MOE_EXPERTS - 1).astype(I32)
    n_used = (pends[-1:] // MOE_BLOCK).astype(I32)

    hbm = pl.BlockSpec(memory_space=pltpu.MemorySpace.HBM)
    smem = pl.BlockSpec(memory_space=pltpu.SMEM)
    x_sorted = pl.pallas_call(
        _dispatch_kernel,
        grid=(B * nt,),
        in_specs=[pl.BlockSpec((None, 8, TS), lambda i: (i, 0, 0), memory_space=pltpu.SMEM), smem,
                  pl.BlockSpec((TS, D), lambda i: (i, 0)), hbm],
        out_specs=hbm,
        out_shape=jax.ShapeDtypeStruct((n_rows, D), F32),
        scratch_shapes=[pltpu.SemaphoreType.DMA(())],
        input_output_aliases={3: 0},
        compiler_params=_params("arbitrary"),
        name="moe_dispatch",
    )(meta, pstarts, tok.reshape(T, D), jnp.zeros((n_rows, D), F32))

    y_sorted = pl.pallas_call(
        _expert_kernel,
        grid_spec=pltpu.PrefetchScalarGridSpec(
            num_scalar_prefetch=2,
            grid=(n_blocks,),
            in_specs=[pl.BlockSpec((MOE_BLOCK, D), lambda i, be, nu: (i, 0)),
                      pl.BlockSpec((None, D, 2 * MOE_HIDDEN), lambda i, be, nu: (be[i], 0, 0)),
                      pl.BlockSpec((None, MOE_HIDDEN, D), lambda i, be, nu: (be[i], 0, 0))],
            out_specs=pl.BlockSpec((MOE_BLOCK, D), lambda i, be, nu: (i, 0)),
            scratch_shapes=[pltpu.VMEM((D, 2 * MOE_HIDDEN), BF16), pltpu.VMEM((MOE_HIDDEN, D), BF16)],
        ),
        out_shape=jax.ShapeDtypeStruct((n_rows, D), F32),
        compiler_params=_params("arbitrary"),
        name="moe_experts",
    )(block_expert, n_used, x_sorted, w_gu, w_down)

    return pl.pallas_call(
        _combine_kernel,
        grid=(B, nt),
        in_specs=[pl.BlockSpec((None, 8, TS), lambda b, s: (b * nt + s, 0, 0), memory_space=pltpu.SMEM), smem,
                  _row_spec(D, t0), _mod_spec(d6, t0), out_row(LANES), hbm],
        out_specs=_row_spec(D, t0),
        out_shape=jax.ShapeDtypeStruct((B, S, D), F32),
        scratch_shapes=[pltpu.VMEM((2 * TS, D), F32), pltpu.SemaphoreType.DMA(())],
        input_output_aliases={2: 0},
        compiler_params=_params("arbitrary", "arbitrary"),
        name="moe_combine",
    )(meta, pstarts, xs, modtab, wt, y_sorted)


def _final_kernel(x_ref, g_ref, o_ref):
    x = x_ref[...]
    ms = jnp.mean(x * x, axis=-1, keepdims=True)
    o_ref[...] = x * lax.rsqrt(ms + EPS) * g_ref[...]


def kernel(x, c, ctx, c_ctx, w_ada, b_ada, norm_mix_g, norm_ffn_g, final_norm_g, attn_w_qkv, attn_w_o, attn_q_norm_g, attn_k_norm_g, pool_w, pool_scale, ssm_w_in, ssm_conv_w, ssm_conv_b, ssm_dt_bias, ssm_a_log, ssm_d, ssm_norm_g, ssm_w_out, moe_w_router_group, moe_w_router_expert, moe_w_gate_up, moe_w_down):
    B, n, D = x.shape
    m = ctx.shape[1]
    depth = w_ada.shape[0]
    assert m == TS and n % TS == 0 and n % GRID_W == 0
    xs = jnp.concatenate([ctx, x], axis=1)
    mods = _ada(c, c_ctx, w_ada, b_ada)
    cos_t, sin_t = _rope_tables(n, m)
    for i in range(depth):
        kind, j = i % N_MIXERS, i // N_MIXERS
        ctx_out = i < depth - 1
        modtab = jnp.stack([jnp.broadcast_to(mods[i, B], (B, 6 * D)), mods[i, :B]], axis=1)[:, :, None, :]
        if kind == 0:
            xs = _attention_layer(xs, modtab, norm_mix_g[i], attn_w_qkv[j], attn_w_o[j], attn_q_norm_g[j],
                                  attn_k_norm_g[j], cos_t, sin_t, ctx_out)
        elif kind == 1:
            assert ctx_out
            xs = _pool_layer(xs, modtab, norm_mix_g[i], pool_w[j], pool_scale[j], m)
        else:
            assert ctx_out
            xs = _ssd_layer(xs, modtab, norm_mix_g[i], ssm_w_in[j], ssm_conv_w[j], ssm_conv_b[j],
                            ssm_dt_bias[j], ssm_a_log[j], ssm_d[j], ssm_norm_g[j], ssm_w_out[j], m)
        xs = _moe_layer(xs, modtab, norm_ffn_g[i], moe_w_router_group[i], moe_w_router_expert[i],
                        moe_w_gate_up[i], moe_w_down[i], ctx_out)
    return pl.pallas_call(
        _final_kernel,
        grid=(B, n // TS),
        in_specs=[_row_spec(D, m // TS), _full_spec((1, D))],
        out_specs=pl.BlockSpec((None, TS, D), lambda b, s: (b, s, 0)),
        out_shape=jax.ShapeDtypeStruct((B, n, D), F32),
        compiler_params=_params("arbitrary", "arbitrary"),
        name="final_norm",
    )(xs, final_norm_g[None])
```

```python
import functools

import jax
import jax.numpy as jnp
from jax import lax
from jax.experimental import pallas as pl
from jax.experimental.pallas import tpu as pltpu

F32 = jnp.float32
BF16 = jnp.bfloat16
I32 = jnp.int32
HI = lax.Precision.HIGHEST
EPS = 1e-6
NEG_INF = float("-inf")
LOG2E = 1.4426950408889634

TS = 256
LANES = 128
SUBLANES = 8
GRID_W = 64
ROPE_THETA = 10000.0
N_MIXERS = 3

N_HEADS = 16
N_KV = 4
HEAD_DIM = 64
Q_PER_KV = N_HEADS // N_KV

POOL_WINDOWS = (2, 4, 8, 16)
POOL_PAD = 16
ROW_CHUNK = 256

SSM_HEADS = 32
SSM_HEAD_DIM = 64
SSM_GROUPS = 4
SSM_HPG = SSM_HEADS // SSM_GROUPS
SSM_STATE = 128
SSM_CONV = 4
SSM_CHUNK = 128
SSM_D_INNER = SSM_HEADS * SSM_HEAD_DIM
SSM_GN = SSM_GROUPS * SSM_STATE
SSM_CONV_DIM = SSM_D_INNER + 2 * SSM_GN
CONV_PAD = 8

MOE_GROUPS = 4
MOE_EPG = 8
MOE_EXPERTS = MOE_GROUPS * MOE_EPG
MOE_HIDDEN = 512
MOE_BLOCK = 256
DMA_UNROLL = 8


def _params(*sem):
    return pltpu.CompilerParams(dimension_semantics=sem)


def _silu(v):
    return v / (1.0 + jnp.exp(-v))


def _split(v):
    hi = v.astype(BF16)
    return hi, (v - hi.astype(F32)).astype(BF16)


def _dot_split_lhs(a, e):
    hi, lo = _split(a)
    return jnp.dot(hi, e, preferred_element_type=F32) + jnp.dot(lo, e, preferred_element_type=F32)


def _dot_split(a, b_hi, b_lo):
    hi, lo = _split(a)
    return jnp.dot(hi, b_hi, preferred_element_type=F32) + (
        jnp.dot(hi, b_lo, preferred_element_type=F32) + jnp.dot(lo, b_hi, preferred_element_type=F32))


def _normmod(x, g, shift, scale):
    ms = jnp.mean(x * x, axis=-1, keepdims=True)
    return (x * lax.rsqrt(ms + EPS) * g) * (1.0 + scale) + shift


def _ada_kernel(a_ref, w_ref, b_ref, o_ref):
    a = _silu(a_ref[...])
    o_ref[0] = jnp.dot(a, w_ref[0], precision=HI, preferred_element_type=F32) + b_ref[0]


def _ada(c, c_ctx, w_ada, b_ada):
    depth, d, d6 = w_ada.shape
    b = c.shape[0]
    assert b + 1 <= 8
    a = jnp.concatenate([c, c_ctx[None], jnp.zeros((8 - b - 1, d), F32)], axis=0)
    tn = 1536
    return pl.pallas_call(
        _ada_kernel,
        grid=(depth, d6 // tn),
        in_specs=[pl.BlockSpec((8, d), lambda i, j: (0, 0)),
                  pl.BlockSpec((1, d, tn), lambda i, j: (i, 0, j)),
                  pl.BlockSpec((1, 1, tn), lambda i, j: (i, 0, j))],
        out_specs=pl.BlockSpec((1, 8, tn), lambda i, j: (i, 0, j)),
        out_shape=jax.ShapeDtypeStruct((depth, 8, d6), F32),
        compiler_params=_params("arbitrary", "arbitrary"),
        name="ada",
    )(a, w_ada, b_ada.reshape(depth, 1, d6))


def _row_spec(width, t0, col=0):
    return pl.BlockSpec((None, TS, width), lambda b, s: (b, s + t0, col))


def _mod_spec(d6, t0):
    return pl.BlockSpec((None, None, 1, d6), lambda b, s: (b, jnp.minimum(s + t0, 1), 0, 0))


def _full_spec(shape):
    nd = len(shape)
    return pl.BlockSpec(shape, lambda b, s: (0,) * nd)


def _qkv_kernel(x_ref, mod_ref, g_ref, w_ref, gqk_ref, cos_ref, sin_ref, eh_ref, eht_ref,
                qT_ref, k_ref, vT_ref):
    d = x_ref.shape[-1]
    nq = N_HEADS * HEAD_DIM
    nqk = nq + N_KV * HEAD_DIM
    h = _normmod(x_ref[...], g_ref[...], mod_ref[:, 0:d], mod_ref[:, d:2 * d]).astype(BF16)
    qkv = jnp.dot(h, w_ref[...], preferred_element_type=F32)
    qk = qkv[:, :nqk]
    ss = _dot_split_lhs(qk * qk, eh_ref[...])
    rinv = lax.rsqrt(ss * (1.0 / HEAD_DIM) + EPS)
    qk = qk * _dot_split_lhs(rinv, eht_ref[...]) * gqk_ref[...]
    cos = cos_ref[...]
    sin = sin_ref[...]
    lane = lax.broadcasted_iota(I32, (TS, LANES), 1)
    even = (lane & 1) == 0
    blocks = []
    for j in range(nqk // LANES):
        blk = qk[:, LANES * j:LANES * (j + 1)]
        partner = jnp.where(even, pltpu.roll(blk, LANES - 1, 1), pltpu.roll(blk, 1, 1))
        blocks.append(blk * cos + partner * sin)
    q = jnp.concatenate(blocks[:nq // LANES], axis=1)
    k = jnp.concatenate(blocks[nq // LANES:], axis=1)
    qT_ref[...] = q.T.astype(BF16)
    for kv in range(N_KV):
        k_ref[kv] = k[:, HEAD_DIM * kv:HEAD_DIM * (kv + 1)].astype(BF16)
    vT_ref[...] = qkv[:, nqk:].T.astype(BF16)


def _attn_ctx_kernel(qT_ref, k_ref, vT_ref, o_ref):
    for g in range(Q_PER_KV):
        rows = slice(HEAD_DIM * g, HEAD_DIM * (g + 1))
        s = jnp.dot(k_ref[...], qT_ref[rows, :], preferred_element_type=F32)
        p = jnp.exp2(s - jnp.max(s, axis=0, keepdims=True))
        l = jnp.sum(p, axis=0, keepdims=True)
        o = jnp.dot(vT_ref[...], p.astype(BF16), preferred_element_type=F32)
        o_ref[rows, :] = (o / l).astype(BF16)


def _attn_main_kernel(qT_ref, k_ref, vT_ref, o_ref, s0_ref, s1_ref, m0_ref, m1_ref, *, n_units, q_col0):
    for ref in (s0_ref, s1_ref, m0_ref, m1_ref):
        ref[...] = jnp.zeros(ref.shape, F32)

    def unit(i):
        i = jnp.clip(i, 0, n_units - 1)
        row = pl.multiple_of((i % Q_PER_KV) * HEAD_DIM, HEAD_DIM)
        col = pl.multiple_of((i // Q_PER_KV) * TS, TS)
        return row, col

    def step(i, s_w, m_w, s_r, m_r):
        row, col = unit(i)
        q = qT_ref[pl.ds(row, HEAD_DIM), pl.ds(q_col0 + col, TS)]
        s = jnp.dot(k_ref[...], q, preferred_element_type=F32)
        s_w[...] = s
        m_w[...] = jnp.max(s, axis=0, keepdims=True)
        row, col = unit(i - 1)
        p = jnp.exp2(s_r[...] - m_r[...])
        l = jnp.sum(p, axis=0, keepdims=True)
        o = jnp.dot(vT_ref[...], p.astype(BF16), preferred_element_type=F32)
        o_ref[pl.ds(row, HEAD_DIM), pl.ds(col, TS)] = (o / l).astype(BF16)

    def body(j, carry):
        step(2 * j, s0_ref, m0_ref, s1_ref, m1_ref)
        step(2 * j + 1, s1_ref, m1_ref, s0_ref, m0_ref)
        return carry

    lax.fori_loop(0, n_units // 2 + 1, body, 0)


def _oproj_kernel(oT_ref, w_ref, x_ref, mod_ref, out_ref):
    d = x_ref.shape[-1]
    o = oT_ref[...].astype(F32).T.astype(BF16)
    y = jnp.dot(o, w_ref[...], preferred_element_type=F32)
    out_ref[...] = x_ref[...] + mod_ref[:, 2 * d:3 * d] * y


def _rope_tables(n, m):
    rows = n // GRID_W
    row = jnp.broadcast_to(jnp.arange(rows)[:, None], (rows, GRID_W)).reshape(-1).astype(F32)
    col = jnp.broadcast_to(jnp.arange(GRID_W)[None, :], (rows, GRID_W)).reshape(-1).astype(F32)
    n_freq = HEAD_DIM // 4
    inv_freq = ROPE_THETA ** (-jnp.arange(n_freq, dtype=F32) / n_freq)
    ang = jnp.concatenate([row[:, None] * inv_freq, col[:, None] * inv_freq], axis=-1)
    cos = jnp.repeat(jnp.cos(ang), 2, axis=1)
    sign = jnp.tile(jnp.array([-1.0, 1.0], F32), HEAD_DIM // 2)
    sin = jnp.repeat(jnp.sin(ang), 2, axis=1) * sign
    cos = jnp.concatenate([jnp.ones((m, HEAD_DIM), F32), cos], axis=0)
    sin = jnp.concatenate([jnp.zeros((m, HEAD_DIM), F32), sin], axis=0)
    reps = LANES // HEAD_DIM
    return jnp.tile(cos, (1, reps)), jnp.tile(sin, (1, reps))


def _attention_layer(xs, modtab, norm_g, w_qkv, w_o, q_g, k_g, cos_t, sin_t, ctx_out):
    B, S, D = xs.shape
    nt = S // TS
    nq = N_HEADS * HEAD_DIM
    nkv = N_KV * HEAD_DIM
    nqk = nq + nkv
    d6 = modtab.shape[-1]
    q_scale = HEAD_DIM ** -0.5 * LOG2E
    gqk = jnp.concatenate([jnp.tile(q_g, N_HEADS) * q_scale, jnp.tile(k_g, N_KV)])[None]
    head_of = jnp.arange(nqk) // HEAD_DIM
    eh = (head_of[:, None] == jnp.arange(LANES)[None, :]).astype(BF16)
    qT, k4, vT = pl.pallas_call(
        _qkv_kernel,
        grid=(B, nt),
        in_specs=[_row_spec(D, 0), _mod_spec(d6, 0), _full_spec((1, D)), _full_spec((D, nqk + nkv)),
                  _full_spec((1, nqk)),
                  pl.BlockSpec((TS, LANES), lambda b, s: (s, 0)),
                  pl.BlockSpec((TS, LANES), lambda b, s: (s, 0)),
                  _full_spec((nqk, LANES)), _full_spec((LANES, nqk))],
        out_specs=[pl.BlockSpec((None, nq, TS), lambda b, s: (b, 0, s)),
                   pl.BlockSpec((None, N_KV, TS, HEAD_DIM), lambda b, s: (b, 0, s, 0)),
                   pl.BlockSpec((None, nkv, TS), lambda b, s: (b, 0, s))],
        out_shape=[jax.ShapeDtypeStruct((B, nq, S), BF16),
                   jax.ShapeDtypeStruct((B, N_KV, S, HEAD_DIM), BF16),
                   jax.ShapeDtypeStruct((B, nkv, S), BF16)],
        compiler_params=_params("arbitrary", "arbitrary"),
        name="attn_qkv",
    )(xs, modtab, norm_g[None], w_qkv.astype(BF16), gqk, cos_t, sin_t, eh, eh.T)


    gw = Q_PER_KV * HEAD_DIM
    n_lat = S - TS
    w_o = w_o.astype(BF16)
    if ctx_out:
        oT_ctx = pl.pallas_call(
            _attn_ctx_kernel,
            grid=(B, N_KV),
            in_specs=[pl.BlockSpec((None, gw, TS), lambda b, kv: (b, kv, 0)),
                      pl.BlockSpec((None, None, TS, HEAD_DIM), lambda b, kv: (b, kv, 0, 0)),
                      pl.BlockSpec((None, HEAD_DIM, TS), lambda b, kv: (b, kv, 0))],
            out_specs=pl.BlockSpec((None, gw, TS), lambda b, kv: (b, kv, 0)),
            out_shape=jax.ShapeDtypeStruct((B, nq, TS), BF16),
            compiler_params=_params("arbitrary", "arbitrary"),
            name="attn_ctx",
        )(qT, k4, vT)
        xs = pl.pallas_call(
            _oproj_kernel,
            grid=(B, 1),
            in_specs=[pl.BlockSpec((None, nq, TS), lambda b, s: (b, 0, 0)), _full_spec((nq, D)),
                      _row_spec(D, 0), _mod_spec(d6, 0)],
            out_specs=_row_spec(D, 0),
            out_shape=jax.ShapeDtypeStruct((B, S, D), F32),
            input_output_aliases={2: 0},
            compiler_params=_params("arbitrary", "arbitrary"),
            name="attn_oproj_ctx",
        )(oT_ctx, w_o, xs, modtab)

    n_units = (n_lat // TS) * Q_PER_KV
    assert n_units % 2 == 0
    oT = pl.pallas_call(
        functools.partial(_attn_main_kernel, n_units=n_units, q_col0=TS),
        grid=(B, N_KV),
        in_specs=[pl.BlockSpec((None, gw, S), lambda b, kv: (b, kv, 0)),
                  pl.BlockSpec((None, None, S, HEAD_DIM), lambda b, kv: (b, kv, 0, 0)),
                  pl.BlockSpec((None, HEAD_DIM, S), lambda b, kv: (b, kv, 0))],
        out_specs=pl.BlockSpec((None, gw, n_lat), lambda b, kv: (b, kv, 0)),
        out_shape=jax.ShapeDtypeStruct((B, nq, n_lat), BF16),
        scratch_shapes=[pltpu.VMEM((S, TS), F32), pltpu.VMEM((S, TS), F32),
                        pltpu.VMEM((1, TS), F32), pltpu.VMEM((1, TS), F32)],
        compiler_params=_params("arbitrary", "arbitrary"),
        name="attn_core",
    )(qT, k4, vT)

    return pl.pallas_call(
        _oproj_kernel,
        grid=(B, n_lat // TS),
        in_specs=[pl.BlockSpec((None, nq, TS), lambda b, s: (b, 0, s)), _full_spec((nq, D)),
                  _row_spec(D, 1), _mod_spec(d6, 1)],
        out_specs=_row_spec(D, 1),
        out_shape=jax.ShapeDtypeStruct((B, S, D), F32),
        input_output_aliases={2: 0},
        compiler_params=_params("arbitrary", "arbitrary"),
        name="attn_oproj",
    )(oT, w_o, xs, modtab)


def _normmod_kernel(x_ref, mod_ref, g_ref, h_ref):
    d = x_ref.shape[-1]
    h_ref[...] = _normmod(x_ref[...], g_ref[...], mod_ref[:, 0:d], mod_ref[:, d:2 * d])


def _pool_kernel(h_ref, x_ref, w_ref, ps_ref, gate_ref, out_ref, pad_ref, *, segs):
    gi = pl.program_id(1)
    zeros = jnp.zeros((POOL_PAD, h_ref.shape[-1]), F32)
    for widx, win in enumerate(POOL_WINDOWS):
        half = win // 2

        @pl.when(gi == widx)
        def _(half=half):
            for si, (r0, n) in enumerate(segs):
                pad_ref[0:POOL_PAD, :] = zeros
                pad_ref[POOL_PAD:POOL_PAD + n, :] = h_ref[r0:r0 + n, :]
                pad_ref[POOL_PAD + n:2 * POOL_PAD + n, :] = zeros
                gate = gate_ref[si]
                for c0 in range(0, n, ROW_CHUNK):
                    base = POOL_PAD + c0
                    acc = pad_ref[base - half:base - half + ROW_CHUNK, :]
                    for j in range(-half + 1, half):
                        acc = acc + pad_ref[base + j:base + j + ROW_CHUNK, :]
                    t = c0 + lax.broadcasted_iota(I32, (ROW_CHUNK, 1), 0)
                    cnt = jnp.minimum(t + half, n) - jnp.maximum(t - half, 0)
                    diff = acc / cnt.astype(F32) - pad_ref[base:base + ROW_CHUNK, :]
                    y = jnp.dot(diff.astype(BF16), w_ref[...], preferred_element_type=F32) * ps_ref[...]
                    rows = slice(r0 + c0, r0 + c0 + ROW_CHUNK)
                    out_ref[rows, :] = x_ref[rows, :] + gate * y


def _pool_layer(xs, modtab, norm_g, w_pool, pool_scale, m):
    B, S, D = xs.shape
    nt = S // TS
    d6 = modtab.shape[-1]
    pg = D // len(POOL_WINDOWS)
    h = pl.pallas_call(
        _normmod_kernel,
        grid=(B, nt),
        in_specs=[_row_spec(D, 0), _mod_spec(d6, 0), _full_spec((1, D))],
        out_specs=_row_spec(D, 0),
        out_shape=jax.ShapeDtypeStruct((B, S, D), F32),
        compiler_params=_params("arbitrary", "arbitrary"),
        name="pool_normmod",
    )(xs, modtab, norm_g[None])
    segs = ((0, m), (m, S - m))
    seq_spec = pl.BlockSpec((None, S, pg), lambda b, g: (b, 0, g))
    return pl.pallas_call(
        functools.partial(_pool_kernel, segs=segs),
        grid=(B, len(POOL_WINDOWS)),
        in_specs=[seq_spec, seq_spec,
                  pl.BlockSpec((None, pg, pg), lambda b, g: (g, 0, 0)),
                  pl.BlockSpec((1, pg), lambda b, g: (0, g)),
                  pl.BlockSpec((None, 2, 1, pg), lambda b, g: (b, 0, 0, 2 * (D // pg) + g))],
        out_specs=seq_spec,
        out_shape=jax.ShapeDtypeStruct((B, S, D), F32),
        scratch_shapes=[pltpu.VMEM((S - m + 2 * POOL_PAD, pg), F32)],
        compiler_params=_params("arbitrary", "arbitrary"),
        name="pool_mix",
    )(h, xs, w_pool.astype(BF16), pool_scale[None], modtab)


def _ssm_in_kernel(x_ref, mod_ref, g_ref, wz_ref, wx_ref, wdt_hi_ref, wdt_lo_ref, z_ref, xbc_ref, dt_ref):
    d = x_ref.shape[-1]
    hf = _normmod(x_ref[...], g_ref[...], mod_ref[:, 0:d], mod_ref[:, d:2 * d])
    h = hf.astype(BF16)
    z_ref[...] = jnp.dot(h, wz_ref[...], preferred_element_type=F32)
    xbc_ref[...] = jnp.dot(h, wx_ref[...], preferred_element_type=F32)
    dt_ref[...] = _dot_split(hf, wdt_hi_ref[...], wdt_lo_ref[...])


def _conv_kernel(u_ref, w_ref, b_ref, o_ref, pad_ref, *, segs):
    zeros = jnp.zeros((CONV_PAD, u_ref.shape[-1]), F32)
    for r0, n in segs:
        pad_ref[0:CONV_PAD, :] = zeros
        pad_ref[CONV_PAD:CONV_PAD + n, :] = u_ref[r0:r0 + n, :]
        pad_ref[CONV_PAD + n:2 * CONV_PAD + n, :] = zeros
        for c0 in range(0, n, ROW_CHUNK):
            acc = b_ref[...]
            for k in range(SSM_CONV):
                lo = CONV_PAD + c0 - SSM_CONV // 2 + k
                acc = acc + w_ref[k:k + 1, :] * pad_ref[lo:lo + ROW_CHUNK, :]
            o_ref[r0 + c0:r0 + c0 + ROW_CHUNK, :] = _silu(acc)


def _ssd_kernel(xbc_ref, dt_ref, dtb_ref, a_ref, tri_ref, sel_ref, y_ref, h_ref, *, direction):
    q = SSM_CHUNK
    lane0 = SSM_HEADS * direction

    @pl.when(pl.program_id(1) == 0)
    def _():
        h_ref[...] = jnp.zeros(h_ref.shape, F32)

    raw = dt_ref[...] + dtb_ref[...]
    dt = jnp.maximum(raw, 0.0) + jnp.log1p(jnp.exp(-jnp.abs(raw)))
    la = dt * a_ref[...]
    cs = jnp.dot(tri_ref[...], la, precision=HI, preferred_element_type=F32)
    tot = cs[q - 1:q, :]
    if direction == 0:
        u = cs
        dec_in = jnp.exp(u)
        dec_end = jnp.exp(tot - u)
    else:
        u = la - cs
        dec_in = jnp.exp(tot + u)
        dec_end = jnp.exp(-u)
    uT = u.T
    dtT = dt.T
    stacked = jnp.concatenate([dec_in, dt * dec_end, jnp.broadcast_to(jnp.exp(tot), (8, LANES))], axis=0)
    spread = _dot_split_lhs(stacked, sel_ref[...])
    din_col = spread[0:q]
    dtot_col = spread[2 * q:2 * q + 1]
    x = xbc_ref[:, 0:SSM_D_INNER]
    x_b = x.astype(BF16)
    xd_b = (x * spread[q:2 * q]).astype(BF16)
    row = lax.broadcasted_iota(I32, (q, q), 0)
    col = lax.broadcasted_iota(I32, (q, q), 1)
    mask = (row >= col) if direction == 0 else (col >= row)
    gw = SSM_HPG * SSM_HEAD_DIM
    for g in range(SSM_GROUPS):
        b_lo = SSM_D_INNER + SSM_STATE * g
        c_lo = SSM_D_INNER + SSM_GN + SSM_STATE * g
        bT = xbc_ref[:, b_lo:b_lo + SSM_STATE].T.astype(BF16)
        cg = xbc_ref[:, c_lo:c_lo + SSM_STATE].astype(BF16)
        scores = jnp.dot(cg, bT, preferred_element_type=F32)
        hg = h_ref[g]
        cols = slice(gw * g, gw * (g + 1))
        y_off = jnp.dot(cg, hg.astype(BF16), preferred_element_type=F32) * din_col[:, cols]
        for r in range(SSM_HPG):
            hh = SSM_HPG * g + r
            hl = lane0 + hh
            seg = u[:, hl:hl + 1] - uT[hl:hl + 1, :]
            decay = jnp.exp(jnp.where(mask, seg, NEG_INF))
            hc = slice(SSM_HEAD_DIM * hh, SSM_HEAD_DIM * (hh + 1))
            mix = (scores * decay * dtT[hl:hl + 1, :]).astype(BF16)
            y_d = jnp.dot(mix, x_b[:, hc], preferred_element_type=F32)
            y_ref[:, hc] = y_d + y_off[:, SSM_HEAD_DIM * r:SSM_HEAD_DIM * (r + 1)]
        h_ref[g] = hg * dtot_col[:, cols] + jnp.dot(bT, xd_b[:, cols], preferred_element_type=F32)


def _ssm_out_kernel(y0_ref, y1_ref, xh_ref, z_ref, dsk_ref, ng_ref, w_ref, x_ref, mod_ref, out_ref):
    d = x_ref.shape[-1]
    y = y0_ref[...] + y1_ref[...] + dsk_ref[...] * xh_ref[...]
    g = y * _silu(z_ref[...])
    gsz = SSM_D_INNER // SSM_GROUPS
    parts = []
    for k in range(SSM_GROUPS):
        gk = g[:, gsz * k:gsz * (k + 1)]
        parts.append(gk * lax.rsqrt(jnp.mean(gk * gk, axis=-1, keepdims=True) + EPS))
    gn = (jnp.concatenate(parts, axis=1) * ng_ref[...]).astype(BF16)
    o = jnp.dot(gn, w_ref[...], preferred_element_type=F32)
    out_ref[...] = x_ref[...] + mod_ref[:, 2 * d:3 * d] * o


def _ssd_layer(xs, modtab, norm_g, w_in, conv_w, conv_b, dt_bias, a_log, d_skip, ssm_norm_g, w_out, m):
    B, S, D = xs.shape
    nt = S // TS
    d6 = modtab.shape[-1]
    di = SSM_D_INNER
    cd = SSM_CONV_DIM
    wz = w_in[:, :di].astype(BF16)
    wx = w_in[:, di:di + cd].astype(BF16)
    wdt_hi, wdt_lo = _split(jnp.pad(w_in[:, di + cd:], ((0, 0), (0, LANES - 2 * SSM_HEADS))))
    z, xbc_raw, dt_raw = pl.pallas_call(
        _ssm_in_kernel,
        grid=(B, nt),
        in_specs=[_row_spec(D, 0), _mod_spec(d6, 0), _full_spec((1, D)), _full_spec((D, di)),
                  _full_spec((D, cd)), _full_spec((D, LANES)), _full_spec((D, LANES))],
        out_specs=[_row_spec(di, 0), _row_spec(cd, 0), _row_spec(LANES, 0)],
        out_shape=[jax.ShapeDtypeStruct((B, S, di), F32), jax.ShapeDtypeStruct((B, S, cd), F32),
                   jax.ShapeDtypeStruct((B, S, LANES), F32)],
        compiler_params=_params("arbitrary", "arbitrary"),
        name="ssm_in",
    )(xs, modtab, norm_g[None], wz, wx, wdt_hi, wdt_lo)

    segs = ((0, m), (m, S - m))
    cw = 256
    seq_spec = pl.BlockSpec((None, S, cw), lambda b, j: (b, 0, j))
    xbc = pl.pallas_call(
        functools.partial(_conv_kernel, segs=segs),
        grid=(B, cd // cw),
        in_specs=[seq_spec, pl.BlockSpec((SSM_CONV, cw), lambda b, j: (0, j)),
                  pl.BlockSpec((1, cw), lambda b, j: (0, j))],
        out_specs=seq_spec,
        out_shape=jax.ShapeDtypeStruct((B, S, cd), F32),
        scratch_shapes=[pltpu.VMEM((S - m + 2 * CONV_PAD, cw), F32)],
        compiler_params=_params("arbitrary", "arbitrary"),
        name="ssm_conv",
    )(xbc_raw, conv_w, conv_b[None])

    q = SSM_CHUNK
    nc = S // q
    mc = m // q
    pad = LANES - 2 * SSM_HEADS
    dtb = jnp.pad(dt_bias.reshape(-1), (0, pad))[None]
    a_neg = jnp.pad(-jnp.exp(a_log.reshape(-1)), (0, pad))[None]
    tri = (jnp.arange(q)[:, None] >= jnp.arange(q)[None, :]).astype(F32)
    ys = []
    for direction in range(2):
        if direction == 0:
            cmap = lambda c: c
        else:
            cmap = lambda c: jnp.where(c < mc, mc - 1 - c, nc - 1 - (c - mc))
        head_lane = SSM_HEADS * direction + jnp.arange(di) // SSM_HEAD_DIM
        sel = (jnp.arange(LANES)[:, None] == head_lane[None, :]).astype(BF16)
        ys.append(pl.pallas_call(
            functools.partial(_ssd_kernel, direction=direction),
            grid=(B, nc),
            in_specs=[pl.BlockSpec((None, q, cd), lambda b, c, cmap=cmap: (b, cmap(c), 0)),
                      pl.BlockSpec((None, q, LANES), lambda b, c, cmap=cmap: (b, cmap(c), 0)),
                      _full_spec((1, LANES)), _full_spec((1, LANES)), _full_spec((q, q)),
                      _full_spec((LANES, di))],
            out_specs=pl.BlockSpec((None, q, di), lambda b, c, cmap=cmap: (b, cmap(c), 0)),
            out_shape=jax.ShapeDtypeStruct((B, S, di), F32),
            scratch_shapes=[pltpu.VMEM((SSM_GROUPS, SSM_STATE, SSM_HPG * SSM_HEAD_DIM), F32)],
            compiler_params=_params("arbitrary", "arbitrary"),
            name=f"ssd_scan{direction}",
        )(xbc, dt_raw, dtb, a_neg, tri, sel))

    dsk = jnp.repeat(d_skip[0] + d_skip[1], SSM_HEAD_DIM)[None]
    return pl.pallas_call(
        _ssm_out_kernel,
        grid=(B, nt),
        in_specs=[_row_spec(di, 0), _row_spec(di, 0), _row_spec(di, 0), _row_spec(di, 0),
                  _full_spec((1, di)), _full_spec((1, di)), _full_spec((di, D)),
                  _row_spec(D, 0), _mod_spec(d6, 0)],
        out_specs=_row_spec(D, 0),
        out_shape=jax.ShapeDtypeStruct((B, S, D), F32),
        input_output_aliases={7: 0},
        compiler_params=_params("arbitrary", "arbitrary"),
        name="ssm_out",
    )(ys[0], ys[1], xbc, z, dsk, ssm_norm_g[None], w_out.astype(BF16), xs, modtab)


def _store_slabs(ref, v):
    rows = v.shape[0]
    for s in range(SUBLANES):
        ref[pl.ds(s, rows, stride=SUBLANES), :] = v[:, LANES * s:LANES * (s + 1)]


def _load_slab_chunk(ref, rows, s, row0=0):
    return ref[pl.ds(SUBLANES * row0 + s, rows, stride=SUBLANES), :]


def _slab(ref, row):
    return ref.at[pl.ds(pl.multiple_of(row * SUBLANES, SUBLANES), SUBLANES)]


def _router_kernel(x_ref, mod_ref, g_ref, wr_hi_ref, wr_lo_ref, ltri_ref, tok_ref, meta_ref, wt_ref, cnt_ref,
                   run_ref):
    d = x_ref.shape[-1]

    @pl.when((pl.program_id(0) == 0) & (pl.program_id(1) == 0))
    def _():
        run_ref[...] = jnp.zeros(run_ref.shape, F32)

    t = _normmod(x_ref[...], g_ref[...], mod_ref[:, 3 * d:4 * d], mod_ref[:, 4 * d:5 * d])
    _store_slabs(tok_ref, t)
    logits = _dot_split(t, wr_hi_ref[...], wr_lo_ref[...])
    lane = lax.broadcasted_iota(I32, logits.shape, 1).astype(F32)
    big = float(LANES)
    gl = jnp.where(lane < MOE_GROUPS, logits, NEG_INF)
    gmax = jnp.max(gl, axis=1, keepdims=True)
    gate = 1.0 / jnp.sum(jnp.exp(gl - gmax), axis=1, keepdims=True)
    grp = jnp.min(jnp.where(gl == gmax, lane, big), axis=1, keepdims=True)
    lo = MOE_GROUPS + MOE_EPG * grp
    el = jnp.where((lane >= lo) & (lane < lo + MOE_EPG), logits, NEG_INF)
    v1 = jnp.max(el, axis=1, keepdims=True)
    i1 = jnp.min(jnp.where(el == v1, lane, big), axis=1, keepdims=True)
    el2 = jnp.where(lane == i1, NEG_INF, el)
    v2 = jnp.max(el2, axis=1, keepdims=True)
    i2 = jnp.min(jnp.where(el2 == v2, lane, big), axis=1, keepdims=True)
    e2 = jnp.exp(v2 - v1)
    den = 1.0 + e2
    w1 = gate * (1.0 / den)
    w2 = gate * (e2 / den)
    wt_ref[...] = jnp.where(lane == 0.0, w1, jnp.where(lane == 1.0, w2, 0.0))
    oh1 = (lane == i1).astype(F32)
    oh2 = (lane == i2).astype(F32)
    oh = oh1 + oh2
    before = jnp.dot(ltri_ref[...], oh.astype(BF16), preferred_element_type=F32) + run_ref[...]
    r1 = jnp.sum(oh1 * before, axis=1, keepdims=True)
    r2 = jnp.sum(oh2 * before, axis=1, keepdims=True)
    run = run_ref[...] + jnp.sum(oh, axis=0, keepdims=True)
    run_ref[...] = run
    cnt_ref[...] = jnp.broadcast_to(run, cnt_ref.shape)
    cols = jnp.where(lane == 0.0, i1 - MOE_GROUPS,
                     jnp.where(lane == 1.0, i2 - MOE_GROUPS,
                               jnp.where(lane == 2.0, r1, jnp.where(lane == 3.0, r2, 0.0))))
    meta_ref[...] = cols.T[0:8, :].astype(I32)


def _row_copy(src_ref, src_row, dst_ref, dst_row, sem):
    return pltpu.make_async_copy(_slab(src_ref, src_row), _slab(dst_ref, dst_row), sem)


def _dispatch_kernel(dest_ref, tok_ref, init_hbm, xs_hbm, sem):
    del init_hbm

    def body(j, carry):
        _row_copy(tok_ref, j, xs_hbm, dest_ref[0, j], sem).start()
        _row_copy(tok_ref, j, xs_hbm, dest_ref[1, j], sem).start()
        return carry

    lax.fori_loop(0, TS, body, 0, unroll=DMA_UNROLL)
    for _ in range(2):
        pltpu.make_async_copy(tok_ref, xs_hbm.at[pl.ds(0, TS * SUBLANES)], sem).wait()


def _expert_kernel(be_ref, nu_ref, xs_ref, wgu_ref, wd_ref, y_ref, wgu_b_ref, wd_b_ref):
    i = pl.program_id(0)
    rows = xs_ref.shape[0] // SUBLANES

    @pl.when((i == 0) | (be_ref[i] != be_ref[jnp.maximum(i - 1, 0)]))
    def _():
        wgu_b_ref[...] = wgu_ref[...].astype(BF16)
        wd_b_ref[...] = wd_ref[...].astype(BF16)

    @pl.when(i < nu_ref[0])
    def _():
        x = jnp.concatenate([_load_slab_chunk(xs_ref, rows, s) for s in range(SUBLANES)], axis=1)
        gu = jnp.dot(x.astype(BF16), wgu_b_ref[...], preferred_element_type=F32)
        hid = _silu(gu[:, :MOE_HIDDEN]) * gu[:, MOE_HIDDEN:]
        _store_slabs(y_ref, jnp.dot(hid.astype(BF16), wd_b_ref[...], preferred_element_type=F32))

    @pl.when(i >= nu_ref[0])
    def _():
        y_ref[...] = jnp.zeros(y_ref.shape, F32)


def _combine_kernel(dest_ref, x_ref, mod_ref, wt_ref, y_hbm, out_ref, buf_ref, sem):
    d = x_ref.shape[-1]

    def body(j, carry):
        _row_copy(y_hbm, dest_ref[0, j], buf_ref, j, sem).start()
        _row_copy(y_hbm, dest_ref[1, j], buf_ref, TS + j, sem).start()
        return carry

    lax.fori_loop(0, TS, body, 0, unroll=DMA_UNROLL)
    pltpu.make_async_copy(y_hbm.at[pl.ds(0, 2 * TS * SUBLANES)], buf_ref, sem).wait()
    w1 = wt_ref[:, 0:1]
    w2 = wt_ref[:, 1:2]
    for s in range(SUBLANES):
        cols = slice(LANES * s, LANES * (s + 1))
        f = w1 * _load_slab_chunk(buf_ref, TS, s) + w2 * _load_slab_chunk(buf_ref, TS, s, row0=TS)
        out_ref[:, cols] = x_ref[:, cols] + mod_ref[:, 5 * d + LANES * s:5 * d + LANES * (s + 1)] * f


def _moe_layer(xs, modtab, norm_g, w_rg, w_re, w_gu, w_down, layer, ctx_out):
    B, S, D = xs.shape
    assert D == SUBLANES * LANES
    d6 = modtab.shape[-1]
    t0 = 0 if ctx_out else 1
    nt = S // TS - t0
    R = nt * TS
    wr_hi, wr_lo = _split(
        jnp.pad(jnp.concatenate([w_rg, w_re], axis=1), ((0, 0), (0, LANES - MOE_GROUPS - MOE_EXPERTS))))
    out_row = lambda width: pl.BlockSpec((None, TS, width), lambda b, s: (b, s, 0))
    ltri = (jnp.arange(TS)[:, None] > jnp.arange(TS)[None, :]).astype(BF16)
    tok, meta, wt, cnt = pl.pallas_call(
        _router_kernel,
        grid=(B, nt),
        in_specs=[_row_spec(D, t0), _mod_spec(d6, t0), _full_spec((1, D)), _full_spec((D, LANES)),
                  _full_spec((D, LANES)), _full_spec((TS, TS))],
        out_specs=[pl.BlockSpec((TS * SUBLANES, LANES), lambda b, s: (b * nt + s, 0)),
                   pl.BlockSpec((None, 8, TS), lambda b, s: (b * nt + s, 0, 0)), out_row(LANES),
                   _full_spec((8, LANES))],
        out_shape=[jax.ShapeDtypeStruct((B * R * SUBLANES, LANES), F32), jax.ShapeDtypeStruct((B * nt, 8, TS), I32),
                   jax.ShapeDtypeStruct((B, R, LANES), F32), jax.ShapeDtypeStruct((8, LANES), F32)],
        scratch_shapes=[pltpu.VMEM((1, LANES), F32)],
        compiler_params=_params("arbitrary", "arbitrary"),
        name="moe_router",
    )(xs, modtab, norm_g[None], wr_hi, wr_lo, ltri)

    T = B * R
    n_blocks = -(-2 * T // MOE_BLOCK) + MOE_EXPERTS
    n_rows = n_blocks * MOE_BLOCK
    counts = cnt[0, MOE_GROUPS:MOE_GROUPS + MOE_EXPERTS].astype(I32)
    padded = (counts + MOE_BLOCK - 1) // MOE_BLOCK * MOE_BLOCK
    pends = jnp.cumsum(padded)
    pstarts = pends - padded
    blk_start = jnp.arange(n_blocks, dtype=I32) * MOE_BLOCK
    block_expert = jnp.minimum(jnp.sum(pends[None, :] <= blk_start[:, None], axis=1),
                               MOE_EXPERTS - 1).astype(I32)
    n_used = (pends[-1:] // MOE_BLOCK).astype(I32)

    hbm = pl.BlockSpec(memory_space=pltpu.MemorySpace.HBM)
    expert_start = jnp.sum(jnp.where(meta[:, 0:2, :, None] == jnp.arange(MOE_EXPERTS), pstarts, 0), axis=-1)
    dest = expert_start + meta[:, 2:4, :]
    x_sorted = pl.pallas_call(
        _dispatch_kernel,
        grid=(B * nt,),
        in_specs=[pl.BlockSpec((None, 2, TS), lambda i: (i, 0, 0), memory_space=pltpu.SMEM),
                  pl.BlockSpec((TS * SUBLANES, LANES), lambda i: (i, 0)), hbm],
        out_specs=hbm,
        out_shape=jax.ShapeDtypeStruct((n_rows * SUBLANES, LANES), F32),
        scratch_shapes=[pltpu.SemaphoreType.DMA(())],
        input_output_aliases={2: 0},
        compiler_params=_params("arbitrary"),
        name="moe_dispatch",
    )(dest, tok, jnp.zeros((n_rows * SUBLANES, LANES), F32))

    blk_spec = pl.BlockSpec((MOE_BLOCK * SUBLANES, LANES), lambda i, be, nu: (i, 0))
    y_sorted = pl.pallas_call(
        _expert_kernel,
        grid_spec=pltpu.PrefetchScalarGridSpec(
            num_scalar_prefetch=2,
            grid=(n_blocks,),
            in_specs=[blk_spec,
                      pl.BlockSpec((None, None, D, 2 * MOE_HIDDEN), lambda i, be, nu: (layer, be[i], 0, 0)),
                      pl.BlockSpec((None, None, MOE_HIDDEN, D), lambda i, be, nu: (layer, be[i], 0, 0))],
            out_specs=blk_spec,
            scratch_shapes=[pltpu.VMEM((D, 2 * MOE_HIDDEN), BF16), pltpu.VMEM((MOE_HIDDEN, D), BF16)],
        ),
        out_shape=jax.ShapeDtypeStruct((n_rows * SUBLANES, LANES), F32),
        compiler_params=_params("arbitrary"),
        name="moe_experts",
    )(block_expert, n_used, x_sorted, w_gu, w_down)

    return pl.pallas_call(
        _combine_kernel,
        grid=(B, nt),
        in_specs=[pl.BlockSpec((None, 2, TS), lambda b, s: (b * nt + s, 0, 0), memory_space=pltpu.SMEM),
                  _row_spec(D, t0), _mod_spec(d6, t0), out_row(LANES), hbm],
        out_specs=_row_spec(D, t0),
        out_shape=jax.ShapeDtypeStruct((B, S, D), F32),
        scratch_shapes=[pltpu.VMEM((2 * TS * SUBLANES, LANES), F32), pltpu.SemaphoreType.DMA(())],
        input_output_aliases={1: 0},
        compiler_params=_params("arbitrary", "arbitrary"),
        name="moe_combine",
    )(dest, xs, modtab, wt, y_sorted)


def _final_kernel(x_ref, g_ref, o_ref):
    x = x_ref[...]
    ms = jnp.mean(x * x, axis=-1, keepdims=True)
    o_ref[...] = x * lax.rsqrt(ms + EPS) * g_ref[...]


def kernel(x, c, ctx, c_ctx, w_ada, b_ada, norm_mix_g, norm_ffn_g, final_norm_g, attn_w_qkv, attn_w_o, attn_q_norm_g, attn_k_norm_g, pool_w, pool_scale, ssm_w_in, ssm_conv_w, ssm_conv_b, ssm_dt_bias, ssm_a_log, ssm_d, ssm_norm_g, ssm_w_out, moe_w_router_group, moe_w_router_expert, moe_w_gate_up, moe_w_down):
    B, n, D = x.shape
    m = ctx.shape[1]
    depth = w_ada.shape[0]
    assert m == TS and n % TS == 0 and n % GRID_W == 0
    xs = jnp.concatenate([ctx, x], axis=1)
    mods = _ada(c, c_ctx, w_ada, b_ada)
    cos_t, sin_t = _rope_tables(n, m)
    for i in range(depth):
        kind, j = i % N_MIXERS, i // N_MIXERS
        ctx_out = i < depth - 1
        modtab = jnp.stack([jnp.broadcast_to(mods[i, B], (B, 6 * D)), mods[i, :B]], axis=1)[:, :, None, :]
        if kind == 0:
            xs = _attention_layer(xs, modtab, norm_mix_g[i], attn_w_qkv[j], attn_w_o[j], attn_q_norm_g[j],
                                  attn_k_norm_g[j], cos_t, sin_t, ctx_out)
        elif kind == 1:
            assert ctx_out
            xs = _pool_layer(xs, modtab, norm_mix_g[i], pool_w[j], pool_scale[j], m)
        else:
            assert ctx_out
            xs = _ssd_layer(xs, modtab, norm_mix_g[i], ssm_w_in[j], ssm_conv_w[j], ssm_conv_b[j],
                            ssm_dt_bias[j], ssm_a_log[j], ssm_d[j], ssm_norm_g[j], ssm_w_out[j], m)
        xs = _moe_layer(xs, modtab, norm_ffn_g[i], moe_w_router_group[i], moe_w_router_expert[i],
                        moe_w_gate_up, moe_w_down, i, ctx_out)
    return pl.pallas_call(
        _final_kernel,
        grid=(B, n // TS),
        in_specs=[_row_spec(D, m // TS), _full_spec((1, D))],
        out_specs=pl.BlockSpec((None, TS, D), lambda b, s: (b, s, 0)),
        out_shape=jax.ShapeDtypeStruct((B, n, D), F32),
        compiler_params=_params("arbitrary", "arbitrary"),
        name="final_norm",
    )(xs, final_norm_g[None])
```

```python
import functools

import jax
import jax.numpy as jnp
from jax import lax
from jax.experimental import pallas as pl
from jax.experimental.pallas import tpu as pltpu

F32 = jnp.float32
BF16 = jnp.bfloat16
I32 = jnp.int32
HI = lax.Precision.HIGHEST
EPS = 1e-6
NEG_INF = float("-inf")
LOG2E = 1.4426950408889634

TS = 256
LANES = 128
SUBLANES = 8
GRID_W = 64
ROPE_THETA = 10000.0
N_MIXERS = 3

N_HEADS = 16
N_KV = 4
HEAD_DIM = 64
Q_PER_KV = N_HEADS // N_KV

POOL_WINDOWS = (2, 4, 8, 16)
POOL_PAD = 16
ROW_CHUNK = 256

SSM_HEADS = 32
SSM_HEAD_DIM = 64
SSM_GROUPS = 4
SSM_HPG = SSM_HEADS // SSM_GROUPS
SSM_STATE = 128
SSM_CONV = 4
SSM_CHUNK = 128
SSM_D_INNER = SSM_HEADS * SSM_HEAD_DIM
SSM_GN = SSM_GROUPS * SSM_STATE
SSM_CONV_DIM = SSM_D_INNER + 2 * SSM_GN
CONV_PAD = 8

MOE_GROUPS = 4
MOE_EPG = 8
MOE_EXPERTS = MOE_GROUPS * MOE_EPG
MOE_HIDDEN = 512
MOE_BLOCK = 256
DMA_UNROLL = 8
DISPATCH_TILES = 4


def _params(*sem):
    return pltpu.CompilerParams(dimension_semantics=sem)


def _silu(v):
    return v / (1.0 + jnp.exp(-v))


def _split(v):
    hi = v.astype(BF16)
    return hi, (v - hi.astype(F32)).astype(BF16)


def _dot_split_lhs(a, e):
    hi, lo = _split(a)
    return jnp.dot(hi, e, preferred_element_type=F32) + jnp.dot(lo, e, preferred_element_type=F32)


def _dot_split(a, b_hi, b_lo):
    hi, lo = _split(a)
    return jnp.dot(hi, b_hi, preferred_element_type=F32) + (
        jnp.dot(hi, b_lo, preferred_element_type=F32) + jnp.dot(lo, b_hi, preferred_element_type=F32))


def _normmod(x, g, shift, scale):
    ms = jnp.mean(x * x, axis=-1, keepdims=True)
    return (x * lax.rsqrt(ms + EPS) * g) * (1.0 + scale) + shift


def _ada_kernel(a_ref, w_ref, b_ref, o_ref):
    a = _silu(a_ref[...])
    o_ref[0] = jnp.dot(a, w_ref[0], precision=HI, preferred_element_type=F32) + b_ref[0]


def _ada(c, c_ctx, w_ada, b_ada):
    depth, d, d6 = w_ada.shape
    b = c.shape[0]
    assert b + 1 <= 8
    a = jnp.concatenate([c, c_ctx[None], jnp.zeros((8 - b - 1, d), F32)], axis=0)
    tn = 1536
    return pl.pallas_call(
        _ada_kernel,
        grid=(depth, d6 // tn),
        in_specs=[pl.BlockSpec((8, d), lambda i, j: (0, 0)),
                  pl.BlockSpec((1, d, tn), lambda i, j: (i, 0, j)),
                  pl.BlockSpec((1, 1, tn), lambda i, j: (i, 0, j))],
        out_specs=pl.BlockSpec((1, 8, tn), lambda i, j: (i, 0, j)),
        out_shape=jax.ShapeDtypeStruct((depth, 8, d6), F32),
        compiler_params=_params("arbitrary", "arbitrary"),
        name="ada",
    )(a, w_ada, b_ada.reshape(depth, 1, d6))


def _row_spec(width, t0, col=0):
    return pl.BlockSpec((None, TS, width), lambda b, s: (b, s + t0, col))


def _mod_spec(d6, t0):
    return pl.BlockSpec((None, None, 1, d6), lambda b, s: (b, jnp.minimum(s + t0, 1), 0, 0))


def _full_spec(shape):
    nd = len(shape)
    return pl.BlockSpec(shape, lambda b, s: (0,) * nd)


def _qkv_kernel(x_ref, mod_ref, g_ref, w_ref, gqk_ref, cos_ref, sin_ref, eh_ref, eht_ref,
                qT_ref, k_ref, vT_ref):
    d = x_ref.shape[-1]
    nq = N_HEADS * HEAD_DIM
    nqk = nq + N_KV * HEAD_DIM
    h = _normmod(x_ref[...], g_ref[...], mod_ref[:, 0:d], mod_ref[:, d:2 * d]).astype(BF16)
    qkv = jnp.dot(h, w_ref[...], preferred_element_type=F32)
    qk = qkv[:, :nqk]
    ss = _dot_split_lhs(qk * qk, eh_ref[...])
    rinv = lax.rsqrt(ss * (1.0 / HEAD_DIM) + EPS)
    qk = qk * _dot_split_lhs(rinv, eht_ref[...]) * gqk_ref[...]
    cos = cos_ref[...]
    sin = sin_ref[...]
    lane = lax.broadcasted_iota(I32, (TS, LANES), 1)
    even = (lane & 1) == 0
    blocks = []
    for j in range(nqk // LANES):
        blk = qk[:, LANES * j:LANES * (j + 1)]
        partner = jnp.where(even, pltpu.roll(blk, LANES - 1, 1), pltpu.roll(blk, 1, 1))
        blocks.append(blk * cos + partner * sin)
    q = jnp.concatenate(blocks[:nq // LANES], axis=1)
    k = jnp.concatenate(blocks[nq // LANES:], axis=1)
    qT_ref[...] = q.T.astype(BF16)
    for kv in range(N_KV):
        k_ref[kv] = k[:, HEAD_DIM * kv:HEAD_DIM * (kv + 1)].astype(BF16)
    vT_ref[...] = qkv[:, nqk:].T.astype(BF16)


def _attn_ctx_kernel(qT_ref, k_ref, vT_ref, o_ref):
    for g in range(Q_PER_KV):
        rows = slice(HEAD_DIM * g, HEAD_DIM * (g + 1))
        s = jnp.dot(k_ref[...], qT_ref[rows, :], preferred_element_type=F32)
        p = jnp.exp2(s - jnp.max(s, axis=0, keepdims=True))
        l = jnp.sum(p, axis=0, keepdims=True)
        o = jnp.dot(vT_ref[...], p.astype(BF16), preferred_element_type=F32)
        o_ref[rows, :] = (o / l).astype(BF16)


def _attn_main_kernel(qT_ref, k_ref, vT_ref, o_ref, s0_ref, s1_ref, m0_ref, m1_ref, *, n_units, q_col0):
    for ref in (s0_ref, s1_ref, m0_ref, m1_ref):
        ref[...] = jnp.zeros(ref.shape, F32)

    def unit(i):
        i = jnp.clip(i, 0, n_units - 1)
        row = pl.multiple_of((i % Q_PER_KV) * HEAD_DIM, HEAD_DIM)
        col = pl.multiple_of((i // Q_PER_KV) * TS, TS)
        return row, col

    def step(i, s_w, m_w, s_r, m_r):
        row, col = unit(i)
        q = qT_ref[pl.ds(row, HEAD_DIM), pl.ds(q_col0 + col, TS)]
        s = jnp.dot(k_ref[...], q, preferred_element_type=F32)
        s_w[...] = s
        m_w[...] = jnp.max(s, axis=0, keepdims=True)
        row, col = unit(i - 1)
        p = jnp.exp2(s_r[...] - m_r[...])
        l = jnp.sum(p, axis=0, keepdims=True)
        o = jnp.dot(vT_ref[...], p.astype(BF16), preferred_element_type=F32)
        o_ref[pl.ds(row, HEAD_DIM), pl.ds(col, TS)] = (o / l).astype(BF16)

    def body(j, carry):
        step(2 * j, s0_ref, m0_ref, s1_ref, m1_ref)
        step(2 * j + 1, s1_ref, m1_ref, s0_ref, m0_ref)
        return carry

    lax.fori_loop(0, n_units // 2 + 1, body, 0)


def _oproj_kernel(oT_ref, w_ref, x_ref, mod_ref, out_ref):
    d = x_ref.shape[-1]
    o = oT_ref[...].astype(F32).T.astype(BF16)
    y = jnp.dot(o, w_ref[...], preferred_element_type=F32)
    out_ref[...] = x_ref[...] + mod_ref[:, 2 * d:3 * d] * y


def _rope_tables(n, m):
    rows = n // GRID_W
    row = jnp.broadcast_to(jnp.arange(rows)[:, None], (rows, GRID_W)).reshape(-1).astype(F32)
    col = jnp.broadcast_to(jnp.arange(GRID_W)[None, :], (rows, GRID_W)).reshape(-1).astype(F32)
    n_freq = HEAD_DIM // 4
    inv_freq = ROPE_THETA ** (-jnp.arange(n_freq, dtype=F32) / n_freq)
    ang = jnp.concatenate([row[:, None] * inv_freq, col[:, None] * inv_freq], axis=-1)
    cos = jnp.repeat(jnp.cos(ang), 2, axis=1)
    sign = jnp.tile(jnp.array([-1.0, 1.0], F32), HEAD_DIM // 2)
    sin = jnp.repeat(jnp.sin(ang), 2, axis=1) * sign
    cos = jnp.concatenate([jnp.ones((m, HEAD_DIM), F32), cos], axis=0)
    sin = jnp.concatenate([jnp.zeros((m, HEAD_DIM), F32), sin], axis=0)
    reps = LANES // HEAD_DIM
    return jnp.tile(cos, (1, reps)), jnp.tile(sin, (1, reps))


def _attention_layer(xs, modtab, norm_g, w_qkv, w_o, q_g, k_g, cos_t, sin_t, ctx_out):
    B, S, D = xs.shape
    nt = S // TS
    nq = N_HEADS * HEAD_DIM
    nkv = N_KV * HEAD_DIM
    nqk = nq + nkv
    d6 = modtab.shape[-1]
    q_scale = HEAD_DIM ** -0.5 * LOG2E
    gqk = jnp.concatenate([jnp.tile(q_g, N_HEADS) * q_scale, jnp.tile(k_g, N_KV)])[None]
    head_of = jnp.arange(nqk) // HEAD_DIM
    eh = (head_of[:, None] == jnp.arange(LANES)[None, :]).astype(BF16)
    qT, k4, vT = pl.pallas_call(
        _qkv_kernel,
        grid=(B, nt),
        in_specs=[_row_spec(D, 0), _mod_spec(d6, 0), _full_spec((1, D)), _full_spec((D, nqk + nkv)),
                  _full_spec((1, nqk)),
                  pl.BlockSpec((TS, LANES), lambda b, s: (s, 0)),
                  pl.BlockSpec((TS, LANES), lambda b, s: (s, 0)),
                  _full_spec((nqk, LANES)), _full_spec((LANES, nqk))],
        out_specs=[pl.BlockSpec((None, nq, TS), lambda b, s: (b, 0, s)),
                   pl.BlockSpec((None, N_KV, TS, HEAD_DIM), lambda b, s: (b, 0, s, 0)),
                   pl.BlockSpec((None, nkv, TS), lambda b, s: (b, 0, s))],
        out_shape=[jax.ShapeDtypeStruct((B, nq, S), BF16),
                   jax.ShapeDtypeStruct((B, N_KV, S, HEAD_DIM), BF16),
                   jax.ShapeDtypeStruct((B, nkv, S), BF16)],
        compiler_params=_params("arbitrary", "arbitrary"),
        name="attn_qkv",
    )(xs, modtab, norm_g[None], w_qkv.astype(BF16), gqk, cos_t, sin_t, eh, eh.T)


    gw = Q_PER_KV * HEAD_DIM
    n_lat = S - TS
    w_o = w_o.astype(BF16)
    if ctx_out:
        oT_ctx = pl.pallas_call(
            _attn_ctx_kernel,
            grid=(B, N_KV),
            in_specs=[pl.BlockSpec((None, gw, TS), lambda b, kv: (b, kv, 0)),
                      pl.BlockSpec((None, None, TS, HEAD_DIM), lambda b, kv: (b, kv, 0, 0)),
                      pl.BlockSpec((None, HEAD_DIM, TS), lambda b, kv: (b, kv, 0))],
            out_specs=pl.BlockSpec((None, gw, TS), lambda b, kv: (b, kv, 0)),
            out_shape=jax.ShapeDtypeStruct((B, nq, TS), BF16),
            compiler_params=_params("arbitrary", "arbitrary"),
            name="attn_ctx",
        )(qT, k4, vT)
        xs = pl.pallas_call(
            _oproj_kernel,
            grid=(B, 1),
            in_specs=[pl.BlockSpec((None, nq, TS), lambda b, s: (b, 0, 0)), _full_spec((nq, D)),
                      _row_spec(D, 0), _mod_spec(d6, 0)],
            out_specs=_row_spec(D, 0),
            out_shape=jax.ShapeDtypeStruct((B, S, D), F32),
            input_output_aliases={2: 0},
            compiler_params=_params("arbitrary", "arbitrary"),
            name="attn_oproj_ctx",
        )(oT_ctx, w_o, xs, modtab)

    n_units = (n_lat // TS) * Q_PER_KV
    assert n_units % 2 == 0
    oT = pl.pallas_call(
        functools.partial(_attn_main_kernel, n_units=n_units, q_col0=TS),
        grid=(B, N_KV),
        in_specs=[pl.BlockSpec((None, gw, S), lambda b, kv: (b, kv, 0)),
                  pl.BlockSpec((None, None, S, HEAD_DIM), lambda b, kv: (b, kv, 0, 0)),
                  pl.BlockSpec((None, HEAD_DIM, S), lambda b, kv: (b, kv, 0))],
        out_specs=pl.BlockSpec((None, gw, n_lat), lambda b, kv: (b, kv, 0)),
        out_shape=jax.ShapeDtypeStruct((B, nq, n_lat), BF16),
        scratch_shapes=[pltpu.VMEM((S, TS), F32), pltpu.VMEM((S, TS), F32),
                        pltpu.VMEM((1, TS), F32), pltpu.VMEM((1, TS), F32)],
        compiler_params=_params("arbitrary", "arbitrary"),
        name="attn_core",
    )(qT, k4, vT)

    return pl.pallas_call(
        _oproj_kernel,
        grid=(B, n_lat // TS),
        in_specs=[pl.BlockSpec((None, nq, TS), lambda b, s: (b, 0, s)), _full_spec((nq, D)),
                  _row_spec(D, 1), _mod_spec(d6, 1)],
        out_specs=_row_spec(D, 1),
        out_shape=jax.ShapeDtypeStruct((B, S, D), F32),
        input_output_aliases={2: 0},
        compiler_params=_params("arbitrary", "arbitrary"),
        name="attn_oproj",
    )(oT, w_o, xs, modtab)


def _normmod_kernel(x_ref, mod_ref, g_ref, h_ref):
    d = x_ref.shape[-1]
    h_ref[...] = _normmod(x_ref[...], g_ref[...], mod_ref[:, 0:d], mod_ref[:, d:2 * d])


def _pool_kernel(h_ref, x_ref, w_ref, ps_ref, gate_ref, out_ref, pad_ref, *, segs):
    gi = pl.program_id(1)
    zeros = jnp.zeros((POOL_PAD, h_ref.shape[-1]), F32)
    for widx, win in enumerate(POOL_WINDOWS):
        half = win // 2

        @pl.when(gi == widx)
        def _(half=half):
            for si, (r0, n) in enumerate(segs):
                pad_ref[0:POOL_PAD, :] = zeros
                pad_ref[POOL_PAD:POOL_PAD + n, :] = h_ref[r0:r0 + n, :]
                pad_ref[POOL_PAD + n:2 * POOL_PAD + n, :] = zeros
                gate = gate_ref[si]
                for c0 in range(0, n, ROW_CHUNK):
                    base = POOL_PAD + c0
                    acc = pad_ref[base - half:base - half + ROW_CHUNK, :]
                    for j in range(-half + 1, half):
                        acc = acc + pad_ref[base + j:base + j + ROW_CHUNK, :]
                    t = c0 + lax.broadcasted_iota(I32, (ROW_CHUNK, 1), 0)
                    cnt = jnp.minimum(t + half, n) - jnp.maximum(t - half, 0)
                    diff = acc / cnt.astype(F32) - pad_ref[base:base + ROW_CHUNK, :]
                    y = jnp.dot(diff.astype(BF16), w_ref[...], preferred_element_type=F32) * ps_ref[...]
                    rows = slice(r0 + c0, r0 + c0 + ROW_CHUNK)
                    out_ref[rows, :] = x_ref[rows, :] + gate * y


def _pool_layer(xs, modtab, norm_g, w_pool, pool_scale, m):
    B, S, D = xs.shape
    nt = S // TS
    d6 = modtab.shape[-1]
    pg = D // len(POOL_WINDOWS)
    h = pl.pallas_call(
        _normmod_kernel,
        grid=(B, nt),
        in_specs=[_row_spec(D, 0), _mod_spec(d6, 0), _full_spec((1, D))],
        out_specs=_row_spec(D, 0),
        out_shape=jax.ShapeDtypeStruct((B, S, D), F32),
        compiler_params=_params("arbitrary", "arbitrary"),
        name="pool_normmod",
    )(xs, modtab, norm_g[None])
    segs = ((0, m), (m, S - m))
    seq_spec = pl.BlockSpec((None, S, pg), lambda b, g: (b, 0, g))
    return pl.pallas_call(
        functools.partial(_pool_kernel, segs=segs),
        grid=(B, len(POOL_WINDOWS)),
        in_specs=[seq_spec, seq_spec,
                  pl.BlockSpec((None, pg, pg), lambda b, g: (g, 0, 0)),
                  pl.BlockSpec((1, pg), lambda b, g: (0, g)),
                  pl.BlockSpec((None, 2, 1, pg), lambda b, g: (b, 0, 0, 2 * (D // pg) + g))],
        out_specs=seq_spec,
        out_shape=jax.ShapeDtypeStruct((B, S, D), F32),
        scratch_shapes=[pltpu.VMEM((S - m + 2 * POOL_PAD, pg), F32)],
        compiler_params=_params("arbitrary", "arbitrary"),
        name="pool_mix",
    )(h, xs, w_pool.astype(BF16), pool_scale[None], modtab)


def _ssm_in_kernel(x_ref, mod_ref, g_ref, wz_ref, wx_ref, wdt_hi_ref, wdt_lo_ref, z_ref, xbc_ref, dt_ref):
    d = x_ref.shape[-1]
    hf = _normmod(x_ref[...], g_ref[...], mod_ref[:, 0:d], mod_ref[:, d:2 * d])
    h = hf.astype(BF16)
    z_ref[...] = jnp.dot(h, wz_ref[...], preferred_element_type=F32)
    xbc_ref[...] = jnp.dot(h, wx_ref[...], preferred_element_type=F32)
    dt_ref[...] = _dot_split(hf, wdt_hi_ref[...], wdt_lo_ref[...])


def _conv_kernel(u_ref, w_ref, b_ref, o_ref, pad_ref, *, segs):
    zeros = jnp.zeros((CONV_PAD, u_ref.shape[-1]), F32)
    for r0, n in segs:
        pad_ref[0:CONV_PAD, :] = zeros
        pad_ref[CONV_PAD:CONV_PAD + n, :] = u_ref[r0:r0 + n, :]
        pad_ref[CONV_PAD + n:2 * CONV_PAD + n, :] = zeros
        for c0 in range(0, n, ROW_CHUNK):
            acc = b_ref[...]
            for k in range(SSM_CONV):
                lo = CONV_PAD + c0 - SSM_CONV // 2 + k
                acc = acc + w_ref[k:k + 1, :] * pad_ref[lo:lo + ROW_CHUNK, :]
            o_ref[r0 + c0:r0 + c0 + ROW_CHUNK, :] = _silu(acc)


def _ssd_kernel(xbc_ref, dt_ref, dtb_ref, a_ref, tri_ref, sel_ref, y_ref, h_ref, *, direction):
    q = SSM_CHUNK
    lane0 = SSM_HEADS * direction

    @pl.when(pl.program_id(1) == 0)
    def _():
        h_ref[...] = jnp.zeros(h_ref.shape, F32)

    raw = dt_ref[...] + dtb_ref[...]
    dt = jnp.maximum(raw, 0.0) + jnp.log1p(jnp.exp(-jnp.abs(raw)))
    la = dt * a_ref[...]
    cs = jnp.dot(tri_ref[...], la, precision=HI, preferred_element_type=F32)
    tot = cs[q - 1:q, :]
    if direction == 0:
        u = cs
        dec_in = jnp.exp(u)
        dec_end = jnp.exp(tot - u)
    else:
        u = la - cs
        dec_in = jnp.exp(tot + u)
        dec_end = jnp.exp(-u)
    uT = u.T
    dtT = dt.T
    stacked = jnp.concatenate([dec_in, dt * dec_end, jnp.broadcast_to(jnp.exp(tot), (8, LANES))], axis=0)
    spread = _dot_split_lhs(stacked, sel_ref[...])
    din_col = spread[0:q]
    dtot_col = spread[2 * q:2 * q + 1]
    x = xbc_ref[:, 0:SSM_D_INNER]
    x_b = x.astype(BF16)
    xd_b = (x * spread[q:2 * q]).astype(BF16)
    row = lax.broadcasted_iota(I32, (q, q), 0)
    col = lax.broadcasted_iota(I32, (q, q), 1)
    mask = (row >= col) if direction == 0 else (col >= row)
    gw = SSM_HPG * SSM_HEAD_DIM
    for g in range(SSM_GROUPS):
        b_lo = SSM_D_INNER + SSM_STATE * g
        c_lo = SSM_D_INNER + SSM_GN + SSM_STATE * g
        bT = xbc_ref[:, b_lo:b_lo + SSM_STATE].T.astype(BF16)
        cg = xbc_ref[:, c_lo:c_lo + SSM_STATE].astype(BF16)
        scores = jnp.dot(cg, bT, preferred_element_type=F32)
        hg = h_ref[g]
        cols = slice(gw * g, gw * (g + 1))
        y_off = jnp.dot(cg, hg.astype(BF16), preferred_element_type=F32) * din_col[:, cols]
        for r in range(SSM_HPG):
            hh = SSM_HPG * g + r
            hl = lane0 + hh
            seg = u[:, hl:hl + 1] - uT[hl:hl + 1, :]
            decay = jnp.exp(jnp.where(mask, seg, NEG_INF))
            hc = slice(SSM_HEAD_DIM * hh, SSM_HEAD_DIM * (hh + 1))
            mix = (scores * decay * dtT[hl:hl + 1, :]).astype(BF16)
            y_d = jnp.dot(mix, x_b[:, hc], preferred_element_type=F32)
            y_ref[:, hc] = y_d + y_off[:, SSM_HEAD_DIM * r:SSM_HEAD_DIM * (r + 1)]
        h_ref[g] = hg * dtot_col[:, cols] + jnp.dot(bT, xd_b[:, cols], preferred_element_type=F32)


def _ssm_out_kernel(y0_ref, y1_ref, xh_ref, z_ref, dsk_ref, ng_ref, w_ref, x_ref, mod_ref, out_ref):
    d = x_ref.shape[-1]
    y = y0_ref[...] + y1_ref[...] + dsk_ref[...] * xh_ref[...]
    g = y * _silu(z_ref[...])
    gsz = SSM_D_INNER // SSM_GROUPS
    parts = []
    for k in range(SSM_GROUPS):
        gk = g[:, gsz * k:gsz * (k + 1)]
        parts.append(gk * lax.rsqrt(jnp.mean(gk * gk, axis=-1, keepdims=True) + EPS))
    gn = (jnp.concatenate(parts, axis=1) * ng_ref[...]).astype(BF16)
    o = jnp.dot(gn, w_ref[...], preferred_element_type=F32)
    out_ref[...] = x_ref[...] + mod_ref[:, 2 * d:3 * d] * o


def _ssd_layer(xs, modtab, norm_g, w_in, conv_w, conv_b, dt_bias, a_log, d_skip, ssm_norm_g, w_out, m):
    B, S, D = xs.shape
    nt = S // TS
    d6 = modtab.shape[-1]
    di = SSM_D_INNER
    cd = SSM_CONV_DIM
    wz = w_in[:, :di].astype(BF16)
    wx = w_in[:, di:di + cd].astype(BF16)
    wdt_hi, wdt_lo = _split(jnp.pad(w_in[:, di + cd:], ((0, 0), (0, LANES - 2 * SSM_HEADS))))
    z, xbc_raw, dt_raw = pl.pallas_call(
        _ssm_in_kernel,
        grid=(B, nt),
        in_specs=[_row_spec(D, 0), _mod_spec(d6, 0), _full_spec((1, D)), _full_spec((D, di)),
                  _full_spec((D, cd)), _full_spec((D, LANES)), _full_spec((D, LANES))],
        out_specs=[_row_spec(di, 0), _row_spec(cd, 0), _row_spec(LANES, 0)],
        out_shape=[jax.ShapeDtypeStruct((B, S, di), F32), jax.ShapeDtypeStruct((B, S, cd), F32),
                   jax.ShapeDtypeStruct((B, S, LANES), F32)],
        compiler_params=_params("arbitrary", "arbitrary"),
        name="ssm_in",
    )(xs, modtab, norm_g[None], wz, wx, wdt_hi, wdt_lo)

    segs = ((0, m), (m, S - m))
    cw = 256
    seq_spec = pl.BlockSpec((None, S, cw), lambda b, j: (b, 0, j))
    xbc = pl.pallas_call(
        functools.partial(_conv_kernel, segs=segs),
        grid=(B, cd // cw),
        in_specs=[seq_spec, pl.BlockSpec((SSM_CONV, cw), lambda b, j: (0, j)),
                  pl.BlockSpec((1, cw), lambda b, j: (0, j))],
        out_specs=seq_spec,
        out_shape=jax.ShapeDtypeStruct((B, S, cd), F32),
        scratch_shapes=[pltpu.VMEM((S - m + 2 * CONV_PAD, cw), F32)],
        compiler_params=_params("arbitrary", "arbitrary"),
        name="ssm_conv",
    )(xbc_raw, conv_w, conv_b[None])

    q = SSM_CHUNK
    nc = S // q
    mc = m // q
    pad = LANES - 2 * SSM_HEADS
    dtb = jnp.pad(dt_bias.reshape(-1), (0, pad))[None]
    a_neg = jnp.pad(-jnp.exp(a_log.reshape(-1)), (0, pad))[None]
    tri = (jnp.arange(q)[:, None] >= jnp.arange(q)[None, :]).astype(F32)
    ys = []
    for direction in range(2):
        if direction == 0:
            cmap = lambda c: c
        else:
            cmap = lambda c: jnp.where(c < mc, mc - 1 - c, nc - 1 - (c - mc))
        head_lane = SSM_HEADS * direction + jnp.arange(di) // SSM_HEAD_DIM
        sel = (jnp.arange(LANES)[:, None] == head_lane[None, :]).astype(BF16)
        ys.append(pl.pallas_call(
            functools.partial(_ssd_kernel, direction=direction),
            grid=(B, nc),
            in_specs=[pl.BlockSpec((None, q, cd), lambda b, c, cmap=cmap: (b, cmap(c), 0)),
                      pl.BlockSpec((None, q, LANES), lambda b, c, cmap=cmap: (b, cmap(c), 0)),
                      _full_spec((1, LANES)), _full_spec((1, LANES)), _full_spec((q, q)),
                      _full_spec((LANES, di))],
            out_specs=pl.BlockSpec((None, q, di), lambda b, c, cmap=cmap: (b, cmap(c), 0)),
            out_shape=jax.ShapeDtypeStruct((B, S, di), F32),
            scratch_shapes=[pltpu.VMEM((SSM_GROUPS, SSM_STATE, SSM_HPG * SSM_HEAD_DIM), F32)],
            compiler_params=_params("arbitrary", "arbitrary"),
            name=f"ssd_scan{direction}",
        )(xbc, dt_raw, dtb, a_neg, tri, sel))

    dsk = jnp.repeat(d_skip[0] + d_skip[1], SSM_HEAD_DIM)[None]
    return pl.pallas_call(
        _ssm_out_kernel,
        grid=(B, nt),
        in_specs=[_row_spec(di, 0), _row_spec(di, 0), _row_spec(di, 0), _row_spec(di, 0),
                  _full_spec((1, di)), _full_spec((1, di)), _full_spec((di, D)),
                  _row_spec(D, 0), _mod_spec(d6, 0)],
        out_specs=_row_spec(D, 0),
        out_shape=jax.ShapeDtypeStruct((B, S, D), F32),
        input_output_aliases={7: 0},
        compiler_params=_params("arbitrary", "arbitrary"),
        name="ssm_out",
    )(ys[0], ys[1], xbc, z, dsk, ssm_norm_g[None], w_out.astype(BF16), xs, modtab)


def _store_slabs(ref, v):
    rows = v.shape[0]
    for s in range(SUBLANES):
        ref[pl.ds(s, rows, stride=SUBLANES), :] = v[:, LANES * s:LANES * (s + 1)]


def _load_slab_chunk(ref, rows, s, row0=0):
    return ref[pl.ds(SUBLANES * row0 + s, rows, stride=SUBLANES), :]


def _slab(ref, row):
    return ref.at[pl.ds(pl.multiple_of(row * SUBLANES, SUBLANES), SUBLANES)]


def _route(logits):
    lane = lax.broadcasted_iota(I32, logits.shape, 1).astype(F32)
    big = float(LANES)
    gl = jnp.where(lane < MOE_GROUPS, logits, NEG_INF)
    gmax = jnp.max(gl, axis=1, keepdims=True)
    gate = 1.0 / jnp.sum(jnp.exp(gl - gmax), axis=1, keepdims=True)
    grp = jnp.min(jnp.where(gl == gmax, lane, big), axis=1, keepdims=True)
    lo = MOE_GROUPS + MOE_EPG * grp
    el = jnp.where((lane >= lo) & (lane < lo + MOE_EPG), logits, NEG_INF)
    v1 = jnp.max(el, axis=1, keepdims=True)
    i1 = jnp.min(jnp.where(el == v1, lane, big), axis=1, keepdims=True)
    el2 = jnp.where(lane == i1, NEG_INF, el)
    v2 = jnp.max(el2, axis=1, keepdims=True)
    i2 = jnp.min(jnp.where(el2 == v2, lane, big), axis=1, keepdims=True)
    e2 = jnp.exp(v2 - v1)
    den = 1.0 + e2
    return i1, i2, gate * (1.0 / den), gate * (e2 / den)


def _router_kernel(x_ref, mod_ref, g_ref, wr_hi_ref, wr_lo_ref, ltri_ref, tok_ref, meta_ref, wt_ref, cnt_ref,
                   run_ref):
    d = x_ref.shape[-1]

    @pl.when((pl.program_id(0) == 0) & (pl.program_id(1) == 0))
    def _():
        run_ref[...] = jnp.zeros(run_ref.shape, F32)

    t = _normmod(x_ref[...], g_ref[...], mod_ref[:, 3 * d:4 * d], mod_ref[:, 4 * d:5 * d])
    _store_slabs(tok_ref, t)
    logits = _dot_split(t, wr_hi_ref[...], wr_lo_ref[...])
    i1, i2, w1, w2 = _route(logits)
    lane = lax.broadcasted_iota(I32, logits.shape, 1).astype(F32)
    wt_ref[...] = jnp.where(lane == 0.0, w1, jnp.where(lane == 1.0, w2, 0.0))
    oh1 = (lane == i1).astype(F32)
    oh2 = (lane == i2).astype(F32)
    oh = oh1 + oh2
    before = jnp.dot(ltri_ref[...], oh.astype(BF16), preferred_element_type=F32) + run_ref[...]
    r1 = jnp.sum(oh1 * before, axis=1, keepdims=True)
    r2 = jnp.sum(oh2 * before, axis=1, keepdims=True)
    run = run_ref[...] + jnp.sum(oh, axis=0, keepdims=True)
    run_ref[...] = run
    cnt_ref[...] = jnp.broadcast_to(run, cnt_ref.shape)
    cols = jnp.where(lane == 0.0, i1 - MOE_GROUPS,
                     jnp.where(lane == 1.0, i2 - MOE_GROUPS,
                               jnp.where(lane == 2.0, r1, jnp.where(lane == 3.0, r2, 0.0))))
    meta_ref[...] = cols.T[0:8, :].astype(I32)


def _row_copy(src_ref, src_row, dst_ref, dst_row, sem):
    return pltpu.make_async_copy(_slab(src_ref, src_row), _slab(dst_ref, dst_row), sem)


def _dispatch_kernel(dest_ref, tok_ref, init_hbm, xs_hbm, sem):
    del init_hbm

    for h in range(DISPATCH_TILES):
        def body(j, carry, h=h):
            _row_copy(tok_ref, h * TS + j, xs_hbm, dest_ref[h, 0, j], sem).start()
            _row_copy(tok_ref, h * TS + j, xs_hbm, dest_ref[h, 1, j], sem).start()
            return carry

        lax.fori_loop(0, TS, body, 0, unroll=DMA_UNROLL)
    for _ in range(2):
        pltpu.make_async_copy(tok_ref, xs_hbm.at[pl.ds(0, tok_ref.shape[0])], sem).wait()


def _expert_kernel(be_ref, nu_ref, xs_ref, wgu_ref, wd_ref, y_ref, wgu_b_ref, wd_b_ref):
    i = pl.program_id(0)
    rows = xs_ref.shape[0] // SUBLANES

    @pl.when((i == 0) | (be_ref[i] != be_ref[jnp.maximum(i - 1, 0)]))
    def _():
        wgu_b_ref[...] = wgu_ref[...].astype(BF16)
        wd_b_ref[...] = wd_ref[...].astype(BF16)

    @pl.when(i < nu_ref[0])
    def _():
        x = jnp.concatenate([_load_slab_chunk(xs_ref, rows, s) for s in range(SUBLANES)], axis=1)
        gu = jnp.dot(x.astype(BF16), wgu_b_ref[...], preferred_element_type=F32)
        hid = _silu(gu[:, :MOE_HIDDEN]) * gu[:, MOE_HIDDEN:]
        _store_slabs(y_ref, jnp.dot(hid.astype(BF16), wd_b_ref[...], preferred_element_type=F32))

    @pl.when(i >= nu_ref[0])
    def _():
        y_ref[...] = jnp.zeros(y_ref.shape, F32)


def _combine_kernel(dest_ref, next_dest_ref, x_ref, mod_ref, wt_ref, y_hbm, *rest, final_norm):
    if final_norm:
        fg_ref, out_ref, buf_ref, sem = rest
    else:
        out_ref, buf_ref, sem = rest
    d = x_ref.shape[-1]
    step = pl.program_id(0) * pl.num_programs(1) + pl.program_id(1)
    n_steps = pl.num_programs(0) * pl.num_programs(1)
    slot = step % 2
    slot_rows = 2 * TS

    def gather(d_ref, to_slot):
        base = to_slot * slot_rows

        def body(j, carry):
            _row_copy(y_hbm, d_ref[0, j], buf_ref, base + j, sem.at[to_slot]).start()
            _row_copy(y_hbm, d_ref[1, j], buf_ref, base + TS + j, sem.at[to_slot]).start()
            return carry

        lax.fori_loop(0, TS, body, 0, unroll=DMA_UNROLL)

    @pl.when(step == 0)
    def _():
        gather(dest_ref, 0)

    @pl.when(step + 1 < n_steps)
    def _():
        gather(next_dest_ref, 1 - slot)

    row0 = slot * slot_rows
    slot_view = buf_ref.at[pl.ds(pl.multiple_of(row0 * SUBLANES, SUBLANES), slot_rows * SUBLANES)]
    pltpu.make_async_copy(y_hbm.at[pl.ds(0, slot_rows * SUBLANES)], slot_view, sem.at[slot]).wait()
    w1 = wt_ref[:, 0:1]
    w2 = wt_ref[:, 1:2]
    for s in range(SUBLANES):
        cols = slice(LANES * s, LANES * (s + 1))
        f = (w1 * _load_slab_chunk(buf_ref, TS, s, row0=row0)
             + w2 * _load_slab_chunk(buf_ref, TS, s, row0=row0 + TS))
        out_ref[:, cols] = x_ref[:, cols] + mod_ref[:, 5 * d + LANES * s:5 * d + LANES * (s + 1)] * f
    if final_norm:
        v = out_ref[...]
        out_ref[...] = v * lax.rsqrt(jnp.mean(v * v, axis=-1, keepdims=True) + EPS) * fg_ref[...]


def _moe_layer(xs, modtab, norm_g, w_rg, w_re, w_gu, w_down, layer, ctx_out, final_g=None):
    assert final_g is None or not ctx_out
    B, S, D = xs.shape
    assert D == SUBLANES * LANES
    d6 = modtab.shape[-1]
    t0 = 0 if ctx_out else 1
    nt = S // TS - t0
    R = nt * TS
    wr_hi, wr_lo = _split(
        jnp.pad(jnp.concatenate([w_rg, w_re], axis=1), ((0, 0), (0, LANES - MOE_GROUPS - MOE_EXPERTS))))
    out_row = lambda width: pl.BlockSpec((None, TS, width), lambda b, s: (b, s, 0))
    ltri = (jnp.arange(TS)[:, None] > jnp.arange(TS)[None, :]).astype(BF16)
    tok, meta, wt, cnt = pl.pallas_call(
        _router_kernel,
        grid=(B, nt),
        in_specs=[_row_spec(D, t0), _mod_spec(d6, t0), _full_spec((1, D)), _full_spec((D, LANES)),
                  _full_spec((D, LANES)), _full_spec((TS, TS))],
        out_specs=[pl.BlockSpec((TS * SUBLANES, LANES), lambda b, s: (b * nt + s, 0)),
                   pl.BlockSpec((None, 8, TS), lambda b, s: (b * nt + s, 0, 0)), out_row(LANES),
                   _full_spec((8, LANES))],
        out_shape=[jax.ShapeDtypeStruct((B * R * SUBLANES, LANES), F32), jax.ShapeDtypeStruct((B * nt, 8, TS), I32),
                   jax.ShapeDtypeStruct((B, R, LANES), F32), jax.ShapeDtypeStruct((8, LANES), F32)],
        scratch_shapes=[pltpu.VMEM((1, LANES), F32)],
        compiler_params=_params("arbitrary", "arbitrary"),
        name="moe_router",
    )(xs, modtab, norm_g[None], wr_hi, wr_lo, ltri)

    T = B * R
    n_blocks = -(-2 * T // MOE_BLOCK) + MOE_EXPERTS
    n_rows = n_blocks * MOE_BLOCK
    counts = cnt[0, MOE_GROUPS:MOE_GROUPS + MOE_EXPERTS].astype(I32)
    padded = (counts + MOE_BLOCK - 1) // MOE_BLOCK * MOE_BLOCK
    pends = jnp.cumsum(padded)
    pstarts = pends - padded
    blk_start = jnp.arange(n_blocks, dtype=I32) * MOE_BLOCK
    block_expert = jnp.minimum(jnp.sum(pends[None, :] <= blk_start[:, None], axis=1),
                               MOE_EXPERTS - 1).astype(I32)
    n_used = (pends[-1:] // MOE_BLOCK).astype(I32)

    hbm = pl.BlockSpec(memory_space=pltpu.MemorySpace.HBM)
    expert_start = jnp.sum(jnp.where(meta[:, 0:2, :, None] == jnp.arange(MOE_EXPERTS), pstarts, 0), axis=-1)
    dest = expert_start + meta[:, 2:4, :]
    assert (B * nt) % DISPATCH_TILES == 0
    x_sorted = pl.pallas_call(
        _dispatch_kernel,
        grid=(B * nt // DISPATCH_TILES,),
        in_specs=[pl.BlockSpec((DISPATCH_TILES, 2, TS), lambda i: (i, 0, 0), memory_space=pltpu.SMEM),
                  pl.BlockSpec((DISPATCH_TILES * TS * SUBLANES, LANES), lambda i: (i, 0)), hbm],
        out_specs=hbm,
        out_shape=jax.ShapeDtypeStruct((n_rows * SUBLANES, LANES), F32),
        scratch_shapes=[pltpu.SemaphoreType.DMA(())],
        input_output_aliases={2: 0},
        compiler_params=_params("arbitrary"),
        name="moe_dispatch",
    )(dest, tok, jnp.zeros((n_rows * SUBLANES, LANES), F32))

    blk_spec = pl.BlockSpec((MOE_BLOCK * SUBLANES, LANES), lambda i, be, nu: (i, 0))
    y_sorted = pl.pallas_call(
        _expert_kernel,
        grid_spec=pltpu.PrefetchScalarGridSpec(
            num_scalar_prefetch=2,
            grid=(n_blocks,),
            in_specs=[blk_spec,
                      pl.BlockSpec((None, None, D, 2 * MOE_HIDDEN), lambda i, be, nu: (layer, be[i], 0, 0)),
                      pl.BlockSpec((None, None, MOE_HIDDEN, D), lambda i, be, nu: (layer, be[i], 0, 0))],
            out_specs=blk_spec,
            scratch_shapes=[pltpu.VMEM((D, 2 * MOE_HIDDEN), BF16), pltpu.VMEM((MOE_HIDDEN, D), BF16)],
        ),
        out_shape=jax.ShapeDtypeStruct((n_rows * SUBLANES, LANES), F32),
        compiler_params=_params("arbitrary"),
        name="moe_experts",
    )(block_expert, n_used, x_sorted, w_gu, w_down)

    in_specs = [pl.BlockSpec((None, 2, TS), lambda b, s: (b * nt + s, 0, 0), memory_space=pltpu.SMEM),
                pl.BlockSpec((None, 2, TS), lambda b, s: (jnp.minimum(b * nt + s + 1, B * nt - 1), 0, 0),
                             memory_space=pltpu.SMEM),
                _row_spec(D, t0), _mod_spec(d6, t0), out_row(LANES), hbm]
    operands = [dest, dest, xs, modtab, wt, y_sorted]
    if final_g is None:
        out_spec, out_shape, aliases = _row_spec(D, t0), jax.ShapeDtypeStruct((B, S, D), F32), {2: 0}
    else:
        in_specs.append(_full_spec((1, D)))
        operands.append(final_g[None])
        out_spec, out_shape, aliases = out_row(D), jax.ShapeDtypeStruct((B, R, D), F32), {}
    return pl.pallas_call(
        functools.partial(_combine_kernel, final_norm=final_g is not None),
        grid=(B, nt),
        in_specs=in_specs,
        out_specs=out_spec,
        out_shape=out_shape,
        scratch_shapes=[pltpu.VMEM((2 * 2 * TS * SUBLANES, LANES), F32), pltpu.SemaphoreType.DMA((2,))],
        input_output_aliases=aliases,
        compiler_params=_params("arbitrary", "arbitrary"),
        name="moe_combine",
    )(*operands)


def kernel(x, c, ctx, c_ctx, w_ada, b_ada, norm_mix_g, norm_ffn_g, final_norm_g, attn_w_qkv, attn_w_o, attn_q_norm_g, attn_k_norm_g, pool_w, pool_scale, ssm_w_in, ssm_conv_w, ssm_conv_b, ssm_dt_bias, ssm_a_log, ssm_d, ssm_norm_g, ssm_w_out, moe_w_router_group, moe_w_router_expert, moe_w_gate_up, moe_w_down):
    B, n, D = x.shape
    m = ctx.shape[1]
    depth = w_ada.shape[0]
    assert m == TS and n % TS == 0 and n % GRID_W == 0
    xs = jnp.concatenate([ctx, x], axis=1)
    mods = _ada(c, c_ctx, w_ada, b_ada)
    cos_t, sin_t = _rope_tables(n, m)
    for i in range(depth):
        kind, j = i % N_MIXERS, i // N_MIXERS
        ctx_out = i < depth - 1
        modtab = jnp.stack([jnp.broadcast_to(mods[i, B], (B, 6 * D)), mods[i, :B]], axis=1)[:, :, None, :]
        if kind == 0:
            xs = _attention_layer(xs, modtab, norm_mix_g[i], attn_w_qkv[j], attn_w_o[j], attn_q_norm_g[j],
                                  attn_k_norm_g[j], cos_t, sin_t, ctx_out)
        elif kind == 1:
            assert ctx_out
            xs = _pool_layer(xs, modtab, norm_mix_g[i], pool_w[j], pool_scale[j], m)
        else:
            assert ctx_out
            xs = _ssd_layer(xs, modtab, norm_mix_g[i], ssm_w_in[j], ssm_conv_w[j], ssm_conv_b[j],
                            ssm_dt_bias[j], ssm_a_log[j], ssm_d[j], ssm_norm_g[j], ssm_w_out[j], m)
        xs = _moe_layer(xs, modtab, norm_ffn_g[i], moe_w_router_group[i], moe_w_router_expert[i],
                        moe_w_gate_up, moe_w_down, i, ctx_out,
                        final_g=final_norm_g if i == depth - 1 else None)
    return xs
```

```python
import functools

import jax
import jax.numpy as jnp
from jax import lax
from jax.experimental import pallas as pl
from jax.experimental.pallas import tpu as pltpu

F32 = jnp.float32
BF16 = jnp.bfloat16
I32 = jnp.int32
HI = lax.Precision.HIGHEST
EPS = 1e-6
NEG_INF = float("-inf")
LOG2E = 1.4426950408889634

TS = 256
LANES = 128
SUBLANES = 8
GRID_W = 64
ROPE_THETA = 10000.0
N_MIXERS = 3

N_HEADS = 16
N_KV = 4
HEAD_DIM = 64
Q_PER_KV = N_HEADS // N_KV

POOL_WINDOWS = (2, 4, 8, 16)
POOL_PAD = 16
ROW_CHUNK = 256

SSM_HEADS = 32
SSM_HEAD_DIM = 64
SSM_GROUPS = 4
SSM_HPG = SSM_HEADS // SSM_GROUPS
SSM_STATE = 128
SSM_CONV = 4
SSM_CHUNK = 128
SSM_D_INNER = SSM_HEADS * SSM_HEAD_DIM
SSM_GN = SSM_GROUPS * SSM_STATE
SSM_CONV_DIM = SSM_D_INNER + 2 * SSM_GN
CONV_PAD = 8

MOE_GROUPS = 4
MOE_EPG = 8
MOE_EXPERTS = MOE_GROUPS * MOE_EPG
MOE_HIDDEN = 512
MOE_BLOCK = 256
DMA_UNROLL = 8
DISPATCH_TILES = 4


def _params(*sem):
    return pltpu.CompilerParams(dimension_semantics=sem)


def _silu(v):
    return v / (1.0 + jnp.exp(-v))


def _split(v):
    hi = v.astype(BF16)
    return hi, (v - hi.astype(F32)).astype(BF16)


def _dot_split_lhs(a, e):
    hi, lo = _split(a)
    return jnp.dot(hi, e, preferred_element_type=F32) + jnp.dot(lo, e, preferred_element_type=F32)


def _dot_split(a, b_hi, b_lo):
    hi, lo = _split(a)
    return jnp.dot(hi, b_hi, preferred_element_type=F32) + (
        jnp.dot(hi, b_lo, preferred_element_type=F32) + jnp.dot(lo, b_hi, preferred_element_type=F32))


def _normmod(x, g, shift, scale):
    ms = jnp.mean(x * x, axis=-1, keepdims=True)
    return (x * lax.rsqrt(ms + EPS) * g) * (1.0 + scale) + shift


def _ada_kernel(a_ref, w_ref, b_ref, o_ref):
    a = _silu(a_ref[...])
    o_ref[0] = jnp.dot(a, w_ref[0], precision=HI, preferred_element_type=F32) + b_ref[0]


def _ada(c, c_ctx, w_ada, b_ada):
    depth, d, d6 = w_ada.shape
    b = c.shape[0]
    assert b + 1 <= 8
    a = jnp.concatenate([c, c_ctx[None], jnp.zeros((8 - b - 1, d), F32)], axis=0)
    tn = 1536
    return pl.pallas_call(
        _ada_kernel,
        grid=(depth, d6 // tn),
        in_specs=[pl.BlockSpec((8, d), lambda i, j: (0, 0)),
                  pl.BlockSpec((1, d, tn), lambda i, j: (i, 0, j)),
                  pl.BlockSpec((1, 1, tn), lambda i, j: (i, 0, j))],
        out_specs=pl.BlockSpec((1, 8, tn), lambda i, j: (i, 0, j)),
        out_shape=jax.ShapeDtypeStruct((depth, 8, d6), F32),
        compiler_params=_params("arbitrary", "arbitrary"),
        name="ada",
    )(a, w_ada, b_ada.reshape(depth, 1, d6))


def _row_spec(width, t0, col=0):
    return pl.BlockSpec((None, TS, width), lambda b, s: (b, s + t0, col))


def _mod_spec(d6, t0):
    return pl.BlockSpec((None, None, 1, d6), lambda b, s: (b, jnp.minimum(s + t0, 1), 0, 0))


def _full_spec(shape):
    nd = len(shape)
    return pl.BlockSpec(shape, lambda b, s: (0,) * nd)


def _qkv_kernel(x_ref, mod_ref, g_ref, wT_ref, gqk_ref, cos_ref, sin_ref, qT_ref, k_ref, vT_ref):
    d = x_ref.shape[-1]
    nq = N_HEADS * HEAD_DIM
    nqk = nq + N_KV * HEAD_DIM
    h = _normmod(x_ref[...], g_ref[...], mod_ref[:, 0:d], mod_ref[:, d:2 * d])
    t = jnp.dot(wT_ref[...], h.T.astype(BF16), preferred_element_type=F32)
    cos = cos_ref[...]
    sin = sin_ref[...]
    row = lax.broadcasted_iota(I32, (HEAD_DIM, TS), 0)
    even = (row & 1) == 0
    for hh in range(nqk // HEAD_DIM):
        rows = slice(HEAD_DIM * hh, HEAD_DIM * (hh + 1))
        blk = t[rows, :]
        rinv = lax.rsqrt(jnp.mean(blk * blk, axis=0, keepdims=True) + EPS)
        y = blk * rinv * gqk_ref[rows, :]
        partner = jnp.where(even, pltpu.roll(y, HEAD_DIM - 1, 0), pltpu.roll(y, 1, 0))
        out = y * cos + partner * sin
        if hh < N_HEADS:
            qT_ref[rows, :] = out.astype(BF16)
        else:
            k_ref[hh - N_HEADS] = out.T.astype(BF16)
    vT_ref[...] = t[nqk:, :].astype(BF16)


def _attn_ctx_kernel(qT_ref, k_ref, vT_ref, o_ref):
    for g in range(Q_PER_KV):
        rows = slice(HEAD_DIM * g, HEAD_DIM * (g + 1))
        s = jnp.dot(k_ref[...], qT_ref[rows, :], preferred_element_type=F32)
        p = jnp.exp2(s - jnp.max(s, axis=0, keepdims=True))
        l = jnp.sum(p, axis=0, keepdims=True)
        o = jnp.dot(vT_ref[...], p.astype(BF16), preferred_element_type=F32)
        o_ref[rows, :] = (o / l).astype(BF16)


def _attn_main_kernel(qT_ref, k_ref, vT_ref, o_ref, s0_ref, s1_ref, m0_ref, m1_ref, *, n_units, q_col0):
    for ref in (s0_ref, s1_ref, m0_ref, m1_ref):
        ref[...] = jnp.zeros(ref.shape, F32)

    def unit(i):
        i = jnp.clip(i, 0, n_units - 1)
        row = pl.multiple_of((i % Q_PER_KV) * HEAD_DIM, HEAD_DIM)
        col = pl.multiple_of((i // Q_PER_KV) * TS, TS)
        return row, col

    def step(i, s_w, m_w, s_r, m_r):
        row, col = unit(i)
        q = qT_ref[pl.ds(row, HEAD_DIM), pl.ds(q_col0 + col, TS)]
        s = jnp.dot(k_ref[...], q, preferred_element_type=F32)
        s_w[...] = s
        m_w[...] = jnp.max(s, axis=0, keepdims=True)
        row, col = unit(i - 1)
        p = jnp.exp2(s_r[...] - m_r[...])
        l = jnp.sum(p, axis=0, keepdims=True)
        o = jnp.dot(vT_ref[...], p.astype(BF16), preferred_element_type=F32)
        o_ref[pl.ds(row, HEAD_DIM), pl.ds(col, TS)] = (o / l).astype(BF16)

    def body(j, carry):
        step(2 * j, s0_ref, m0_ref, s1_ref, m1_ref)
        step(2 * j + 1, s1_ref, m1_ref, s0_ref, m0_ref)
        return carry

    lax.fori_loop(0, n_units // 2 + 1, body, 0)


def _oproj_kernel(oT_ref, w_ref, x_ref, mod_ref, out_ref):
    d = x_ref.shape[-1]
    o = oT_ref[...].astype(F32).T.astype(BF16)
    y = jnp.dot(o, w_ref[...], preferred_element_type=F32)
    out_ref[...] = x_ref[...] + mod_ref[:, 2 * d:3 * d] * y


def _rope_tables(n, m):
    rows = n // GRID_W
    row = jnp.broadcast_to(jnp.arange(rows)[:, None], (rows, GRID_W)).reshape(-1).astype(F32)
    col = jnp.broadcast_to(jnp.arange(GRID_W)[None, :], (rows, GRID_W)).reshape(-1).astype(F32)
    n_freq = HEAD_DIM // 4
    inv_freq = ROPE_THETA ** (-jnp.arange(n_freq, dtype=F32) / n_freq)
    ang = jnp.concatenate([row[:, None] * inv_freq, col[:, None] * inv_freq], axis=-1)
    cos = jnp.repeat(jnp.cos(ang), 2, axis=1)
    sign = jnp.tile(jnp.array([-1.0, 1.0], F32), HEAD_DIM // 2)
    sin = jnp.repeat(jnp.sin(ang), 2, axis=1) * sign
    cos = jnp.concatenate([jnp.ones((m, HEAD_DIM), F32), cos], axis=0)
    sin = jnp.concatenate([jnp.zeros((m, HEAD_DIM), F32), sin], axis=0)
    return cos.T, sin.T


def _attention_layer(xs, modtab, norm_g, w_qkv, w_o, q_g, k_g, cos_t, sin_t, ctx_out):
    B, S, D = xs.shape
    nt = S // TS
    nq = N_HEADS * HEAD_DIM
    nkv = N_KV * HEAD_DIM
    nqk = nq + nkv
    d6 = modtab.shape[-1]
    q_scale = HEAD_DIM ** -0.5 * LOG2E
    gqk = jnp.concatenate([jnp.tile(q_g, N_HEADS) * q_scale, jnp.tile(k_g, N_KV)])
    gqk = jnp.broadcast_to(gqk[:, None], (nqk, TS))
    qT, k4, vT = pl.pallas_call(
        _qkv_kernel,
        grid=(B, nt),
        in_specs=[_row_spec(D, 0), _mod_spec(d6, 0), _full_spec((1, D)), _full_spec((nqk + nkv, D)),
                  _full_spec((nqk, TS)),
                  pl.BlockSpec((HEAD_DIM, TS), lambda b, s: (0, s)),
                  pl.BlockSpec((HEAD_DIM, TS), lambda b, s: (0, s))],
        out_specs=[pl.BlockSpec((None, nq, TS), lambda b, s: (b, 0, s)),
                   pl.BlockSpec((None, N_KV, TS, HEAD_DIM), lambda b, s: (b, 0, s, 0)),
                   pl.BlockSpec((None, nkv, TS), lambda b, s: (b, 0, s))],
        out_shape=[jax.ShapeDtypeStruct((B, nq, S), BF16),
                   jax.ShapeDtypeStruct((B, N_KV, S, HEAD_DIM), BF16),
                   jax.ShapeDtypeStruct((B, nkv, S), BF16)],
        compiler_params=_params("arbitrary", "arbitrary"),
        name="attn_qkv",
    )(xs, modtab, norm_g[None], w_qkv.T.astype(BF16), gqk, cos_t, sin_t)

    gw = Q_PER_KV * HEAD_DIM
    n_lat = S - TS
    w_o = w_o.astype(BF16)
    if ctx_out:
        oT_ctx = pl.pallas_call(
            _attn_ctx_kernel,
            grid=(B, N_KV),
            in_specs=[pl.BlockSpec((None, gw, TS), lambda b, kv: (b, kv, 0)),
                      pl.BlockSpec((None, None, TS, HEAD_DIM), lambda b, kv: (b, kv, 0, 0)),
                      pl.BlockSpec((None, HEAD_DIM, TS), lambda b, kv: (b, kv, 0))],
            out_specs=pl.BlockSpec((None, gw, TS), lambda b, kv: (b, kv, 0)),
            out_shape=jax.ShapeDtypeStruct((B, nq, TS), BF16),
            compiler_params=_params("arbitrary", "arbitrary"),
            name="attn_ctx",
        )(qT, k4, vT)
        xs = pl.pallas_call(
            _oproj_kernel,
            grid=(B, 1),
            in_specs=[pl.BlockSpec((None, nq, TS), lambda b, s: (b, 0, 0)), _full_spec((nq, D)),
                      _row_spec(D, 0), _mod_spec(d6, 0)],
            out_specs=_row_spec(D, 0),
            out_shape=jax.ShapeDtypeStruct((B, S, D), F32),
            input_output_aliases={2: 0},
            compiler_params=_params("arbitrary", "arbitrary"),
            name="attn_oproj_ctx",
        )(oT_ctx, w_o, xs, modtab)

    n_units = (n_lat // TS) * Q_PER_KV
    assert n_units % 2 == 0
    oT = pl.pallas_call(
        functools.partial(_attn_main_kernel, n_units=n_units, q_col0=TS),
        grid=(B, N_KV),
        in_specs=[pl.BlockSpec((None, gw, S), lambda b, kv: (b, kv, 0)),
                  pl.BlockSpec((None, None, S, HEAD_DIM), lambda b, kv: (b, kv, 0, 0)),
                  pl.BlockSpec((None, HEAD_DIM, S), lambda b, kv: (b, kv, 0))],
        out_specs=pl.BlockSpec((None, gw, n_lat), lambda b, kv: (b, kv, 0)),
        out_shape=jax.ShapeDtypeStruct((B, nq, n_lat), BF16),
        scratch_shapes=[pltpu.VMEM((S, TS), F32), pltpu.VMEM((S, TS), F32),
                        pltpu.VMEM((1, TS), F32), pltpu.VMEM((1, TS), F32)],
        compiler_params=_params("arbitrary", "arbitrary"),
        name="attn_core",
    )(qT, k4, vT)

    return pl.pallas_call(
        _oproj_kernel,
        grid=(B, n_lat // TS),
        in_specs=[pl.BlockSpec((None, nq, TS), lambda b, s: (b, 0, s)), _full_spec((nq, D)),
                  _row_spec(D, 1), _mod_spec(d6, 1)],
        out_specs=_row_spec(D, 1),
        out_shape=jax.ShapeDtypeStruct((B, S, D), F32),
        input_output_aliases={2: 0},
        compiler_params=_params("arbitrary", "arbitrary"),
        name="attn_oproj",
    )(oT, w_o, xs, modtab)


def _normmod_kernel(x_ref, mod_ref, g_ref, h_ref):
    d = x_ref.shape[-1]
    h_ref[...] = _normmod(x_ref[...], g_ref[...], mod_ref[:, 0:d], mod_ref[:, d:2 * d])


def _pool_kernel(h_ref, x_ref, w_ref, ps_ref, gate_ref, out_ref, pad_ref, *, segs):
    gi = pl.program_id(1)
    zeros = jnp.zeros((POOL_PAD, h_ref.shape[-1]), F32)
    for widx, win in enumerate(POOL_WINDOWS):
        half = win // 2

        @pl.when(gi == widx)
        def _(half=half):
            for si, (r0, n) in enumerate(segs):
                pad_ref[0:POOL_PAD, :] = zeros
                pad_ref[POOL_PAD:POOL_PAD + n, :] = h_ref[r0:r0 + n, :]
                pad_ref[POOL_PAD + n:2 * POOL_PAD + n, :] = zeros
                gate = gate_ref[si]
                for c0 in range(0, n, ROW_CHUNK):
                    base = POOL_PAD + c0
                    acc = pad_ref[base - half:base - half + ROW_CHUNK, :]
                    for j in range(-half + 1, half):
                        acc = acc + pad_ref[base + j:base + j + ROW_CHUNK, :]
                    t = c0 + lax.broadcasted_iota(I32, (ROW_CHUNK, 1), 0)
                    cnt = jnp.minimum(t + half, n) - jnp.maximum(t - half, 0)
                    diff = acc / cnt.astype(F32) - pad_ref[base:base + ROW_CHUNK, :]
                    y = jnp.dot(diff.astype(BF16), w_ref[...], preferred_element_type=F32) * ps_ref[...]
                    rows = slice(r0 + c0, r0 + c0 + ROW_CHUNK)
                    out_ref[rows, :] = x_ref[rows, :] + gate * y


def _pool_layer(xs, modtab, norm_g, w_pool, pool_scale, m):
    B, S, D = xs.shape
    nt = S // TS
    d6 = modtab.shape[-1]
    pg = D // len(POOL_WINDOWS)
    h = pl.pallas_call(
        _normmod_kernel,
        grid=(B, nt),
        in_specs=[_row_spec(D, 0), _mod_spec(d6, 0), _full_spec((1, D))],
        out_specs=_row_spec(D, 0),
        out_shape=jax.ShapeDtypeStruct((B, S, D), F32),
        compiler_params=_params("arbitrary", "arbitrary"),
        name="pool_normmod",
    )(xs, modtab, norm_g[None])
    segs = ((0, m), (m, S - m))
    seq_spec = pl.BlockSpec((None, S, pg), lambda b, g: (b, 0, g))
    return pl.pallas_call(
        functools.partial(_pool_kernel, segs=segs),
        grid=(B, len(POOL_WINDOWS)),
        in_specs=[seq_spec, seq_spec,
                  pl.BlockSpec((None, pg, pg), lambda b, g: (g, 0, 0)),
                  pl.BlockSpec((1, pg), lambda b, g: (0, g)),
                  pl.BlockSpec((None, 2, 1, pg), lambda b, g: (b, 0, 0, 2 * (D // pg) + g))],
        out_specs=seq_spec,
        out_shape=jax.ShapeDtypeStruct((B, S, D), F32),
        scratch_shapes=[pltpu.VMEM((S - m + 2 * POOL_PAD, pg), F32)],
        compiler_params=_params("arbitrary", "arbitrary"),
        name="pool_mix",
    )(h, xs, w_pool.astype(BF16), pool_scale[None], modtab)


def _ssm_in_kernel(x_ref, mod_ref, g_ref, wz_ref, wx_ref, wdt_hi_ref, wdt_lo_ref, z_ref, xbc_ref, dt_ref):
    d = x_ref.shape[-1]
    hf = _normmod(x_ref[...], g_ref[...], mod_ref[:, 0:d], mod_ref[:, d:2 * d])
    h = hf.astype(BF16)
    z_ref[...] = jnp.dot(h, wz_ref[...], preferred_element_type=F32)
    xbc_ref[...] = jnp.dot(h, wx_ref[...], preferred_element_type=F32)
    dt_ref[...] = _dot_split(hf, wdt_hi_ref[...], wdt_lo_ref[...])


def _conv_kernel(u_ref, w_ref, b_ref, o_ref, pad_ref, *, segs):
    zeros = jnp.zeros((CONV_PAD, u_ref.shape[-1]), F32)
    for r0, n in segs:
        pad_ref[0:CONV_PAD, :] = zeros
        pad_ref[CONV_PAD:CONV_PAD + n, :] = u_ref[r0:r0 + n, :]
        pad_ref[CONV_PAD + n:2 * CONV_PAD + n, :] = zeros
        for c0 in range(0, n, ROW_CHUNK):
            acc = b_ref[...]
            for k in range(SSM_CONV):
                lo = CONV_PAD + c0 - SSM_CONV // 2 + k
                acc = acc + w_ref[k:k + 1, :] * pad_ref[lo:lo + ROW_CHUNK, :]
            o_ref[r0 + c0:r0 + c0 + ROW_CHUNK, :] = _silu(acc)


def _ssd_kernel(xf_ref, dtf_ref, xb_ref, dtb_blk_ref, dtb_ref, a_ref, tri_ref, self_ref, selb_ref,
                yf_ref, yb_ref, hf_ref, hb_ref):
    @pl.when(pl.program_id(1) == 0)
    def _():
        hf_ref[...] = jnp.zeros(hf_ref.shape, F32)
        hb_ref[...] = jnp.zeros(hb_ref.shape, F32)

    _ssd_chunk(xf_ref, dtf_ref, dtb_ref, a_ref, tri_ref, self_ref, yf_ref, hf_ref, direction=0)
    _ssd_chunk(xb_ref, dtb_blk_ref, dtb_ref, a_ref, tri_ref, selb_ref, yb_ref, hb_ref, direction=1)


def _ssd_chunk(xbc_ref, dt_ref, dtb_ref, a_ref, tri_ref, sel_ref, y_ref, h_ref, *, direction):
    q = SSM_CHUNK
    lane0 = SSM_HEADS * direction
    raw = dt_ref[...] + dtb_ref[...]
    dt = jnp.maximum(raw, 0.0) + jnp.log1p(jnp.exp(-jnp.abs(raw)))
    la = dt * a_ref[...]
    cs = jnp.dot(tri_ref[...], la, precision=HI, preferred_element_type=F32)
    tot = cs[q - 1:q, :]
    if direction == 0:
        u = cs
        dec_in = jnp.exp(u)
        dec_end = jnp.exp(tot - u)
    else:
        u = la - cs
        dec_in = jnp.exp(tot + u)
        dec_end = jnp.exp(-u)
    uT = u.T
    dtT = dt.T
    stacked = jnp.concatenate([dec_in, dt * dec_end, jnp.broadcast_to(jnp.exp(tot), (8, LANES))], axis=0)
    spread = _dot_split_lhs(stacked, sel_ref[...])
    din_col = spread[0:q]
    dtot_col = spread[2 * q:2 * q + 1]
    x = xbc_ref[:, 0:SSM_D_INNER]
    x_b = x.astype(BF16)
    xd_b = (x * spread[q:2 * q]).astype(BF16)
    row = lax.broadcasted_iota(I32, (q, q), 0)
    col = lax.broadcasted_iota(I32, (q, q), 1)
    mask = (row >= col) if direction == 0 else (col >= row)
    gw = SSM_HPG * SSM_HEAD_DIM
    for g in range(SSM_GROUPS):
        b_lo = SSM_D_INNER + SSM_STATE * g
        c_lo = SSM_D_INNER + SSM_GN + SSM_STATE * g
        bT = xbc_ref[:, b_lo:b_lo + SSM_STATE].T.astype(BF16)
        cg = xbc_ref[:, c_lo:c_lo + SSM_STATE].astype(BF16)
        scores = jnp.dot(cg, bT, preferred_element_type=F32)
        hg = h_ref[g]
        cols = slice(gw * g, gw * (g + 1))
        y_off = jnp.dot(cg, hg.astype(BF16), preferred_element_type=F32) * din_col[:, cols]
        for r in range(SSM_HPG):
            hh = SSM_HPG * g + r
            hl = lane0 + hh
            seg = u[:, hl:hl + 1] - uT[hl:hl + 1, :]
            decay = jnp.exp(jnp.where(mask, seg, NEG_INF))
            hc = slice(SSM_HEAD_DIM * hh, SSM_HEAD_DIM * (hh + 1))
            mix = (scores * decay * dtT[hl:hl + 1, :]).astype(BF16)
            y_d = jnp.dot(mix, x_b[:, hc], preferred_element_type=F32)
            y_ref[:, hc] = y_d + y_off[:, SSM_HEAD_DIM * r:SSM_HEAD_DIM * (r + 1)]
        h_ref[g] = hg * dtot_col[:, cols] + jnp.dot(bT, xd_b[:, cols], preferred_element_type=F32)


def _ssm_out_kernel(y0_ref, y1_ref, xh_ref, z_ref, dsk_ref, ng_ref, w_ref, x_ref, mod_ref, out_ref):
    d = x_ref.shape[-1]
    y = y0_ref[...] + y1_ref[...] + dsk_ref[...] * xh_ref[...]
    g = y * _silu(z_ref[...])
    gsz = SSM_D_INNER // SSM_GROUPS
    parts = []
    for k in range(SSM_GROUPS):
        gk = g[:, gsz * k:gsz * (k + 1)]
        parts.append(gk * lax.rsqrt(jnp.mean(gk * gk, axis=-1, keepdims=True) + EPS))
    gn = (jnp.concatenate(parts, axis=1) * ng_ref[...]).astype(BF16)
    o = jnp.dot(gn, w_ref[...], preferred_element_type=F32)
    out_ref[...] = x_ref[...] + mod_ref[:, 2 * d:3 * d] * o


def _ssd_layer(xs, modtab, norm_g, w_in, conv_w, conv_b, dt_bias, a_log, d_skip, ssm_norm_g, w_out, m):
    B, S, D = xs.shape
    nt = S // TS
    d6 = modtab.shape[-1]
    di = SSM_D_INNER
    cd = SSM_CONV_DIM
    wz = w_in[:, :di].astype(BF16)
    wx = w_in[:, di:di + cd].astype(BF16)
    wdt_hi, wdt_lo = _split(jnp.pad(w_in[:, di + cd:], ((0, 0), (0, LANES - 2 * SSM_HEADS))))
    z, xbc_raw, dt_raw = pl.pallas_call(
        _ssm_in_kernel,
        grid=(B, nt),
        in_specs=[_row_spec(D, 0), _mod_spec(d6, 0), _full_spec((1, D)), _full_spec((D, di)),
                  _full_spec((D, cd)), _full_spec((D, LANES)), _full_spec((D, LANES))],
        out_specs=[_row_spec(di, 0), _row_spec(cd, 0), _row_spec(LANES, 0)],
        out_shape=[jax.ShapeDtypeStruct((B, S, di), F32), jax.ShapeDtypeStruct((B, S, cd), F32),
                   jax.ShapeDtypeStruct((B, S, LANES), F32)],
        compiler_params=_params("arbitrary", "arbitrary"),
        name="ssm_in",
    )(xs, modtab, norm_g[None], wz, wx, wdt_hi, wdt_lo)

    segs = ((0, m), (m, S - m))
    cw = 256
    seq_spec = pl.BlockSpec((None, S, cw), lambda b, j: (b, 0, j))
    xbc = pl.pallas_call(
        functools.partial(_conv_kernel, segs=segs),
        grid=(B, cd // cw),
        in_specs=[seq_spec, pl.BlockSpec((SSM_CONV, cw), lambda b, j: (0, j)),
                  pl.BlockSpec((1, cw), lambda b, j: (0, j))],
        out_specs=seq_spec,
        out_shape=jax.ShapeDtypeStruct((B, S, cd), F32),
        scratch_shapes=[pltpu.VMEM((S - m + 2 * CONV_PAD, cw), F32)],
        compiler_params=_params("arbitrary", "arbitrary"),
        name="ssm_conv",
    )(xbc_raw, conv_w, conv_b[None])

    q = SSM_CHUNK
    nc = S // q
    mc = m // q
    pad = LANES - 2 * SSM_HEADS
    dtb = jnp.pad(dt_bias.reshape(-1), (0, pad))[None]
    a_neg = jnp.pad(-jnp.exp(a_log.reshape(-1)), (0, pad))[None]
    tri = (jnp.arange(q)[:, None] >= jnp.arange(q)[None, :]).astype(F32)
    fmap = lambda c: c
    bmap = lambda c: jnp.where(c < mc, mc - 1 - c, nc - 1 - (c - mc))
    sels = [(jnp.arange(LANES)[:, None] == (SSM_HEADS * dr + jnp.arange(di) // SSM_HEAD_DIM)[None, :]).astype(BF16)
            for dr in range(2)]
    chunk_spec = lambda width, cmap: pl.BlockSpec((None, q, width), lambda b, c: (b, cmap(c), 0))
    state = pltpu.VMEM((SSM_GROUPS, SSM_STATE, SSM_HPG * SSM_HEAD_DIM), F32)
    ys = pl.pallas_call(
        _ssd_kernel,
        grid=(B, nc),
        in_specs=[chunk_spec(cd, fmap), chunk_spec(LANES, fmap), chunk_spec(cd, bmap), chunk_spec(LANES, bmap),
                  _full_spec((1, LANES)), _full_spec((1, LANES)), _full_spec((q, q)),
                  _full_spec((LANES, di)), _full_spec((LANES, di))],
        out_specs=[chunk_spec(di, fmap), chunk_spec(di, bmap)],
        out_shape=[jax.ShapeDtypeStruct((B, S, di), F32), jax.ShapeDtypeStruct((B, S, di), F32)],
        scratch_shapes=[state, state],
        compiler_params=_params("arbitrary", "arbitrary"),
        name="ssd_scan",
    )(xbc, dt_raw, xbc, dt_raw, dtb, a_neg, tri, sels[0], sels[1])

    dsk = jnp.repeat(d_skip[0] + d_skip[1], SSM_HEAD_DIM)[None]
    return pl.pallas_call(
        _ssm_out_kernel,
        grid=(B, nt),
        in_specs=[_row_spec(di, 0), _row_spec(di, 0), _row_spec(di, 0), _row_spec(di, 0),
                  _full_spec((1, di)), _full_spec((1, di)), _full_spec((di, D)),
                  _row_spec(D, 0), _mod_spec(d6, 0)],
        out_specs=_row_spec(D, 0),
        out_shape=jax.ShapeDtypeStruct((B, S, D), F32),
        input_output_aliases={7: 0},
        compiler_params=_params("arbitrary", "arbitrary"),
        name="ssm_out",
    )(ys[0], ys[1], xbc, z, dsk, ssm_norm_g[None], w_out.astype(BF16), xs, modtab)


def _store_slabs(ref, v):
    rows = v.shape[0]
    for s in range(SUBLANES):
        ref[pl.ds(s, rows, stride=SUBLANES), :] = v[:, LANES * s:LANES * (s + 1)]


def _load_slab_chunk(ref, rows, s, row0=0):
    return ref[pl.ds(SUBLANES * row0 + s, rows, stride=SUBLANES), :]


def _slab(ref, row):
    return ref.at[pl.ds(pl.multiple_of(row * SUBLANES, SUBLANES), SUBLANES)]


def _route(logits):
    lane = lax.broadcasted_iota(I32, logits.shape, 1).astype(F32)
    big = float(LANES)
    gl = jnp.where(lane < MOE_GROUPS, logits, NEG_INF)
    gmax = jnp.max(gl, axis=1, keepdims=True)
    gate = 1.0 / jnp.sum(jnp.exp(gl - gmax), axis=1, keepdims=True)
    grp = jnp.min(jnp.where(gl == gmax, lane, big), axis=1, keepdims=True)
    lo = MOE_GROUPS + MOE_EPG * grp
    el = jnp.where((lane >= lo) & (lane < lo + MOE_EPG), logits, NEG_INF)
    v1 = jnp.max(el, axis=1, keepdims=True)
    i1 = jnp.min(jnp.where(el == v1, lane, big), axis=1, keepdims=True)
    el2 = jnp.where(lane == i1, NEG_INF, el)
    v2 = jnp.max(el2, axis=1, keepdims=True)
    i2 = jnp.min(jnp.where(el2 == v2, lane, big), axis=1, keepdims=True)
    e2 = jnp.exp(v2 - v1)
    den = 1.0 + e2
    return i1, i2, gate * (1.0 / den), gate * (e2 / den)


def _router_kernel(x_ref, mod_ref, g_ref, wr_hi_ref, wr_lo_ref, ltri_ref, tok_ref, meta_ref, wt_ref, cnt_ref,
                   run_ref):
    d = x_ref.shape[-1]

    @pl.when((pl.program_id(0) == 0) & (pl.program_id(1) == 0))
    def _():
        run_ref[...] = jnp.zeros(run_ref.shape, F32)

    t = _normmod(x_ref[...], g_ref[...], mod_ref[:, 3 * d:4 * d], mod_ref[:, 4 * d:5 * d])
    _store_slabs(tok_ref, t)
    logits = _dot_split(t, wr_hi_ref[...], wr_lo_ref[...])
    i1, i2, w1, w2 = _route(logits)
    lane = lax.broadcasted_iota(I32, logits.shape, 1).astype(F32)
    wt_ref[...] = jnp.where(lane == 0.0, w1, jnp.where(lane == 1.0, w2, 0.0))
    oh1 = (lane == i1).astype(F32)
    oh2 = (lane == i2).astype(F32)
    oh = oh1 + oh2
    before = jnp.dot(ltri_ref[...], oh.astype(BF16), preferred_element_type=F32) + run_ref[...]
    r1 = jnp.sum(oh1 * before, axis=1, keepdims=True)
    r2 = jnp.sum(oh2 * before, axis=1, keepdims=True)
    run = run_ref[...] + jnp.sum(oh, axis=0, keepdims=True)
    run_ref[...] = run
    cnt_ref[...] = jnp.broadcast_to(run, cnt_ref.shape)
    cols = jnp.where(lane == 0.0, i1 - MOE_GROUPS,
                     jnp.where(lane == 1.0, i2 - MOE_GROUPS,
                               jnp.where(lane == 2.0, r1, jnp.where(lane == 3.0, r2, 0.0))))
    meta_ref[...] = cols.T[0:8, :].astype(I32)


def _row_copy(src_ref, src_row, dst_ref, dst_row, sem):
    return pltpu.make_async_copy(_slab(src_ref, src_row), _slab(dst_ref, dst_row), sem)


def _dispatch_kernel(dest_ref, tok_ref, init_hbm, xs_hbm, sem):
    del init_hbm

    for h in range(DISPATCH_TILES):
        def body(j, carry, h=h):
            _row_copy(tok_ref, h * TS + j, xs_hbm, dest_ref[h, 0, j], sem).start()
            _row_copy(tok_ref, h * TS + j, xs_hbm, dest_ref[h, 1, j], sem).start()
            return carry

        lax.fori_loop(0, TS, body, 0, unroll=DMA_UNROLL)
    for _ in range(2):
        pltpu.make_async_copy(tok_ref, xs_hbm.at[pl.ds(0, tok_ref.shape[0])], sem).wait()


def _expert_kernel(be_ref, nu_ref, xs_ref, wgu_ref, wd_ref, y_ref, wgu_b_ref, wd_b_ref):
    i = pl.program_id(0)
    rows = xs_ref.shape[0] // SUBLANES

    @pl.when((i == 0) | (be_ref[i] != be_ref[jnp.maximum(i - 1, 0)]))
    def _():
        wgu_b_ref[...] = wgu_ref[...].astype(BF16)
        wd_b_ref[...] = wd_ref[...].astype(BF16)

    @pl.when(i < nu_ref[0])
    def _():
        x = jnp.concatenate([_load_slab_chunk(xs_ref, rows, s) for s in range(SUBLANES)], axis=1)
        gu = jnp.dot(x.astype(BF16), wgu_b_ref[...], preferred_element_type=F32)
        hid = _silu(gu[:, :MOE_HIDDEN]) * gu[:, MOE_HIDDEN:]
        _store_slabs(y_ref, jnp.dot(hid.astype(BF16), wd_b_ref[...], preferred_element_type=F32))

    @pl.when(i >= nu_ref[0])
    def _():
        y_ref[...] = jnp.zeros(y_ref.shape, F32)


def _combine_kernel(dest_ref, next_dest_ref, x_ref, mod_ref, wt_ref, y_hbm, *rest, final_norm):
    if final_norm:
        fg_ref, out_ref, buf_ref, sem = rest
    else:
        out_ref, buf_ref, sem = rest
    d = x_ref.shape[-1]
    step = pl.program_id(0) * pl.num_programs(1) + pl.program_id(1)
    n_steps = pl.num_programs(0) * pl.num_programs(1)
    slot = step % 2
    slot_rows = 2 * TS

    def gather(d_ref, to_slot):
        base = to_slot * slot_rows

        def body(j, carry):
            _row_copy(y_hbm, d_ref[0, j], buf_ref, base + j, sem.at[to_slot]).start()
            _row_copy(y_hbm, d_ref[1, j], buf_ref, base + TS + j, sem.at[to_slot]).start()
            return carry

        lax.fori_loop(0, TS, body, 0, unroll=DMA_UNROLL)

    @pl.when(step == 0)
    def _():
        gather(dest_ref, 0)

    @pl.when(step + 1 < n_steps)
    def _():
        gather(next_dest_ref, 1 - slot)

    row0 = slot * slot_rows
    slot_view = buf_ref.at[pl.ds(pl.multiple_of(row0 * SUBLANES, SUBLANES), slot_rows * SUBLANES)]
    pltpu.make_async_copy(y_hbm.at[pl.ds(0, slot_rows * SUBLANES)], slot_view, sem.at[slot]).wait()
    w1 = wt_ref[:, 0:1]
    w2 = wt_ref[:, 1:2]
    for s in range(SUBLANES):
        cols = slice(LANES * s, LANES * (s + 1))
        f = (w1 * _load_slab_chunk(buf_ref, TS, s, row0=row0)
             + w2 * _load_slab_chunk(buf_ref, TS, s, row0=row0 + TS))
        out_ref[:, cols] = x_ref[:, cols] + mod_ref[:, 5 * d + LANES * s:5 * d + LANES * (s + 1)] * f
    if final_norm:
        v = out_ref[...]
        out_ref[...] = v * lax.rsqrt(jnp.mean(v * v, axis=-1, keepdims=True) + EPS) * fg_ref[...]


def _moe_layer(xs, modtab, norm_g, w_rg, w_re, w_gu, w_down, layer, ctx_out, final_g=None):
    assert final_g is None or not ctx_out
    B, S, D = xs.shape
    assert D == SUBLANES * LANES
    d6 = modtab.shape[-1]
    t0 = 0 if ctx_out else 1
    nt = S // TS - t0
    R = nt * TS
    wr_hi, wr_lo = _split(
        jnp.pad(jnp.concatenate([w_rg, w_re], axis=1), ((0, 0), (0, LANES - MOE_GROUPS - MOE_EXPERTS))))
    out_row = lambda width: pl.BlockSpec((None, TS, width), lambda b, s: (b, s, 0))
    ltri = (jnp.arange(TS)[:, None] > jnp.arange(TS)[None, :]).astype(BF16)
    tok, meta, wt, cnt = pl.pallas_call(
        _router_kernel,
        grid=(B, nt),
        in_specs=[_row_spec(D, t0), _mod_spec(d6, t0), _full_spec((1, D)), _full_spec((D, LANES)),
                  _full_spec((D, LANES)), _full_spec((TS, TS))],
        out_specs=[pl.BlockSpec((TS * SUBLANES, LANES), lambda b, s: (b * nt + s, 0)),
                   pl.BlockSpec((None, 8, TS), lambda b, s: (b * nt + s, 0, 0)), out_row(LANES),
                   _full_spec((8, LANES))],
        out_shape=[jax.ShapeDtypeStruct((B * R * SUBLANES, LANES), F32), jax.ShapeDtypeStruct((B * nt, 8, TS), I32),
                   jax.ShapeDtypeStruct((B, R, LANES), F32), jax.ShapeDtypeStruct((8, LANES), F32)],
        scratch_shapes=[pltpu.VMEM((1, LANES), F32)],
        compiler_params=_params("arbitrary", "arbitrary"),
        name="moe_router",
    )(xs, modtab, norm_g[None], wr_hi, wr_lo, ltri)

    T = B * R
    n_blocks = -(-2 * T // MOE_BLOCK) + MOE_EXPERTS
    n_rows = n_blocks * MOE_BLOCK
    counts = cnt[0, MOE_GROUPS:MOE_GROUPS + MOE_EXPERTS].astype(I32)
    padded = (counts + MOE_BLOCK - 1) // MOE_BLOCK * MOE_BLOCK
    pends = jnp.cumsum(padded)
    pstarts = pends - padded
    blk_start = jnp.arange(n_blocks, dtype=I32) * MOE_BLOCK
    block_expert = jnp.minimum(jnp.sum(pends[None, :] <= blk_start[:, None], axis=1),
                               MOE_EXPERTS - 1).astype(I32)
    n_used = (pends[-1:] // MOE_BLOCK).astype(I32)

    hbm = pl.BlockSpec(memory_space=pltpu.MemorySpace.HBM)
    expert_start = jnp.sum(jnp.where(meta[:, 0:2, :, None] == jnp.arange(MOE_EXPERTS), pstarts, 0), axis=-1)
    dest = expert_start + meta[:, 2:4, :]
    assert (B * nt) % DISPATCH_TILES == 0
    x_sorted = pl.pallas_call(
        _dispatch_kernel,
        grid=(B * nt // DISPATCH_TILES,),
        in_specs=[pl.BlockSpec((DISPATCH_TILES, 2, TS), lambda i: (i, 0, 0), memory_space=pltpu.SMEM),
                  pl.BlockSpec((DISPATCH_TILES * TS * SUBLANES, LANES), lambda i: (i, 0)), hbm],
        out_specs=hbm,
        out_shape=jax.ShapeDtypeStruct((n_rows * SUBLANES, LANES), F32),
        scratch_shapes=[pltpu.SemaphoreType.DMA(())],
        input_output_aliases={2: 0},
        compiler_params=_params("arbitrary"),
        name="moe_dispatch",
    )(dest, tok, jnp.zeros((n_rows * SUBLANES, LANES), F32))

    blk_spec = pl.BlockSpec((MOE_BLOCK * SUBLANES, LANES), lambda i, be, nu: (i, 0))
    y_sorted = pl.pallas_call(
        _expert_kernel,
        grid_spec=pltpu.PrefetchScalarGridSpec(
            num_scalar_prefetch=2,
            grid=(n_blocks,),
            in_specs=[blk_spec,
                      pl.BlockSpec((None, None, D, 2 * MOE_HIDDEN), lambda i, be, nu: (layer, be[i], 0, 0)),
                      pl.BlockSpec((None, None, MOE_HIDDEN, D), lambda i, be, nu: (layer, be[i], 0, 0))],
            out_specs=blk_spec,
            scratch_shapes=[pltpu.VMEM((D, 2 * MOE_HIDDEN), BF16), pltpu.VMEM((MOE_HIDDEN, D), BF16)],
        ),
        out_shape=jax.ShapeDtypeStruct((n_rows * SUBLANES, LANES), F32),
        compiler_params=_params("arbitrary"),
        name="moe_experts",
    )(block_expert, n_used, x_sorted, w_gu, w_down)

    in_specs = [pl.BlockSpec((None, 2, TS), lambda b, s: (b * nt + s, 0, 0), memory_space=pltpu.SMEM),
                pl.BlockSpec((None, 2, TS), lambda b, s: (jnp.minimum(b * nt + s + 1, B * nt - 1), 0, 0),
                             memory_space=pltpu.SMEM),
                _row_spec(D, t0), _mod_spec(d6, t0), out_row(LANES), hbm]
    operands = [dest, dest, xs, modtab, wt, y_sorted]
    if final_g is None:
        out_spec, out_shape, aliases = _row_spec(D, t0), jax.ShapeDtypeStruct((B, S, D), F32), {2: 0}
    else:
        in_specs.append(_full_spec((1, D)))
        operands.append(final_g[None])
        out_spec, out_shape, aliases = out_row(D), jax.ShapeDtypeStruct((B, R, D), F32), {}
    return pl.pallas_call(
        functools.partial(_combine_kernel, final_norm=final_g is not None),
        grid=(B, nt),
        in_specs=in_specs,
        out_specs=out_spec,
        out_shape=out_shape,
        scratch_shapes=[pltpu.VMEM((2 * 2 * TS * SUBLANES, LANES), F32), pltpu.SemaphoreType.DMA((2,))],
        input_output_aliases=aliases,
        compiler_params=_params("arbitrary", "arbitrary"),
        name="moe_combine",
    )(*operands)


def kernel(x, c, ctx, c_ctx, w_ada, b_ada, norm_mix_g, norm_ffn_g, final_norm_g, attn_w_qkv, attn_w_o, attn_q_norm_g, attn_k_norm_g, pool_w, pool_scale, ssm_w_in, ssm_conv_w, ssm_conv_b, ssm_dt_bias, ssm_a_log, ssm_d, ssm_norm_g, ssm_w_out, moe_w_router_group, moe_w_router_expert, moe_w_gate_up, moe_w_down):
    B, n, D = x.shape
    m = ctx.shape[1]
    depth = w_ada.shape[0]
    assert m == TS and n % TS == 0 and n % GRID_W == 0
    xs = jnp.concatenate([ctx, x], axis=1)
    mods = _ada(c, c_ctx, w_ada, b_ada)
    cos_t, sin_t = _rope_tables(n, m)
    for i in range(depth):
        kind, j = i % N_MIXERS, i // N_MIXERS
        ctx_out = i < depth - 1
        modtab = jnp.stack([jnp.broadcast_to(mods[i, B], (B, 6 * D)), mods[i, :B]], axis=1)[:, :, None, :]
        if kind == 0:
            xs = _attention_layer(xs, modtab, norm_mix_g[i], attn_w_qkv[j], attn_w_o[j], attn_q_norm_g[j],
                                  attn_k_norm_g[j], cos_t, sin_t, ctx_out)
        elif kind == 1:
            assert ctx_out
            xs = _pool_layer(xs, modtab, norm_mix_g[i], pool_w[j], pool_scale[j], m)
        else:
            assert ctx_out
            xs = _ssd_layer(xs, modtab, norm_mix_g[i], ssm_w_in[j], ssm_conv_w[j], ssm_conv_b[j],
                            ssm_dt_bias[j], ssm_a_log[j], ssm_d[j], ssm_norm_g[j], ssm_w_out[j], m)
        xs = _moe_layer(xs, modtab, norm_ffn_g[i], moe_w_router_group[i], moe_w_router_expert[i],
                        moe_w_gate_up, moe_w_down, i, ctx_out,
                        final_g=final_norm_g if i == depth - 1 else None)
    return xs
```

```python
import functools

import jax
import jax.numpy as jnp
from jax import lax
from jax.experimental import pallas as pl
from jax.experimental.pallas import tpu as pltpu

F32 = jnp.float32
BF16 = jnp.bfloat16
I32 = jnp.int32
HI = lax.Precision.HIGHEST
EPS = 1e-6
NEG_INF = float("-inf")
LOG2E = 1.4426950408889634

TS = 256
LANES = 128
SUBLANES = 8
GRID_W = 64
ROPE_THETA = 10000.0
N_MIXERS = 3

N_HEADS = 16
N_KV = 4
HEAD_DIM = 64
Q_PER_KV = N_HEADS // N_KV

POOL_WINDOWS = (2, 4, 8, 16)
POOL_PAD = 16
ROW_CHUNK = 256

SSM_HEADS = 32
SSM_HEAD_DIM = 64
SSM_GROUPS = 4
SSM_HPG = SSM_HEADS // SSM_GROUPS
SSM_STATE = 128
SSM_CONV = 4
SSM_CHUNK = 128
SSM_D_INNER = SSM_HEADS * SSM_HEAD_DIM
SSM_GN = SSM_GROUPS * SSM_STATE
SSM_CONV_DIM = SSM_D_INNER + 2 * SSM_GN
CONV_PAD = 8

MOE_GROUPS = 4
MOE_EPG = 8
MOE_EXPERTS = MOE_GROUPS * MOE_EPG
MOE_HIDDEN = 512
MOE_BLOCK = 256
DMA_UNROLL = 8
DISPATCH_TILES = 4


def _params(*sem):
    return pltpu.CompilerParams(dimension_semantics=sem)


def _silu(v):
    return v / (1.0 + jnp.exp(-v))


def _split(v):
    hi = v.astype(BF16)
    return hi, (v - hi.astype(F32)).astype(BF16)


def _dot_split_lhs(a, e):
    hi, lo = _split(a)
    return jnp.dot(hi, e, preferred_element_type=F32) + jnp.dot(lo, e, preferred_element_type=F32)


def _dot_split(a, b_hi, b_lo):
    hi, lo = _split(a)
    return jnp.dot(hi, b_hi, preferred_element_type=F32) + (
        jnp.dot(hi, b_lo, preferred_element_type=F32) + jnp.dot(lo, b_hi, preferred_element_type=F32))


def _normmod(x, g, shift, scale):
    ms = jnp.mean(x * x, axis=-1, keepdims=True)
    return (x * lax.rsqrt(ms + EPS) * g) * (1.0 + scale) + shift


def _ada_kernel(a_ref, w_ref, b_ref, o_ref):
    a = _silu(a_ref[...])
    o_ref[0] = jnp.dot(a, w_ref[0], precision=HI, preferred_element_type=F32) + b_ref[0]


def _ada(c, c_ctx, w_ada, b_ada):
    depth, d, d6 = w_ada.shape
    b = c.shape[0]
    assert b + 1 <= 8
    a = jnp.concatenate([c, c_ctx[None], jnp.zeros((8 - b - 1, d), F32)], axis=0)
    tn = 1536
    return pl.pallas_call(
        _ada_kernel,
        grid=(depth, d6 // tn),
        in_specs=[pl.BlockSpec((8, d), lambda i, j: (0, 0)),
                  pl.BlockSpec((1, d, tn), lambda i, j: (i, 0, j)),
                  pl.BlockSpec((1, 1, tn), lambda i, j: (i, 0, j))],
        out_specs=pl.BlockSpec((1, 8, tn), lambda i, j: (i, 0, j)),
        out_shape=jax.ShapeDtypeStruct((depth, 8, d6), F32),
        compiler_params=_params("arbitrary", "arbitrary"),
        name="ada",
    )(a, w_ada, b_ada.reshape(depth, 1, d6))


def _row_spec(width, t0, col=0):
    return pl.BlockSpec((None, TS, width), lambda b, s: (b, s + t0, col))


def _mod_spec(d6, t0):
    return pl.BlockSpec((None, None, 1, d6), lambda b, s: (b, jnp.minimum(s + t0, 1), 0, 0))


def _full_spec(shape):
    nd = len(shape)
    return pl.BlockSpec(shape, lambda b, s: (0,) * nd)


def _qkv_kernel(x_ref, mod_ref, g_ref, wT_ref, gqk_ref, cos_ref, sin_ref, qT_ref, k_ref, vT_ref):
    d = x_ref.shape[-1]
    nq = N_HEADS * HEAD_DIM
    nqk = nq + N_KV * HEAD_DIM
    h = _normmod(x_ref[...], g_ref[...], mod_ref[:, 0:d], mod_ref[:, d:2 * d])
    t = jnp.dot(wT_ref[...], h.T.astype(BF16), preferred_element_type=F32)
    cos = cos_ref[...]
    sin = sin_ref[...]
    row = lax.broadcasted_iota(I32, (HEAD_DIM, TS), 0)
    even = (row & 1) == 0
    for hh in range(nqk // HEAD_DIM):
        rows = slice(HEAD_DIM * hh, HEAD_DIM * (hh + 1))
        blk = t[rows, :]
        rinv = lax.rsqrt(jnp.mean(blk * blk, axis=0, keepdims=True) + EPS)
        y = blk * rinv * gqk_ref[rows, :]
        partner = jnp.where(even, pltpu.roll(y, HEAD_DIM - 1, 0), pltpu.roll(y, 1, 0))
        out = y * cos + partner * sin
        if hh < N_HEADS:
            qT_ref[rows, :] = out.astype(BF16)
        else:
            k_ref[hh - N_HEADS] = out.T.astype(BF16)
    vT_ref[...] = t[nqk:, :].astype(BF16)


def _attn_ctx_kernel(qT_ref, k_ref, vT_ref, o_ref):
    for g in range(Q_PER_KV):
        rows = slice(HEAD_DIM * g, HEAD_DIM * (g + 1))
        s = jnp.dot(k_ref[...], qT_ref[rows, :], preferred_element_type=F32)
        p = jnp.exp2(s - jnp.max(s, axis=0, keepdims=True))
        l = jnp.sum(p, axis=0, keepdims=True)
        o = jnp.dot(vT_ref[...], p.astype(BF16), preferred_element_type=F32)
        o_ref[rows, :] = (o / l).astype(BF16)


def _attn_main_kernel(qT_ref, k_ref, vT_ref, o_ref, s0_ref, s1_ref, m0_ref, m1_ref, *, n_units, q_col0):
    for ref in (s0_ref, s1_ref, m0_ref, m1_ref):
        ref[...] = jnp.zeros(ref.shape, F32)

    def unit(i):
        i = jnp.clip(i, 0, n_units - 1)
        row = pl.multiple_of((i % Q_PER_KV) * HEAD_DIM, HEAD_DIM)
        col = pl.multiple_of((i // Q_PER_KV) * TS, TS)
        return row, col

    def step(i, s_w, m_w, s_r, m_r):
        row, col = unit(i)
        q = qT_ref[pl.ds(row, HEAD_DIM), pl.ds(q_col0 + col, TS)]
        s = jnp.dot(k_ref[...], q, preferred_element_type=F32)
        s_w[...] = s
        m_w[...] = jnp.max(s, axis=0, keepdims=True)
        row, col = unit(i - 1)
        p = jnp.exp2(s_r[...] - m_r[...])
        l = jnp.sum(p, axis=0, keepdims=True)
        o = jnp.dot(vT_ref[...], p.astype(BF16), preferred_element_type=F32)
        o_ref[pl.ds(row, HEAD_DIM), pl.ds(col, TS)] = (o / l).astype(BF16)

    def body(j, carry):
        step(2 * j, s0_ref, m0_ref, s1_ref, m1_ref)
        step(2 * j + 1, s1_ref, m1_ref, s0_ref, m0_ref)
        return carry

    lax.fori_loop(0, n_units // 2 + 1, body, 0)


def _oproj_kernel(oT_ref, w_ref, x_ref, mod_ref, out_ref):
    d = x_ref.shape[-1]
    y = lax.dot_general(oT_ref[...], w_ref[...], (((0,), (0,)), ((), ())), preferred_element_type=F32)
    out_ref[...] = x_ref[...] + mod_ref[:, 2 * d:3 * d] * y


def _rope_tables(n, m):
    rows = n // GRID_W
    row = jnp.broadcast_to(jnp.arange(rows)[:, None], (rows, GRID_W)).reshape(-1).astype(F32)
    col = jnp.broadcast_to(jnp.arange(GRID_W)[None, :], (rows, GRID_W)).reshape(-1).astype(F32)
    n_freq = HEAD_DIM // 4
    inv_freq = ROPE_THETA ** (-jnp.arange(n_freq, dtype=F32) / n_freq)
    ang = jnp.concatenate([row[:, None] * inv_freq, col[:, None] * inv_freq], axis=-1)
    cos = jnp.repeat(jnp.cos(ang), 2, axis=1)
    sign = jnp.tile(jnp.array([-1.0, 1.0], F32), HEAD_DIM // 2)
    sin = jnp.repeat(jnp.sin(ang), 2, axis=1) * sign
    cos = jnp.concatenate([jnp.ones((m, HEAD_DIM), F32), cos], axis=0)
    sin = jnp.concatenate([jnp.zeros((m, HEAD_DIM), F32), sin], axis=0)
    return cos.T, sin.T


def _attention_layer(xs, modtab, norm_g, w_qkv, w_o, q_g, k_g, cos_t, sin_t, ctx_out):
    B, S, D = xs.shape
    nt = S // TS
    nq = N_HEADS * HEAD_DIM
    nkv = N_KV * HEAD_DIM
    nqk = nq + nkv
    d6 = modtab.shape[-1]
    q_scale = HEAD_DIM ** -0.5 * LOG2E
    gqk = jnp.concatenate([jnp.tile(q_g, N_HEADS) * q_scale, jnp.tile(k_g, N_KV)])
    gqk = jnp.broadcast_to(gqk[:, None], (nqk, TS))
    qT, k4, vT = pl.pallas_call(
        _qkv_kernel,
        grid=(B, nt),
        in_specs=[_row_spec(D, 0), _mod_spec(d6, 0), _full_spec((1, D)), _full_spec((nqk + nkv, D)),
                  _full_spec((nqk, TS)),
                  pl.BlockSpec((HEAD_DIM, TS), lambda b, s: (0, s)),
                  pl.BlockSpec((HEAD_DIM, TS), lambda b, s: (0, s))],
        out_specs=[pl.BlockSpec((None, nq, TS), lambda b, s: (b, 0, s)),
                   pl.BlockSpec((None, N_KV, TS, HEAD_DIM), lambda b, s: (b, 0, s, 0)),
                   pl.BlockSpec((None, nkv, TS), lambda b, s: (b, 0, s))],
        out_shape=[jax.ShapeDtypeStruct((B, nq, S), BF16),
                   jax.ShapeDtypeStruct((B, N_KV, S, HEAD_DIM), BF16),
                   jax.ShapeDtypeStruct((B, nkv, S), BF16)],
        compiler_params=_params("arbitrary", "arbitrary"),
        name="attn_qkv",
    )(xs, modtab, norm_g[None], w_qkv.T.astype(BF16), gqk, cos_t, sin_t)

    gw = Q_PER_KV * HEAD_DIM
    n_lat = S - TS
    w_o = w_o.astype(BF16)
    if ctx_out:
        oT_ctx = pl.pallas_call(
            _attn_ctx_kernel,
            grid=(B, N_KV),
            in_specs=[pl.BlockSpec((None, gw, TS), lambda b, kv: (b, kv, 0)),
                      pl.BlockSpec((None, None, TS, HEAD_DIM), lambda b, kv: (b, kv, 0, 0)),
                      pl.BlockSpec((None, HEAD_DIM, TS), lambda b, kv: (b, kv, 0))],
            out_specs=pl.BlockSpec((None, gw, TS), lambda b, kv: (b, kv, 0)),
            out_shape=jax.ShapeDtypeStruct((B, nq, TS), BF16),
            compiler_params=_params("arbitrary", "arbitrary"),
            name="attn_ctx",
        )(qT, k4, vT)
        xs = pl.pallas_call(
            _oproj_kernel,
            grid=(B, 1),
            in_specs=[pl.BlockSpec((None, nq, TS), lambda b, s: (b, 0, 0)), _full_spec((nq, D)),
                      _row_spec(D, 0), _mod_spec(d6, 0)],
            out_specs=_row_spec(D, 0),
            out_shape=jax.ShapeDtypeStruct((B, S, D), F32),
            input_output_aliases={2: 0},
            compiler_params=_params("arbitrary", "arbitrary"),
            name="attn_oproj_ctx",
        )(oT_ctx, w_o, xs, modtab)

    n_units = (n_lat // TS) * Q_PER_KV
    assert n_units % 2 == 0
    oT = pl.pallas_call(
        functools.partial(_attn_main_kernel, n_units=n_units, q_col0=TS),
        grid=(B, N_KV),
        in_specs=[pl.BlockSpec((None, gw, S), lambda b, kv: (b, kv, 0)),
                  pl.BlockSpec((None, None, S, HEAD_DIM), lambda b, kv: (b, kv, 0, 0)),
                  pl.BlockSpec((None, HEAD_DIM, S), lambda b, kv: (b, kv, 0))],
        out_specs=pl.BlockSpec((None, gw, n_lat), lambda b, kv: (b, kv, 0)),
        out_shape=jax.ShapeDtypeStruct((B, nq, n_lat), BF16),
        scratch_shapes=[pltpu.VMEM((S, TS), F32), pltpu.VMEM((S, TS), F32),
                        pltpu.VMEM((1, TS), F32), pltpu.VMEM((1, TS), F32)],
        compiler_params=_params("arbitrary", "arbitrary"),
        name="attn_core",
    )(qT, k4, vT)

    return pl.pallas_call(
        _oproj_kernel,
        grid=(B, n_lat // TS),
        in_specs=[pl.BlockSpec((None, nq, TS), lambda b, s: (b, 0, s)), _full_spec((nq, D)),
                  _row_spec(D, 1), _mod_spec(d6, 1)],
        out_specs=_row_spec(D, 1),
        out_shape=jax.ShapeDtypeStruct((B, S, D), F32),
        input_output_aliases={2: 0},
        compiler_params=_params("arbitrary", "arbitrary"),
        name="attn_oproj",
    )(oT, w_o, xs, modtab)


def _normmod_kernel(x_ref, mod_ref, g_ref, h_ref):
    d = x_ref.shape[-1]
    h_ref[...] = _normmod(x_ref[...], g_ref[...], mod_ref[:, 0:d], mod_ref[:, d:2 * d])


def _pool_kernel(h_ref, x_ref, w_ref, ps_ref, gate_ref, out_ref, pad_ref, *, segs):
    gi = pl.program_id(1)
    zeros = jnp.zeros((POOL_PAD, h_ref.shape[-1]), F32)
    for widx, win in enumerate(POOL_WINDOWS):
        half = win // 2

        @pl.when(gi == widx)
        def _(half=half):
            for si, (r0, n) in enumerate(segs):
                pad_ref[0:POOL_PAD, :] = zeros
                pad_ref[POOL_PAD:POOL_PAD + n, :] = h_ref[r0:r0 + n, :]
                pad_ref[POOL_PAD + n:2 * POOL_PAD + n, :] = zeros
                gate = gate_ref[si]
                for c0 in range(0, n, ROW_CHUNK):
                    base = POOL_PAD + c0
                    acc = pad_ref[base - half:base - half + ROW_CHUNK, :]
                    for j in range(-half + 1, half):
                        acc = acc + pad_ref[base + j:base + j + ROW_CHUNK, :]
                    t = c0 + lax.broadcasted_iota(I32, (ROW_CHUNK, 1), 0)
                    cnt = jnp.minimum(t + half, n) - jnp.maximum(t - half, 0)
                    diff = acc / cnt.astype(F32) - pad_ref[base:base + ROW_CHUNK, :]
                    y = jnp.dot(diff.astype(BF16), w_ref[...], preferred_element_type=F32) * ps_ref[...]
                    rows = slice(r0 + c0, r0 + c0 + ROW_CHUNK)
                    out_ref[rows, :] = x_ref[rows, :] + gate * y


def _pool_layer(xs, modtab, norm_g, w_pool, pool_scale, m):
    B, S, D = xs.shape
    nt = S // TS
    d6 = modtab.shape[-1]
    pg = D // len(POOL_WINDOWS)
    h = pl.pallas_call(
        _normmod_kernel,
        grid=(B, nt),
        in_specs=[_row_spec(D, 0), _mod_spec(d6, 0), _full_spec((1, D))],
        out_specs=_row_spec(D, 0),
        out_shape=jax.ShapeDtypeStruct((B, S, D), F32),
        compiler_params=_params("arbitrary", "arbitrary"),
        name="pool_normmod",
    )(xs, modtab, norm_g[None])
    segs = ((0, m), (m, S - m))
    seq_spec = pl.BlockSpec((None, S, pg), lambda b, g: (b, 0, g))
    return pl.pallas_call(
        functools.partial(_pool_kernel, segs=segs),
        grid=(B, len(POOL_WINDOWS)),
        in_specs=[seq_spec, seq_spec,
                  pl.BlockSpec((None, pg, pg), lambda b, g: (g, 0, 0)),
                  pl.BlockSpec((1, pg), lambda b, g: (0, g)),
                  pl.BlockSpec((None, 2, 1, pg), lambda b, g: (b, 0, 0, 2 * (D // pg) + g))],
        out_specs=seq_spec,
        out_shape=jax.ShapeDtypeStruct((B, S, D), F32),
        scratch_shapes=[pltpu.VMEM((S - m + 2 * POOL_PAD, pg), F32)],
        compiler_params=_params("arbitrary", "arbitrary"),
        name="pool_mix",
    )(h, xs, w_pool.astype(BF16), pool_scale[None], modtab)


def _ssm_in_kernel(x_ref, mod_ref, g_ref, wz_ref, wx_ref, wdt_hi_ref, wdt_lo_ref, z_ref, xbc_ref, dt_ref):
    d = x_ref.shape[-1]
    hf = _normmod(x_ref[...], g_ref[...], mod_ref[:, 0:d], mod_ref[:, d:2 * d])
    h = hf.astype(BF16)
    z_ref[...] = jnp.dot(h, wz_ref[...], preferred_element_type=F32)
    xbc_ref[...] = jnp.dot(h, wx_ref[...], preferred_element_type=F32)
    dt_ref[...] = _dot_split(hf, wdt_hi_ref[...], wdt_lo_ref[...])


def _conv_kernel(u_ref, w_ref, b_ref, o_ref, pad_ref, *, segs):
    zeros = jnp.zeros((CONV_PAD, u_ref.shape[-1]), F32)
    for r0, n in segs:
        pad_ref[0:CONV_PAD, :] = zeros
        pad_ref[CONV_PAD:CONV_PAD + n, :] = u_ref[r0:r0 + n, :]
        pad_ref[CONV_PAD + n:2 * CONV_PAD + n, :] = zeros
        for c0 in range(0, n, ROW_CHUNK):
            acc = b_ref[...]
            for k in range(SSM_CONV):
                lo = CONV_PAD + c0 - SSM_CONV // 2 + k
                acc = acc + w_ref[k:k + 1, :] * pad_ref[lo:lo + ROW_CHUNK, :]
            o_ref[r0 + c0:r0 + c0 + ROW_CHUNK, :] = _silu(acc)


def _ssd_kernel(xf_ref, dtf_ref, xb_ref, dtb_blk_ref, dtb_ref, a_ref, tri_ref, self_ref, selb_ref,
                yf_ref, yb_ref, hf_ref, hb_ref):
    @pl.when(pl.program_id(1) == 0)
    def _():
        hf_ref[...] = jnp.zeros(hf_ref.shape, F32)
        hb_ref[...] = jnp.zeros(hb_ref.shape, F32)

    _ssd_chunk(xf_ref, dtf_ref, dtb_ref, a_ref, tri_ref, self_ref, yf_ref, hf_ref, direction=0)
    _ssd_chunk(xb_ref, dtb_blk_ref, dtb_ref, a_ref, tri_ref, selb_ref, yb_ref, hb_ref, direction=1)


def _ssd_chunk(xbc_ref, dt_ref, dtb_ref, a_ref, tri_ref, sel_ref, y_ref, h_ref, *, direction):
    q = SSM_CHUNK
    lane0 = SSM_HEADS * direction
    raw = dt_ref[...] + dtb_ref[...]
    dt = jnp.maximum(raw, 0.0) + jnp.log1p(jnp.exp(-jnp.abs(raw)))
    la = dt * a_ref[...]
    cs = jnp.dot(tri_ref[...], la, precision=HI, preferred_element_type=F32)
    tot = cs[q - 1:q, :]
    if direction == 0:
        u = cs
        dec_in = jnp.exp(u)
        dec_end = jnp.exp(tot - u)
    else:
        u = la - cs
        dec_in = jnp.exp(tot + u)
        dec_end = jnp.exp(-u)
    uT = u.T
    dtT = dt.T
    stacked = jnp.concatenate([dec_in, dt * dec_end, jnp.broadcast_to(jnp.exp(tot), (8, LANES))], axis=0)
    spread = _dot_split_lhs(stacked, sel_ref[...])
    din_col = spread[0:q]
    dtot_col = spread[2 * q:2 * q + 1]
    x = xbc_ref[:, 0:SSM_D_INNER]
    x_b = x.astype(BF16)
    xd_b = (x * spread[q:2 * q]).astype(BF16)
    row = lax.broadcasted_iota(I32, (q, q), 0)
    col = lax.broadcasted_iota(I32, (q, q), 1)
    mask = (row >= col) if direction == 0 else (col >= row)
    gw = SSM_HPG * SSM_HEAD_DIM
    for g in range(SSM_GROUPS):
        b_lo = SSM_D_INNER + SSM_STATE * g
        c_lo = SSM_D_INNER + SSM_GN + SSM_STATE * g
        bT = xbc_ref[:, b_lo:b_lo + SSM_STATE].T.astype(BF16)
        cg = xbc_ref[:, c_lo:c_lo + SSM_STATE].astype(BF16)
        scores = jnp.dot(cg, bT, preferred_element_type=F32)
        hg = h_ref[g]
        cols = slice(gw * g, gw * (g + 1))
        y_off = jnp.dot(cg, hg.astype(BF16), preferred_element_type=F32) * din_col[:, cols]
        for r in range(SSM_HPG):
            hh = SSM_HPG * g + r
            hl = lane0 + hh
            seg = u[:, hl:hl + 1] - uT[hl:hl + 1, :]
            decay = jnp.exp(jnp.where(mask, seg, NEG_INF))
            hc = slice(SSM_HEAD_DIM * hh, SSM_HEAD_DIM * (hh + 1))
            mix = (scores * decay * dtT[hl:hl + 1, :]).astype(BF16)
            y_d = jnp.dot(mix, x_b[:, hc], preferred_element_type=F32)
            y_ref[:, hc] = y_d + y_off[:, SSM_HEAD_DIM * r:SSM_HEAD_DIM * (r + 1)]
        h_ref[g] = hg * dtot_col[:, cols] + jnp.dot(bT, xd_b[:, cols], preferred_element_type=F32)


def _ssm_out_kernel(y0_ref, y1_ref, xh_ref, z_ref, dsk_ref, ng_ref, w_ref, x_ref, mod_ref, out_ref):
    d = x_ref.shape[-1]
    y = y0_ref[...] + y1_ref[...] + dsk_ref[...] * xh_ref[...]
    g = y * _silu(z_ref[...])
    gsz = SSM_D_INNER // SSM_GROUPS
    parts = []
    for k in range(SSM_GROUPS):
        gk = g[:, gsz * k:gsz * (k + 1)]
        parts.append(gk * lax.rsqrt(jnp.mean(gk * gk, axis=-1, keepdims=True) + EPS))
    gn = (jnp.concatenate(parts, axis=1) * ng_ref[...]).astype(BF16)
    o = jnp.dot(gn, w_ref[...], preferred_element_type=F32)
    out_ref[...] = x_ref[...] + mod_ref[:, 2 * d:3 * d] * o


def _ssd_layer(xs, modtab, norm_g, w_in, conv_w, conv_b, dt_bias, a_log, d_skip, ssm_norm_g, w_out, m):
    B, S, D = xs.shape
    nt = S // TS
    d6 = modtab.shape[-1]
    di = SSM_D_INNER
    cd = SSM_CONV_DIM
    wz = w_in[:, :di].astype(BF16)
    wx = w_in[:, di:di + cd].astype(BF16)
    wdt_hi, wdt_lo = _split(jnp.pad(w_in[:, di + cd:], ((0, 0), (0, LANES - 2 * SSM_HEADS))))
    z, xbc_raw, dt_raw = pl.pallas_call(
        _ssm_in_kernel,
        grid=(B, nt),
        in_specs=[_row_spec(D, 0), _mod_spec(d6, 0), _full_spec((1, D)), _full_spec((D, di)),
                  _full_spec((D, cd)), _full_spec((D, LANES)), _full_spec((D, LANES))],
        out_specs=[_row_spec(di, 0), _row_spec(cd, 0), _row_spec(LANES, 0)],
        out_shape=[jax.ShapeDtypeStruct((B, S, di), F32), jax.ShapeDtypeStruct((B, S, cd), F32),
                   jax.ShapeDtypeStruct((B, S, LANES), F32)],
        compiler_params=_params("arbitrary", "arbitrary"),
        name="ssm_in",
    )(xs, modtab, norm_g[None], wz, wx, wdt_hi, wdt_lo)

    segs = ((0, m), (m, S - m))
    cw = 256
    seq_spec = pl.BlockSpec((None, S, cw), lambda b, j: (b, 0, j))
    xbc = pl.pallas_call(
        functools.partial(_conv_kernel, segs=segs),
        grid=(B, cd // cw),
        in_specs=[seq_spec, pl.BlockSpec((SSM_CONV, cw), lambda b, j: (0, j)),
                  pl.BlockSpec((1, cw), lambda b, j: (0, j))],
        out_specs=seq_spec,
        out_shape=jax.ShapeDtypeStruct((B, S, cd), F32),
        scratch_shapes=[pltpu.VMEM((S - m + 2 * CONV_PAD, cw), F32)],
        compiler_params=_params("arbitrary", "arbitrary"),
        name="ssm_conv",
    )(xbc_raw, conv_w, conv_b[None])

    q = SSM_CHUNK
    nc = S // q
    mc = m // q
    pad = LANES - 2 * SSM_HEADS
    dtb = jnp.pad(dt_bias.reshape(-1), (0, pad))[None]
    a_neg = jnp.pad(-jnp.exp(a_log.reshape(-1)), (0, pad))[None]
    tri = (jnp.arange(q)[:, None] >= jnp.arange(q)[None, :]).astype(F32)
    fmap = lambda c: c
    bmap = lambda c: jnp.where(c < mc, mc - 1 - c, nc - 1 - (c - mc))
    sels = [(jnp.arange(LANES)[:, None] == (SSM_HEADS * dr + jnp.arange(di) // SSM_HEAD_DIM)[None, :]).astype(BF16)
            for dr in range(2)]
    chunk_spec = lambda width, cmap: pl.BlockSpec((None, q, width), lambda b, c: (b, cmap(c), 0))
    state = pltpu.VMEM((SSM_GROUPS, SSM_STATE, SSM_HPG * SSM_HEAD_DIM), F32)
    ys = pl.pallas_call(
        _ssd_kernel,
        grid=(B, nc),
        in_specs=[chunk_spec(cd, fmap), chunk_spec(LANES, fmap), chunk_spec(cd, bmap), chunk_spec(LANES, bmap),
                  _full_spec((1, LANES)), _full_spec((1, LANES)), _full_spec((q, q)),
                  _full_spec((LANES, di)), _full_spec((LANES, di))],
        out_specs=[chunk_spec(di, fmap), chunk_spec(di, bmap)],
        out_shape=[jax.ShapeDtypeStruct((B, S, di), F32), jax.ShapeDtypeStruct((B, S, di), F32)],
        scratch_shapes=[state, state],
        compiler_params=_params("arbitrary", "arbitrary"),
        name="ssd_scan",
    )(xbc, dt_raw, xbc, dt_raw, dtb, a_neg, tri, sels[0], sels[1])

    dsk = jnp.repeat(d_skip[0] + d_skip[1], SSM_HEAD_DIM)[None]
    return pl.pallas_call(
        _ssm_out_kernel,
        grid=(B, nt),
        in_specs=[_row_spec(di, 0), _row_spec(di, 0), _row_spec(di, 0), _row_spec(di, 0),
                  _full_spec((1, di)), _full_spec((1, di)), _full_spec((di, D)),
                  _row_spec(D, 0), _mod_spec(d6, 0)],
        out_specs=_row_spec(D, 0),
        out_shape=jax.ShapeDtypeStruct((B, S, D), F32),
        input_output_aliases={7: 0},
        compiler_params=_params("arbitrary", "arbitrary"),
        name="ssm_out",
    )(ys[0], ys[1], xbc, z, dsk, ssm_norm_g[None], w_out.astype(BF16), xs, modtab)


def _store_slabs(ref, v):
    rows = v.shape[0]
    for s in range(SUBLANES):
        ref[pl.ds(s, rows, stride=SUBLANES), :] = v[:, LANES * s:LANES * (s + 1)]


def _load_slab_chunk(ref, rows, s, row0=0):
    return ref[pl.ds(SUBLANES * row0 + s, rows, stride=SUBLANES), :]


def _slab(ref, row):
    return ref.at[pl.ds(pl.multiple_of(row * SUBLANES, SUBLANES), SUBLANES)]


def _route(logits):
    lane = lax.broadcasted_iota(I32, logits.shape, 1).astype(F32)
    big = float(LANES)
    gl = jnp.where(lane < MOE_GROUPS, logits, NEG_INF)
    gmax = jnp.max(gl, axis=1, keepdims=True)
    gate = 1.0 / jnp.sum(jnp.exp(gl - gmax), axis=1, keepdims=True)
    grp = jnp.min(jnp.where(gl == gmax, lane, big), axis=1, keepdims=True)
    lo = MOE_GROUPS + MOE_EPG * grp
    el = jnp.where((lane >= lo) & (lane < lo + MOE_EPG), logits, NEG_INF)
    v1 = jnp.max(el, axis=1, keepdims=True)
    i1 = jnp.min(jnp.where(el == v1, lane, big), axis=1, keepdims=True)
    el2 = jnp.where(lane == i1, NEG_INF, el)
    v2 = jnp.max(el2, axis=1, keepdims=True)
    i2 = jnp.min(jnp.where(el2 == v2, lane, big), axis=1, keepdims=True)
    e2 = jnp.exp(v2 - v1)
    den = 1.0 + e2
    return i1, i2, gate * (1.0 / den), gate * (e2 / den)


def _router_kernel(x_ref, mod_ref, g_ref, wr_hi_ref, wr_lo_ref, ltri_ref, tok_ref, meta_ref, wt_ref, cnt_ref,
                   run_ref):
    d = x_ref.shape[-1]

    @pl.when((pl.program_id(0) == 0) & (pl.program_id(1) == 0))
    def _():
        run_ref[...] = jnp.zeros(run_ref.shape, F32)

    t = _normmod(x_ref[...], g_ref[...], mod_ref[:, 3 * d:4 * d], mod_ref[:, 4 * d:5 * d])
    _store_slabs(tok_ref, t)
    logits = _dot_split(t, wr_hi_ref[...], wr_lo_ref[...])
    i1, i2, w1, w2 = _route(logits)
    lane = lax.broadcasted_iota(I32, logits.shape, 1).astype(F32)
    wt_ref[...] = jnp.where(lane == 0.0, w1, jnp.where(lane == 1.0, w2, 0.0))
    oh1 = (lane == i1).astype(F32)
    oh2 = (lane == i2).astype(F32)
    oh = oh1 + oh2
    before = jnp.dot(ltri_ref[...], oh.astype(BF16), preferred_element_type=F32) + run_ref[...]
    r1 = jnp.sum(oh1 * before, axis=1, keepdims=True)
    r2 = jnp.sum(oh2 * before, axis=1, keepdims=True)
    run = run_ref[...] + jnp.sum(oh, axis=0, keepdims=True)
    run_ref[...] = run
    cnt_ref[...] = jnp.broadcast_to(run, cnt_ref.shape)
    cols = jnp.where(lane == 0.0, i1 - MOE_GROUPS,
                     jnp.where(lane == 1.0, i2 - MOE_GROUPS,
                               jnp.where(lane == 2.0, r1, jnp.where(lane == 3.0, r2, 0.0))))
    meta_ref[...] = cols.T[0:8, :].astype(I32)


def _row_copy(src_ref, src_row, dst_ref, dst_row, sem):
    return pltpu.make_async_copy(_slab(src_ref, src_row), _slab(dst_ref, dst_row), sem)


def _dispatch_kernel(dest_ref, tok_ref, init_hbm, xs_hbm, sem):
    del init_hbm

    for h in range(DISPATCH_TILES):
        def body(j, carry, h=h):
            _row_copy(tok_ref, h * TS + j, xs_hbm, dest_ref[h, 0, j], sem).start()
            _row_copy(tok_ref, h * TS + j, xs_hbm, dest_ref[h, 1, j], sem).start()
            return carry

        lax.fori_loop(0, TS, body, 0, unroll=DMA_UNROLL)
    for _ in range(2):
        pltpu.make_async_copy(tok_ref, xs_hbm.at[pl.ds(0, tok_ref.shape[0])], sem).wait()


def _expert_kernel(be_ref, nu_ref, xs_ref, wgu_ref, wd_ref, y_ref, wgu_b_ref, wd_b_ref):
    i = pl.program_id(0)
    rows = xs_ref.shape[0] // SUBLANES

    @pl.when((i == 0) | (be_ref[i] != be_ref[jnp.maximum(i - 1, 0)]))
    def _():
        wgu_b_ref[...] = wgu_ref[...].astype(BF16)
        wd_b_ref[...] = wd_ref[...].astype(BF16)

    @pl.when(i < nu_ref[0])
    def _():
        x = jnp.concatenate([_load_slab_chunk(xs_ref, rows, s) for s in range(SUBLANES)], axis=1)
        gu = jnp.dot(x.astype(BF16), wgu_b_ref[...], preferred_element_type=F32)
        hid = _silu(gu[:, :MOE_HIDDEN]) * gu[:, MOE_HIDDEN:]
        _store_slabs(y_ref, jnp.dot(hid.astype(BF16), wd_b_ref[...], preferred_element_type=F32))

    @pl.when(i >= nu_ref[0])
    def _():
        y_ref[...] = jnp.zeros(y_ref.shape, F32)


def _combine_kernel(dest_ref, next_dest_ref, x_ref, mod_ref, wt_ref, y_hbm, *rest, final_norm):
    if final_norm:
        fg_ref, out_ref, buf_ref, sem = rest
    else:
        out_ref, buf_ref, sem = rest
    d = x_ref.shape[-1]
    step = pl.program_id(0) * pl.num_programs(1) + pl.program_id(1)
    n_steps = pl.num_programs(0) * pl.num_programs(1)
    slot = step % 2
    slot_rows = 2 * TS

    def gather(d_ref, to_slot):
        base = to_slot * slot_rows

        def body(j, carry):
            _row_copy(y_hbm, d_ref[0, j], buf_ref, base + j, sem.at[to_slot]).start()
            _row_copy(y_hbm, d_ref[1, j], buf_ref, base + TS + j, sem.at[to_slot]).start()
            return carry

        lax.fori_loop(0, TS, body, 0, unroll=DMA_UNROLL)

    @pl.when(step == 0)
    def _():
        gather(dest_ref, 0)

    @pl.when(step + 1 < n_steps)
    def _():
        gather(next_dest_ref, 1 - slot)

    row0 = slot * slot_rows
    slot_view = buf_ref.at[pl.ds(pl.multiple_of(row0 * SUBLANES, SUBLANES), slot_rows * SUBLANES)]
    pltpu.make_async_copy(y_hbm.at[pl.ds(0, slot_rows * SUBLANES)], slot_view, sem.at[slot]).wait()
    w1 = wt_ref[:, 0:1]
    w2 = wt_ref[:, 1:2]
    for s in range(SUBLANES):
        cols = slice(LANES * s, LANES * (s + 1))
        f = (w1 * _load_slab_chunk(buf_ref, TS, s, row0=row0)
             + w2 * _load_slab_chunk(buf_ref, TS, s, row0=row0 + TS))
        out_ref[:, cols] = x_ref[:, cols] + mod_ref[:, 5 * d + LANES * s:5 * d + LANES * (s + 1)] * f
    if final_norm:
        v = out_ref[...]
        out_ref[...] = v * lax.rsqrt(jnp.mean(v * v, axis=-1, keepdims=True) + EPS) * fg_ref[...]


def _moe_blocks(n_tokens):
    return -(-2 * n_tokens // MOE_BLOCK) + MOE_EXPERTS


def _moe_layer(xs, modtab, norm_g, w_rg, w_re, w_gu, w_down, layer, ctx_out, sorted_init, final_g=None):
    assert final_g is None or not ctx_out
    B, S, D = xs.shape
    assert D == SUBLANES * LANES
    d6 = modtab.shape[-1]
    t0 = 0 if ctx_out else 1
    nt = S // TS - t0
    R = nt * TS
    wr_hi, wr_lo = _split(
        jnp.pad(jnp.concatenate([w_rg, w_re], axis=1), ((0, 0), (0, LANES - MOE_GROUPS - MOE_EXPERTS))))
    out_row = lambda width: pl.BlockSpec((None, TS, width), lambda b, s: (b, s, 0))
    ltri = (jnp.arange(TS)[:, None] > jnp.arange(TS)[None, :]).astype(BF16)
    tok, meta, wt, cnt = pl.pallas_call(
        _router_kernel,
        grid=(B, nt),
        in_specs=[_row_spec(D, t0), _mod_spec(d6, t0), _full_spec((1, D)), _full_spec((D, LANES)),
                  _full_spec((D, LANES)), _full_spec((TS, TS))],
        out_specs=[pl.BlockSpec((TS * SUBLANES, LANES), lambda b, s: (b * nt + s, 0)),
                   pl.BlockSpec((None, 8, TS), lambda b, s: (b * nt + s, 0, 0)), out_row(LANES),
                   _full_spec((8, LANES))],
        out_shape=[jax.ShapeDtypeStruct((B * R * SUBLANES, LANES), F32), jax.ShapeDtypeStruct((B * nt, 8, TS), I32),
                   jax.ShapeDtypeStruct((B, R, LANES), F32), jax.ShapeDtypeStruct((8, LANES), F32)],
        scratch_shapes=[pltpu.VMEM((1, LANES), F32)],
        compiler_params=_params("arbitrary", "arbitrary"),
        name="moe_router",
    )(xs, modtab, norm_g[None], wr_hi, wr_lo, ltri)

    T = B * R
    n_rows = sorted_init.shape[0] // SUBLANES
    n_blocks = n_rows // MOE_BLOCK
    assert n_blocks >= _moe_blocks(T)
    counts = cnt[0, MOE_GROUPS:MOE_GROUPS + MOE_EXPERTS].astype(I32)
    padded = (counts + MOE_BLOCK - 1) // MOE_BLOCK * MOE_BLOCK
    pends = jnp.cumsum(padded)
    pstarts = pends - padded
    blk_start = jnp.arange(n_blocks, dtype=I32) * MOE_BLOCK
    block_expert = jnp.minimum(jnp.sum(pends[None, :] <= blk_start[:, None], axis=1),
                               MOE_EXPERTS - 1).astype(I32)
    n_used = (pends[-1:] // MOE_BLOCK).astype(I32)

    hbm = pl.BlockSpec(memory_space=pltpu.MemorySpace.HBM)
    expert_start = jnp.sum(jnp.where(meta[:, 0:2, :, None] == jnp.arange(MOE_EXPERTS), pstarts, 0), axis=-1)
    dest = expert_start + meta[:, 2:4, :]
    assert (B * nt) % DISPATCH_TILES == 0
    x_sorted = pl.pallas_call(
        _dispatch_kernel,
        grid=(B * nt // DISPATCH_TILES,),
        in_specs=[pl.BlockSpec((DISPATCH_TILES, 2, TS), lambda i: (i, 0, 0), memory_space=pltpu.SMEM),
                  pl.BlockSpec((DISPATCH_TILES * TS * SUBLANES, LANES), lambda i: (i, 0)), hbm],
        out_specs=hbm,
        out_shape=jax.ShapeDtypeStruct((n_rows * SUBLANES, LANES), F32),
        scratch_shapes=[pltpu.SemaphoreType.DMA(())],
        input_output_aliases={2: 0},
        compiler_params=_params("arbitrary"),
        name="moe_dispatch",
    )(dest, tok, sorted_init)

    blk_spec = pl.BlockSpec((MOE_BLOCK * SUBLANES, LANES), lambda i, be, nu: (i, 0))
    y_sorted = pl.pallas_call(
        _expert_kernel,
        grid_spec=pltpu.PrefetchScalarGridSpec(
            num_scalar_prefetch=2,
            grid=(n_blocks,),
            in_specs=[blk_spec,
                      pl.BlockSpec((None, None, D, 2 * MOE_HIDDEN), lambda i, be, nu: (layer, be[i], 0, 0)),
                      pl.BlockSpec((None, None, MOE_HIDDEN, D), lambda i, be, nu: (layer, be[i], 0, 0))],
            out_specs=blk_spec,
            scratch_shapes=[pltpu.VMEM((D, 2 * MOE_HIDDEN), BF16), pltpu.VMEM((MOE_HIDDEN, D), BF16)],
        ),
        out_shape=jax.ShapeDtypeStruct((n_rows * SUBLANES, LANES), F32),
        compiler_params=_params("arbitrary"),
        name="moe_experts",
    )(block_expert, n_used, x_sorted, w_gu, w_down)

    in_specs = [pl.BlockSpec((None, 2, TS), lambda b, s: (b * nt + s, 0, 0), memory_space=pltpu.SMEM),
                pl.BlockSpec((None, 2, TS), lambda b, s: (jnp.minimum(b * nt + s + 1, B * nt - 1), 0, 0),
                             memory_space=pltpu.SMEM),
                _row_spec(D, t0), _mod_spec(d6, t0), out_row(LANES), hbm]
    operands = [dest, dest, xs, modtab, wt, y_sorted]
    if final_g is None:
        out_spec, out_shape, aliases = _row_spec(D, t0), jax.ShapeDtypeStruct((B, S, D), F32), {2: 0}
    else:
        in_specs.append(_full_spec((1, D)))
        operands.append(final_g[None])
        out_spec, out_shape, aliases = out_row(D), jax.ShapeDtypeStruct((B, R, D), F32), {}
    out = pl.pallas_call(
        functools.partial(_combine_kernel, final_norm=final_g is not None),
        grid=(B, nt),
        in_specs=in_specs,
        out_specs=out_spec,
        out_shape=out_shape,
        scratch_shapes=[pltpu.VMEM((2 * 2 * TS * SUBLANES, LANES), F32), pltpu.SemaphoreType.DMA((2,))],
        input_output_aliases=aliases,
        compiler_params=_params("arbitrary", "arbitrary"),
        name="moe_combine",
    )(*operands)
    return out, x_sorted


def kernel(x, c, ctx, c_ctx, w_ada, b_ada, norm_mix_g, norm_ffn_g, final_norm_g, attn_w_qkv, attn_w_o, attn_q_norm_g, attn_k_norm_g, pool_w, pool_scale, ssm_w_in, ssm_conv_w, ssm_conv_b, ssm_dt_bias, ssm_a_log, ssm_d, ssm_norm_g, ssm_w_out, moe_w_router_group, moe_w_router_expert, moe_w_gate_up, moe_w_down):
    B, n, D = x.shape
    m = ctx.shape[1]
    depth = w_ada.shape[0]
    assert m == TS and n % TS == 0 and n % GRID_W == 0
    xs = jnp.concatenate([ctx, x], axis=1)
    mods = _ada(c, c_ctx, w_ada, b_ada)
    cos_t, sin_t = _rope_tables(n, m)
    sorted_buf = jnp.zeros((_moe_blocks(B * (m + n)) * MOE_BLOCK * SUBLANES, LANES), F32)
    for i in range(depth):
        kind, j = i % N_MIXERS, i // N_MIXERS
        ctx_out = i < depth - 1
        modtab = jnp.stack([jnp.broadcast_to(mods[i, B], (B, 6 * D)), mods[i, :B]], axis=1)[:, :, None, :]
        if kind == 0:
            xs = _attention_layer(xs, modtab, norm_mix_g[i], attn_w_qkv[j], attn_w_o[j], attn_q_norm_g[j],
                                  attn_k_norm_g[j], cos_t, sin_t, ctx_out)
        elif kind == 1:
            assert ctx_out
            xs = _pool_layer(xs, modtab, norm_mix_g[i], pool_w[j], pool_scale[j], m)
        else:
            assert ctx_out
            xs = _ssd_layer(xs, modtab, norm_mix_g[i], ssm_w_in[j], ssm_conv_w[j], ssm_conv_b[j],
                            ssm_dt_bias[j], ssm_a_log[j], ssm_d[j], ssm_norm_g[j], ssm_w_out[j], m)
        xs, sorted_buf = _moe_layer(xs, modtab, norm_ffn_g[i], moe_w_router_group[i], moe_w_router_expert[i],
                                    moe_w_gate_up, moe_w_down, i, ctx_out, sorted_buf,
                                    final_g=final_norm_g if i == depth - 1 else None)
    return xs
```

```python
import functools

import jax
import jax.numpy as jnp
from jax import lax
from jax.experimental import pallas as pl
from jax.experimental.pallas import tpu as pltpu

F32 = jnp.float32
BF16 = jnp.bfloat16
I32 = jnp.int32
HI = lax.Precision.HIGHEST
EPS = 1e-6
NEG_INF = float("-inf")
LOG2E = 1.4426950408889634

TS = 256
LANES = 128
SUBLANES = 8
GRID_W = 64
ROPE_THETA = 10000.0
N_MIXERS = 3

N_HEADS = 16
N_KV = 4
HEAD_DIM = 64
Q_PER_KV = N_HEADS // N_KV

POOL_WINDOWS = (2, 4, 8, 16)
POOL_PAD = 16
ROW_CHUNK = 256

SSM_HEADS = 32
SSM_HEAD_DIM = 64
SSM_GROUPS = 4
SSM_HPG = SSM_HEADS // SSM_GROUPS
SSM_STATE = 128
SSM_CONV = 4
SSM_CHUNK = 128
SSM_D_INNER = SSM_HEADS * SSM_HEAD_DIM
SSM_GN = SSM_GROUPS * SSM_STATE
SSM_CONV_DIM = SSM_D_INNER + 2 * SSM_GN
CONV_PAD = 8

MOE_GROUPS = 4
MOE_EPG = 8
MOE_EXPERTS = MOE_GROUPS * MOE_EPG
MOE_HIDDEN = 512
MOE_BLOCK = 256
DMA_UNROLL = 8


def _params(*sem):
    return pltpu.CompilerParams(dimension_semantics=sem)


def _silu(v):
    return v / (1.0 + jnp.exp(-v))


def _split(v):
    hi = v.astype(BF16)
    return hi, (v - hi.astype(F32)).astype(BF16)


def _dot_split_lhs(a, e):
    hi, lo = _split(a)
    return jnp.dot(hi, e, preferred_element_type=F32) + jnp.dot(lo, e, preferred_element_type=F32)


def _dot_split(a, b_hi, b_lo):
    hi, lo = _split(a)
    return jnp.dot(hi, b_hi, preferred_element_type=F32) + (
        jnp.dot(hi, b_lo, preferred_element_type=F32) + jnp.dot(lo, b_hi, preferred_element_type=F32))


def _normmod(x, g, shift, scale):
    ms = jnp.mean(x * x, axis=-1, keepdims=True)
    return (x * lax.rsqrt(ms + EPS) * g) * (1.0 + scale) + shift


def _ada_kernel(a_ref, w_ref, b_ref, o_ref):
    a = _silu(a_ref[...])
    o_ref[0] = jnp.dot(a, w_ref[0], precision=HI, preferred_element_type=F32) + b_ref[0]


def _ada(c, c_ctx, w_ada, b_ada):
    depth, d, d6 = w_ada.shape
    b = c.shape[0]
    assert b + 1 <= 8
    a = jnp.concatenate([c, c_ctx[None], jnp.zeros((8 - b - 1, d), F32)], axis=0)
    tn = 1536
    return pl.pallas_call(
        _ada_kernel,
        grid=(depth, d6 // tn),
        in_specs=[pl.BlockSpec((8, d), lambda i, j: (0, 0)),
                  pl.BlockSpec((1, d, tn), lambda i, j: (i, 0, j)),
                  pl.BlockSpec((1, 1, tn), lambda i, j: (i, 0, j))],
        out_specs=pl.BlockSpec((1, 8, tn), lambda i, j: (i, 0, j)),
        out_shape=jax.ShapeDtypeStruct((depth, 8, d6), F32),
        compiler_params=_params("arbitrary", "arbitrary"),
        name="ada",
    )(a, w_ada, b_ada.reshape(depth, 1, d6))


def _row_spec(width, t0, col=0):
    return pl.BlockSpec((None, TS, width), lambda b, s: (b, s + t0, col))


def _mod_spec(d6, t0):
    return pl.BlockSpec((None, None, 1, d6), lambda b, s: (b, jnp.minimum(s + t0, 1), 0, 0))


def _full_spec(shape):
    nd = len(shape)
    return pl.BlockSpec(shape, lambda b, s: (0,) * nd)


def _qkv_kernel(x_ref, mod_ref, g_ref, wT_ref, gqk_ref, cos_ref, sin_ref, qT_ref, k_ref, vT_ref):
    d = x_ref.shape[-1]
    nq = N_HEADS * HEAD_DIM
    nqk = nq + N_KV * HEAD_DIM
    h = _normmod(x_ref[...], g_ref[...], mod_ref[:, 0:d], mod_ref[:, d:2 * d])
    t = jnp.dot(wT_ref[...], h.T.astype(BF16), preferred_element_type=F32)
    cos = cos_ref[...]
    sin = sin_ref[...]
    row = lax.broadcasted_iota(I32, (HEAD_DIM, TS), 0)
    even = (row & 1) == 0
    for hh in range(nqk // HEAD_DIM):
        rows = slice(HEAD_DIM * hh, HEAD_DIM * (hh + 1))
        blk = t[rows, :]
        rinv = lax.rsqrt(jnp.mean(blk * blk, axis=0, keepdims=True) + EPS)
        y = blk * rinv * gqk_ref[rows, :]
        partner = jnp.where(even, pltpu.roll(y, HEAD_DIM - 1, 0), pltpu.roll(y, 1, 0))
        out = y * cos + partner * sin
        if hh < N_HEADS:
            qT_ref[rows, :] = out.astype(BF16)
        else:
            k_ref[hh - N_HEADS] = out.T.astype(BF16)
    vT_ref[...] = t[nqk:, :].astype(BF16)


def _attn_ctx_kernel(qT_ref, k_ref, vT_ref, o_ref):
    for g in range(Q_PER_KV):
        rows = slice(HEAD_DIM * g, HEAD_DIM * (g + 1))
        s = jnp.dot(k_ref[...], qT_ref[rows, :], preferred_element_type=F32)
        p = jnp.exp2(s - jnp.max(s, axis=0, keepdims=True))
        l = jnp.sum(p, axis=0, keepdims=True)
        o = jnp.dot(vT_ref[...], p.astype(BF16), preferred_element_type=F32)
        o_ref[rows, :] = (o / l).astype(BF16)


def _attn_main_kernel(qT_ref, k_ref, vT_ref, o_ref, s0_ref, s1_ref, m0_ref, m1_ref, *, n_units, q_col0):
    for ref in (s0_ref, s1_ref, m0_ref, m1_ref):
        ref[...] = jnp.zeros(ref.shape, F32)

    def unit(i):
        i = jnp.clip(i, 0, n_units - 1)
        row = pl.multiple_of((i % Q_PER_KV) * HEAD_DIM, HEAD_DIM)
        col = pl.multiple_of((i // Q_PER_KV) * TS, TS)
        return row, col

    def step(i, s_w, m_w, s_r, m_r):
        row, col = unit(i)
        q = qT_ref[pl.ds(row, HEAD_DIM), pl.ds(q_col0 + col, TS)]
        s = jnp.dot(k_ref[...], q, preferred_element_type=F32)
        s_w[...] = s
        m_w[...] = jnp.max(s, axis=0, keepdims=True)
        row, col = unit(i - 1)
        p = jnp.exp2(s_r[...] - m_r[...])
        l = jnp.sum(p, axis=0, keepdims=True)
        o = jnp.dot(vT_ref[...], p.astype(BF16), preferred_element_type=F32)
        o_ref[pl.ds(row, HEAD_DIM), pl.ds(col, TS)] = (o / l).astype(BF16)

    def body(j, carry):
        step(2 * j, s0_ref, m0_ref, s1_ref, m1_ref)
        step(2 * j + 1, s1_ref, m1_ref, s0_ref, m0_ref)
        return carry

    lax.fori_loop(0, n_units // 2 + 1, body, 0)


def _oproj_kernel(oT_ref, w_ref, x_ref, mod_ref, out_ref):
    d = x_ref.shape[-1]
    y = lax.dot_general(oT_ref[...], w_ref[...], (((0,), (0,)), ((), ())), preferred_element_type=F32)
    out_ref[...] = x_ref[...] + mod_ref[:, 2 * d:3 * d] * y


def _rope_tables(n, m):
    rows = n // GRID_W
    row = jnp.broadcast_to(jnp.arange(rows)[:, None], (rows, GRID_W)).reshape(-1).astype(F32)
    col = jnp.broadcast_to(jnp.arange(GRID_W)[None, :], (rows, GRID_W)).reshape(-1).astype(F32)
    n_freq = HEAD_DIM // 4
    inv_freq = ROPE_THETA ** (-jnp.arange(n_freq, dtype=F32) / n_freq)
    ang = jnp.concatenate([row[:, None] * inv_freq, col[:, None] * inv_freq], axis=-1)
    cos = jnp.repeat(jnp.cos(ang), 2, axis=1)
    sign = jnp.tile(jnp.array([-1.0, 1.0], F32), HEAD_DIM // 2)
    sin = jnp.repeat(jnp.sin(ang), 2, axis=1) * sign
    cos = jnp.concatenate([jnp.ones((m, HEAD_DIM), F32), cos], axis=0)
    sin = jnp.concatenate([jnp.zeros((m, HEAD_DIM), F32), sin], axis=0)
    return cos.T, sin.T


def _attention_layer(xs, modtab, norm_g, w_qkv, w_o, q_g, k_g, cos_t, sin_t, ctx_out):
    B, S, D = xs.shape
    nt = S // TS
    nq = N_HEADS * HEAD_DIM
    nkv = N_KV * HEAD_DIM
    nqk = nq + nkv
    d6 = modtab.shape[-1]
    q_scale = HEAD_DIM ** -0.5 * LOG2E
    gqk = jnp.concatenate([jnp.tile(q_g, N_HEADS) * q_scale, jnp.tile(k_g, N_KV)])
    gqk = jnp.broadcast_to(gqk[:, None], (nqk, TS))
    qT, k4, vT = pl.pallas_call(
        _qkv_kernel,
        grid=(B, nt),
        in_specs=[_row_spec(D, 0), _mod_spec(d6, 0), _full_spec((1, D)), _full_spec((nqk + nkv, D)),
                  _full_spec((nqk, TS)),
                  pl.BlockSpec((HEAD_DIM, TS), lambda b, s: (0, s)),
                  pl.BlockSpec((HEAD_DIM, TS), lambda b, s: (0, s))],
        out_specs=[pl.BlockSpec((None, nq, TS), lambda b, s: (b, 0, s)),
                   pl.BlockSpec((None, N_KV, TS, HEAD_DIM), lambda b, s: (b, 0, s, 0)),
                   pl.BlockSpec((None, nkv, TS), lambda b, s: (b, 0, s))],
        out_shape=[jax.ShapeDtypeStruct((B, nq, S), BF16),
                   jax.ShapeDtypeStruct((B, N_KV, S, HEAD_DIM), BF16),
                   jax.ShapeDtypeStruct((B, nkv, S), BF16)],
        compiler_params=_params("arbitrary", "arbitrary"),
        name="attn_qkv",
    )(xs, modtab, norm_g[None], w_qkv.T.astype(BF16), gqk, cos_t, sin_t)

    gw = Q_PER_KV * HEAD_DIM
    n_lat = S - TS
    w_o = w_o.astype(BF16)
    if ctx_out:
        oT_ctx = pl.pallas_call(
            _attn_ctx_kernel,
            grid=(B, N_KV),
            in_specs=[pl.BlockSpec((None, gw, TS), lambda b, kv: (b, kv, 0)),
                      pl.BlockSpec((None, None, TS, HEAD_DIM), lambda b, kv: (b, kv, 0, 0)),
                      pl.BlockSpec((None, HEAD_DIM, TS), lambda b, kv: (b, kv, 0))],
            out_specs=pl.BlockSpec((None, gw, TS), lambda b, kv: (b, kv, 0)),
            out_shape=jax.ShapeDtypeStruct((B, nq, TS), BF16),
            compiler_params=_params("arbitrary", "arbitrary"),
            name="attn_ctx",
        )(qT, k4, vT)
        xs = pl.pallas_call(
            _oproj_kernel,
            grid=(B, 1),
            in_specs=[pl.BlockSpec((None, nq, TS), lambda b, s: (b, 0, 0)), _full_spec((nq, D)),
                      _row_spec(D, 0), _mod_spec(d6, 0)],
            out_specs=_row_spec(D, 0),
            out_shape=jax.ShapeDtypeStruct((B, S, D), F32),
            input_output_aliases={2: 0},
            compiler_params=_params("arbitrary", "arbitrary"),
            name="attn_oproj_ctx",
        )(oT_ctx, w_o, xs, modtab)

    n_units = (n_lat // TS) * Q_PER_KV
    assert n_units % 2 == 0
    oT = pl.pallas_call(
        functools.partial(_attn_main_kernel, n_units=n_units, q_col0=TS),
        grid=(B, N_KV),
        in_specs=[pl.BlockSpec((None, gw, S), lambda b, kv: (b, kv, 0)),
                  pl.BlockSpec((None, None, S, HEAD_DIM), lambda b, kv: (b, kv, 0, 0)),
                  pl.BlockSpec((None, HEAD_DIM, S), lambda b, kv: (b, kv, 0))],
        out_specs=pl.BlockSpec((None, gw, n_lat), lambda b, kv: (b, kv, 0)),
        out_shape=jax.ShapeDtypeStruct((B, nq, n_lat), BF16),
        scratch_shapes=[pltpu.VMEM((S, TS), F32), pltpu.VMEM((S, TS), F32),
                        pltpu.VMEM((1, TS), F32), pltpu.VMEM((1, TS), F32)],
        compiler_params=_params("arbitrary", "arbitrary"),
        name="attn_core",
    )(qT, k4, vT)

    return pl.pallas_call(
        _oproj_kernel,
        grid=(B, n_lat // TS),
        in_specs=[pl.BlockSpec((None, nq, TS), lambda b, s: (b, 0, s)), _full_spec((nq, D)),
                  _row_spec(D, 1), _mod_spec(d6, 1)],
        out_specs=_row_spec(D, 1),
        out_shape=jax.ShapeDtypeStruct((B, S, D), F32),
        input_output_aliases={2: 0},
        compiler_params=_params("arbitrary", "arbitrary"),
        name="attn_oproj",
    )(oT, w_o, xs, modtab)


def _normmod_kernel(x_ref, mod_ref, g_ref, h_ref):
    d = x_ref.shape[-1]
    h_ref[...] = _normmod(x_ref[...], g_ref[...], mod_ref[:, 0:d], mod_ref[:, d:2 * d])


def _pool_kernel(h_ref, x_ref, w_ref, ps_ref, gate_ref, out_ref, pad_ref, *, segs):
    gi = pl.program_id(1)
    zeros = jnp.zeros((POOL_PAD, h_ref.shape[-1]), F32)
    for widx, win in enumerate(POOL_WINDOWS):
        half = win // 2

        @pl.when(gi == widx)
        def _(half=half):
            for si, (r0, n) in enumerate(segs):
                pad_ref[0:POOL_PAD, :] = zeros
                pad_ref[POOL_PAD:POOL_PAD + n, :] = h_ref[r0:r0 + n, :]
                pad_ref[POOL_PAD + n:2 * POOL_PAD + n, :] = zeros
                gate = gate_ref[si]
                for c0 in range(0, n, ROW_CHUNK):
                    base = POOL_PAD + c0
                    acc = pad_ref[base - half:base - half + ROW_CHUNK, :]
                    for j in range(-half + 1, half):
                        acc = acc + pad_ref[base + j:base + j + ROW_CHUNK, :]
                    t = c0 + lax.broadcasted_iota(I32, (ROW_CHUNK, 1), 0)
                    cnt = jnp.minimum(t + half, n) - jnp.maximum(t - half, 0)
                    diff = acc / cnt.astype(F32) - pad_ref[base:base + ROW_CHUNK, :]
                    y = jnp.dot(diff.astype(BF16), w_ref[...], preferred_element_type=F32) * ps_ref[...]
                    rows = slice(r0 + c0, r0 + c0 + ROW_CHUNK)
                    out_ref[rows, :] = x_ref[rows, :] + gate * y


def _pool_layer(xs, modtab, norm_g, w_pool, pool_scale, m):
    B, S, D = xs.shape
    nt = S // TS
    d6 = modtab.shape[-1]
    pg = D // len(POOL_WINDOWS)
    h = pl.pallas_call(
        _normmod_kernel,
        grid=(B, nt),
        in_specs=[_row_spec(D, 0), _mod_spec(d6, 0), _full_spec((1, D))],
        out_specs=_row_spec(D, 0),
        out_shape=jax.ShapeDtypeStruct((B, S, D), F32),
        compiler_params=_params("arbitrary", "arbitrary"),
        name="pool_normmod",
    )(xs, modtab, norm_g[None])
    segs = ((0, m), (m, S - m))
    seq_spec = pl.BlockSpec((None, S, pg), lambda b, g: (b, 0, g))
    return pl.pallas_call(
        functools.partial(_pool_kernel, segs=segs),
        grid=(B, len(POOL_WINDOWS)),
        in_specs=[seq_spec, seq_spec,
                  pl.BlockSpec((None, pg, pg), lambda b, g: (g, 0, 0)),
                  pl.BlockSpec((1, pg), lambda b, g: (0, g)),
                  pl.BlockSpec((None, 2, 1, pg), lambda b, g: (b, 0, 0, 2 * (D // pg) + g))],
        out_specs=seq_spec,
        out_shape=jax.ShapeDtypeStruct((B, S, D), F32),
        scratch_shapes=[pltpu.VMEM((S - m + 2 * POOL_PAD, pg), F32)],
        compiler_params=_params("arbitrary", "arbitrary"),
        name="pool_mix",
    )(h, xs, w_pool.astype(BF16), pool_scale[None], modtab)


def _ssm_in_kernel(x_ref, mod_ref, g_ref, wz_ref, wx_ref, wdt_hi_ref, wdt_lo_ref, z_ref, xbc_ref, dt_ref):
    d = x_ref.shape[-1]
    hf = _normmod(x_ref[...], g_ref[...], mod_ref[:, 0:d], mod_ref[:, d:2 * d])
    h = hf.astype(BF16)
    z_ref[...] = jnp.dot(h, wz_ref[...], preferred_element_type=F32)
    xbc_ref[...] = jnp.dot(h, wx_ref[...], preferred_element_type=F32)
    dt_ref[...] = _dot_split(hf, wdt_hi_ref[...], wdt_lo_ref[...])


def _conv_kernel(u_ref, w_ref, b_ref, o_ref, pad_ref, *, segs):
    zeros = jnp.zeros((CONV_PAD, u_ref.shape[-1]), F32)
    for r0, n in segs:
        pad_ref[0:CONV_PAD, :] = zeros
        pad_ref[CONV_PAD:CONV_PAD + n, :] = u_ref[r0:r0 + n, :]
        pad_ref[CONV_PAD + n:2 * CONV_PAD + n, :] = zeros
        for c0 in range(0, n, ROW_CHUNK):
            acc = b_ref[...]
            for k in range(SSM_CONV):
                lo = CONV_PAD + c0 - SSM_CONV // 2 + k
                acc = acc + w_ref[k:k + 1, :] * pad_ref[lo:lo + ROW_CHUNK, :]
            o_ref[r0 + c0:r0 + c0 + ROW_CHUNK, :] = _silu(acc)


def _ssd_kernel(xf_ref, dtf_ref, xb_ref, dtb_blk_ref, dtb_ref, a_ref, tri_ref, self_ref, selb_ref,
                yf_ref, yb_ref, hf_ref, hb_ref):
    @pl.when(pl.program_id(1) == 0)
    def _():
        hf_ref[...] = jnp.zeros(hf_ref.shape, F32)
        hb_ref[...] = jnp.zeros(hb_ref.shape, F32)

    _ssd_chunk(xf_ref, dtf_ref, dtb_ref, a_ref, tri_ref, self_ref, yf_ref, hf_ref, direction=0)
    _ssd_chunk(xb_ref, dtb_blk_ref, dtb_ref, a_ref, tri_ref, selb_ref, yb_ref, hb_ref, direction=1)


def _ssd_chunk(xbc_ref, dt_ref, dtb_ref, a_ref, tri_ref, sel_ref, y_ref, h_ref, *, direction):
    q = SSM_CHUNK
    lane0 = SSM_HEADS * direction
    raw = dt_ref[...] + dtb_ref[...]
    dt = jnp.maximum(raw, 0.0) + jnp.log1p(jnp.exp(-jnp.abs(raw)))
    la = dt * a_ref[...]
    cs = jnp.dot(tri_ref[...], la, precision=HI, preferred_element_type=F32)
    tot = cs[q - 1:q, :]
    if direction == 0:
        u = cs
        dec_in = jnp.exp(u)
        dec_end = jnp.exp(tot - u)
    else:
        u = la - cs
        dec_in = jnp.exp(tot + u)
        dec_end = jnp.exp(-u)
    uT = u.T
    dtT = dt.T
    stacked = jnp.concatenate([dec_in, dt * dec_end, jnp.broadcast_to(jnp.exp(tot), (8, LANES))], axis=0)
    spread = _dot_split_lhs(stacked, sel_ref[...])
    din_col = spread[0:q]
    dtot_col = spread[2 * q:2 * q + 1]
    x = xbc_ref[:, 0:SSM_D_INNER]
    x_b = x.astype(BF16)
    xd_b = (x * spread[q:2 * q]).astype(BF16)
    row = lax.broadcasted_iota(I32, (q, q), 0)
    col = lax.broadcasted_iota(I32, (q, q), 1)
    mask = (row >= col) if direction == 0 else (col >= row)
    gw = SSM_HPG * SSM_HEAD_DIM
    for g in range(SSM_GROUPS):
        b_lo = SSM_D_INNER + SSM_STATE * g
        c_lo = SSM_D_INNER + SSM_GN + SSM_STATE * g
        bT = xbc_ref[:, b_lo:b_lo + SSM_STATE].T.astype(BF16)
        cg = xbc_ref[:, c_lo:c_lo + SSM_STATE].astype(BF16)
        scores = jnp.dot(cg, bT, preferred_element_type=F32)
        hg = h_ref[g]
        cols = slice(gw * g, gw * (g + 1))
        y_off = jnp.dot(cg, hg.astype(BF16), preferred_element_type=F32) * din_col[:, cols]
        for r in range(SSM_HPG):
            hh = SSM_HPG * g + r
            hl = lane0 + hh
            seg = u[:, hl:hl + 1] - uT[hl:hl + 1, :]
            decay = jnp.exp(jnp.where(mask, seg, NEG_INF))
            hc = slice(SSM_HEAD_DIM * hh, SSM_HEAD_DIM * (hh + 1))
            mix = (scores * decay * dtT[hl:hl + 1, :]).astype(BF16)
            y_d = jnp.dot(mix, x_b[:, hc], preferred_element_type=F32)
            y_ref[:, hc] = y_d + y_off[:, SSM_HEAD_DIM * r:SSM_HEAD_DIM * (r + 1)]
        h_ref[g] = hg * dtot_col[:, cols] + jnp.dot(bT, xd_b[:, cols], preferred_element_type=F32)


def _ssm_out_kernel(y0_ref, y1_ref, xh_ref, z_ref, dsk_ref, ng_ref, w_ref, x_ref, mod_ref, out_ref):
    d = x_ref.shape[-1]
    y = y0_ref[...] + y1_ref[...] + dsk_ref[...] * xh_ref[...]
    g = y * _silu(z_ref[...])
    gsz = SSM_D_INNER // SSM_GROUPS
    parts = []
    for k in range(SSM_GROUPS):
        gk = g[:, gsz * k:gsz * (k + 1)]
        parts.append(gk * lax.rsqrt(jnp.mean(gk * gk, axis=-1, keepdims=True) + EPS))
    gn = (jnp.concatenate(parts, axis=1) * ng_ref[...]).astype(BF16)
    o = jnp.dot(gn, w_ref[...], preferred_element_type=F32)
    out_ref[...] = x_ref[...] + mod_ref[:, 2 * d:3 * d] * o


def _ssd_layer(xs, modtab, norm_g, w_in, conv_w, conv_b, dt_bias, a_log, d_skip, ssm_norm_g, w_out, m):
    B, S, D = xs.shape
    nt = S // TS
    d6 = modtab.shape[-1]
    di = SSM_D_INNER
    cd = SSM_CONV_DIM
    wz = w_in[:, :di].astype(BF16)
    wx = w_in[:, di:di + cd].astype(BF16)
    wdt_hi, wdt_lo = _split(jnp.pad(w_in[:, di + cd:], ((0, 0), (0, LANES - 2 * SSM_HEADS))))
    z, xbc_raw, dt_raw = pl.pallas_call(
        _ssm_in_kernel,
        grid=(B, nt),
        in_specs=[_row_spec(D, 0), _mod_spec(d6, 0), _full_spec((1, D)), _full_spec((D, di)),
                  _full_spec((D, cd)), _full_spec((D, LANES)), _full_spec((D, LANES))],
        out_specs=[_row_spec(di, 0), _row_spec(cd, 0), _row_spec(LANES, 0)],
        out_shape=[jax.ShapeDtypeStruct((B, S, di), F32), jax.ShapeDtypeStruct((B, S, cd), F32),
                   jax.ShapeDtypeStruct((B, S, LANES), F32)],
        compiler_params=_params("arbitrary", "arbitrary"),
        name="ssm_in",
    )(xs, modtab, norm_g[None], wz, wx, wdt_hi, wdt_lo)

    segs = ((0, m), (m, S - m))
    cw = 256
    seq_spec = pl.BlockSpec((None, S, cw), lambda b, j: (b, 0, j))
    xbc = pl.pallas_call(
        functools.partial(_conv_kernel, segs=segs),
        grid=(B, cd // cw),
        in_specs=[seq_spec, pl.BlockSpec((SSM_CONV, cw), lambda b, j: (0, j)),
                  pl.BlockSpec((1, cw), lambda b, j: (0, j))],
        out_specs=seq_spec,
        out_shape=jax.ShapeDtypeStruct((B, S, cd), F32),
        scratch_shapes=[pltpu.VMEM((S - m + 2 * CONV_PAD, cw), F32)],
        compiler_params=_params("arbitrary", "arbitrary"),
        name="ssm_conv",
    )(xbc_raw, conv_w, conv_b[None])

    q = SSM_CHUNK
    nc = S // q
    mc = m // q
    pad = LANES - 2 * SSM_HEADS
    dtb = jnp.pad(dt_bias.reshape(-1), (0, pad))[None]
    a_neg = jnp.pad(-jnp.exp(a_log.reshape(-1)), (0, pad))[None]
    tri = (jnp.arange(q)[:, None] >= jnp.arange(q)[None, :]).astype(F32)
    fmap = lambda c: c
    bmap = lambda c: jnp.where(c < mc, mc - 1 - c, nc - 1 - (c - mc))
    sels = [(jnp.arange(LANES)[:, None] == (SSM_HEADS * dr + jnp.arange(di) // SSM_HEAD_DIM)[None, :]).astype(BF16)
            for dr in range(2)]
    chunk_spec = lambda width, cmap: pl.BlockSpec((None, q, width), lambda b, c: (b, cmap(c), 0))
    state = pltpu.VMEM((SSM_GROUPS, SSM_STATE, SSM_HPG * SSM_HEAD_DIM), F32)
    ys = pl.pallas_call(
        _ssd_kernel,
        grid=(B, nc),
        in_specs=[chunk_spec(cd, fmap), chunk_spec(LANES, fmap), chunk_spec(cd, bmap), chunk_spec(LANES, bmap),
                  _full_spec((1, LANES)), _full_spec((1, LANES)), _full_spec((q, q)),
                  _full_spec((LANES, di)), _full_spec((LANES, di))],
        out_specs=[chunk_spec(di, fmap), chunk_spec(di, bmap)],
        out_shape=[jax.ShapeDtypeStruct((B, S, di), F32), jax.ShapeDtypeStruct((B, S, di), F32)],
        scratch_shapes=[state, state],
        compiler_params=_params("arbitrary", "arbitrary"),
        name="ssd_scan",
    )(xbc, dt_raw, xbc, dt_raw, dtb, a_neg, tri, sels[0], sels[1])

    dsk = jnp.repeat(d_skip[0] + d_skip[1], SSM_HEAD_DIM)[None]
    return pl.pallas_call(
        _ssm_out_kernel,
        grid=(B, nt),
        in_specs=[_row_spec(di, 0), _row_spec(di, 0), _row_spec(di, 0), _row_spec(di, 0),
                  _full_spec((1, di)), _full_spec((1, di)), _full_spec((di, D)),
                  _row_spec(D, 0), _mod_spec(d6, 0)],
        out_specs=_row_spec(D, 0),
        out_shape=jax.ShapeDtypeStruct((B, S, D), F32),
        input_output_aliases={7: 0},
        compiler_params=_params("arbitrary", "arbitrary"),
        name="ssm_out",
    )(ys[0], ys[1], xbc, z, dsk, ssm_norm_g[None], w_out.astype(BF16), xs, modtab)


def _store_slabs(ref, v):
    rows = v.shape[0]
    for s in range(SUBLANES):
        ref[pl.ds(s, rows, stride=SUBLANES), :] = v[:, LANES * s:LANES * (s + 1)]


def _load_slab_chunk(ref, rows, s, row0=0):
    return ref[pl.ds(SUBLANES * row0 + s, rows, stride=SUBLANES), :]


def _slab(ref, row):
    return ref.at[pl.ds(pl.multiple_of(row * SUBLANES, SUBLANES), SUBLANES)]


def _route(logits):
    lane = lax.broadcasted_iota(I32, logits.shape, 1).astype(F32)
    big = float(LANES)
    gl = jnp.where(lane < MOE_GROUPS, logits, NEG_INF)
    gmax = jnp.max(gl, axis=1, keepdims=True)
    gate = 1.0 / jnp.sum(jnp.exp(gl - gmax), axis=1, keepdims=True)
    grp = jnp.min(jnp.where(gl == gmax, lane, big), axis=1, keepdims=True)
    lo = MOE_GROUPS + MOE_EPG * grp
    el = jnp.where((lane >= lo) & (lane < lo + MOE_EPG), logits, NEG_INF)
    v1 = jnp.max(el, axis=1, keepdims=True)
    i1 = jnp.min(jnp.where(el == v1, lane, big), axis=1, keepdims=True)
    el2 = jnp.where(lane == i1, NEG_INF, el)
    v2 = jnp.max(el2, axis=1, keepdims=True)
    i2 = jnp.min(jnp.where(el2 == v2, lane, big), axis=1, keepdims=True)
    e2 = jnp.exp(v2 - v1)
    den = 1.0 + e2
    return i1, i2, gate * (1.0 / den), gate * (e2 / den)


def _router_kernel(x_ref, mod_ref, g_ref, wr_hi_ref, wr_lo_ref, ltri_ref, tok_ref, meta_ref, wt_ref, cnt_ref,
                   run_ref):
    d = x_ref.shape[-1]

    @pl.when((pl.program_id(0) == 0) & (pl.program_id(1) == 0))
    def _():
        run_ref[...] = jnp.zeros(run_ref.shape, F32)

    t = _normmod(x_ref[...], g_ref[...], mod_ref[:, 3 * d:4 * d], mod_ref[:, 4 * d:5 * d])
    _store_slabs(tok_ref, t)
    logits = _dot_split(t, wr_hi_ref[...], wr_lo_ref[...])
    i1, i2, w1, w2 = _route(logits)
    lane = lax.broadcasted_iota(I32, logits.shape, 1).astype(F32)
    wt_ref[...] = jnp.where(lane == 0.0, w1, jnp.where(lane == 1.0, w2, 0.0))
    oh1 = (lane == i1).astype(F32)
    oh2 = (lane == i2).astype(F32)
    oh = oh1 + oh2
    before = jnp.dot(ltri_ref[...], oh.astype(BF16), preferred_element_type=F32) + run_ref[...]
    r1 = jnp.sum(oh1 * before, axis=1, keepdims=True)
    r2 = jnp.sum(oh2 * before, axis=1, keepdims=True)
    run = run_ref[...] + jnp.sum(oh, axis=0, keepdims=True)
    run_ref[...] = run
    cnt_ref[...] = jnp.broadcast_to(run, cnt_ref.shape)
    cols = jnp.where(lane == 0.0, i1 - MOE_GROUPS,
                     jnp.where(lane == 1.0, i2 - MOE_GROUPS,
                               jnp.where(lane == 2.0, r1, jnp.where(lane == 3.0, r2, 0.0))))
    meta_ref[...] = cols.T[0:8, :].astype(I32)


def _row_copy(src_ref, src_row, dst_ref, dst_row, sem):
    return pltpu.make_async_copy(_slab(src_ref, src_row), _slab(dst_ref, dst_row), sem)


def _dispatch_kernel(dest_ref, tok_ref, init_hbm, xs_hbm, sem):
    del init_hbm

    for h in range(dest_ref.shape[0]):
        def body(j, carry, h=h):
            _row_copy(tok_ref, h * TS + j, xs_hbm, dest_ref[h, 0, j], sem).start()
            _row_copy(tok_ref, h * TS + j, xs_hbm, dest_ref[h, 1, j], sem).start()
            return carry

        lax.fori_loop(0, TS, body, 0, unroll=DMA_UNROLL)
    tile_rows = TS * SUBLANES
    for _ in range(2 * dest_ref.shape[0]):
        pltpu.make_async_copy(tok_ref.at[pl.ds(0, tile_rows)], xs_hbm.at[pl.ds(0, tile_rows)], sem).wait()


def _expert_kernel(be_ref, nu_ref, xs_ref, wgu_ref, wd_ref, y_ref, wgu_b_ref, wd_b_ref):
    i = pl.program_id(0)
    rows = xs_ref.shape[0] // SUBLANES

    @pl.when((i == 0) | (be_ref[i] != be_ref[jnp.maximum(i - 1, 0)]))
    def _():
        wgu_b_ref[...] = wgu_ref[...].astype(BF16)
        wd_b_ref[...] = wd_ref[...].astype(BF16)

    @pl.when(i < nu_ref[0])
    def _():
        x = jnp.concatenate([_load_slab_chunk(xs_ref, rows, s) for s in range(SUBLANES)], axis=1)
        gu = jnp.dot(x.astype(BF16), wgu_b_ref[...], preferred_element_type=F32)
        hid = _silu(gu[:, :MOE_HIDDEN]) * gu[:, MOE_HIDDEN:]
        _store_slabs(y_ref, jnp.dot(hid.astype(BF16), wd_b_ref[...], preferred_element_type=F32))

    @pl.when(i >= nu_ref[0])
    def _():
        y_ref[...] = jnp.zeros(y_ref.shape, F32)


def _combine_kernel(dest_ref, next_dest_ref, x_ref, mod_ref, wt_ref, y_hbm, *rest, final_norm):
    if final_norm:
        fg_ref, out_ref, buf_ref, sem = rest
    else:
        out_ref, buf_ref, sem = rest
    d = x_ref.shape[-1]
    step = pl.program_id(0) * pl.num_programs(1) + pl.program_id(1)
    n_steps = pl.num_programs(0) * pl.num_programs(1)
    slot = step % 2
    slot_rows = 2 * TS

    def gather(d_ref, to_slot):
        base = to_slot * slot_rows

        def body(j, carry):
            _row_copy(y_hbm, d_ref[0, j], buf_ref, base + j, sem.at[to_slot]).start()
            _row_copy(y_hbm, d_ref[1, j], buf_ref, base + TS + j, sem.at[to_slot]).start()
            return carry

        lax.fori_loop(0, TS, body, 0, unroll=DMA_UNROLL)

    @pl.when(step == 0)
    def _():
        gather(dest_ref, 0)

    @pl.when(step + 1 < n_steps)
    def _():
        gather(next_dest_ref, 1 - slot)

    row0 = slot * slot_rows
    slot_view = buf_ref.at[pl.ds(pl.multiple_of(row0 * SUBLANES, SUBLANES), slot_rows * SUBLANES)]
    pltpu.make_async_copy(y_hbm.at[pl.ds(0, slot_rows * SUBLANES)], slot_view, sem.at[slot]).wait()
    w1 = wt_ref[:, 0:1]
    w2 = wt_ref[:, 1:2]
    for s in range(SUBLANES):
        cols = slice(LANES * s, LANES * (s + 1))
        f = (w1 * _load_slab_chunk(buf_ref, TS, s, row0=row0)
             + w2 * _load_slab_chunk(buf_ref, TS, s, row0=row0 + TS))
        out_ref[:, cols] = x_ref[:, cols] + mod_ref[:, 5 * d + LANES * s:5 * d + LANES * (s + 1)] * f
    if final_norm:
        v = out_ref[...]
        out_ref[...] = v * lax.rsqrt(jnp.mean(v * v, axis=-1, keepdims=True) + EPS) * fg_ref[...]


def _moe_blocks(n_tokens):
    return -(-2 * n_tokens // MOE_BLOCK) + MOE_EXPERTS


def _moe_layer(xs, modtab, norm_g, w_rg, w_re, w_gu, w_down, layer, ctx_out, sorted_init, final_g=None):
    assert final_g is None or not ctx_out
    B, S, D = xs.shape
    assert D == SUBLANES * LANES
    d6 = modtab.shape[-1]
    t0 = 0 if ctx_out else 1
    nt = S // TS - t0
    R = nt * TS
    wr_hi, wr_lo = _split(
        jnp.pad(jnp.concatenate([w_rg, w_re], axis=1), ((0, 0), (0, LANES - MOE_GROUPS - MOE_EXPERTS))))
    out_row = lambda width: pl.BlockSpec((None, TS, width), lambda b, s: (b, s, 0))
    ltri = (jnp.arange(TS)[:, None] > jnp.arange(TS)[None, :]).astype(BF16)
    tok, meta, wt, cnt = pl.pallas_call(
        _router_kernel,
        grid=(B, nt),
        in_specs=[_row_spec(D, t0), _mod_spec(d6, t0), _full_spec((1, D)), _full_spec((D, LANES)),
                  _full_spec((D, LANES)), _full_spec((TS, TS))],
        out_specs=[pl.BlockSpec((TS * SUBLANES, LANES), lambda b, s: (b * nt + s, 0)),
                   pl.BlockSpec((None, 8, TS), lambda b, s: (b * nt + s, 0, 0)), out_row(LANES),
                   _full_spec((8, LANES))],
        out_shape=[jax.ShapeDtypeStruct((B * R * SUBLANES, LANES), F32), jax.ShapeDtypeStruct((B * nt, 8, TS), I32),
                   jax.ShapeDtypeStruct((B, R, LANES), F32), jax.ShapeDtypeStruct((8, LANES), F32)],
        scratch_shapes=[pltpu.VMEM((1, LANES), F32)],
        compiler_params=_params("arbitrary", "arbitrary"),
        name="moe_router",
    )(xs, modtab, norm_g[None], wr_hi, wr_lo, ltri)

    T = B * R
    n_rows = sorted_init.shape[0] // SUBLANES
    n_blocks = n_rows // MOE_BLOCK
    assert n_blocks >= _moe_blocks(T)
    counts = cnt[0, MOE_GROUPS:MOE_GROUPS + MOE_EXPERTS].astype(I32)
    padded = (counts + MOE_BLOCK - 1) // MOE_BLOCK * MOE_BLOCK
    pends = jnp.cumsum(padded)
    pstarts = pends - padded
    blk_start = jnp.arange(n_blocks, dtype=I32) * MOE_BLOCK
    block_expert = jnp.minimum(jnp.sum(pends[None, :] <= blk_start[:, None], axis=1),
                               MOE_EXPERTS - 1).astype(I32)
    n_used = (pends[-1:] // MOE_BLOCK).astype(I32)

    hbm = pl.BlockSpec(memory_space=pltpu.MemorySpace.HBM)
    expert_start = jnp.sum(jnp.where(meta[:, 0:2, :, None] == jnp.arange(MOE_EXPERTS), pstarts, 0), axis=-1)
    dest = expert_start + meta[:, 2:4, :]
    x_sorted = pl.pallas_call(
        _dispatch_kernel,
        grid=(B,),
        in_specs=[pl.BlockSpec((nt, 2, TS), lambda i: (i, 0, 0), memory_space=pltpu.SMEM),
                  pl.BlockSpec((nt * TS * SUBLANES, LANES), lambda i: (i, 0)), hbm],
        out_specs=hbm,
        out_shape=jax.ShapeDtypeStruct((n_rows * SUBLANES, LANES), F32),
        scratch_shapes=[pltpu.SemaphoreType.DMA(())],
        input_output_aliases={2: 0},
        compiler_params=_params("arbitrary"),
        name="moe_dispatch",
    )(dest, tok, sorted_init)

    blk_spec = pl.BlockSpec((MOE_BLOCK * SUBLANES, LANES), lambda i, be, nu: (i, 0))
    y_sorted = pl.pallas_call(
        _expert_kernel,
        grid_spec=pltpu.PrefetchScalarGridSpec(
            num_scalar_prefetch=2,
            grid=(n_blocks,),
            in_specs=[blk_spec,
                      pl.BlockSpec((None, None, D, 2 * MOE_HIDDEN), lambda i, be, nu: (layer, be[i], 0, 0)),
                      pl.BlockSpec((None, None, MOE_HIDDEN, D), lambda i, be, nu: (layer, be[i], 0, 0))],
            out_specs=blk_spec,
            scratch_shapes=[pltpu.VMEM((D, 2 * MOE_HIDDEN), BF16), pltpu.VMEM((MOE_HIDDEN, D), BF16)],
        ),
        out_shape=jax.ShapeDtypeStruct((n_rows * SUBLANES, LANES), F32),
        compiler_params=_params("arbitrary"),
        name="moe_experts",
    )(block_expert, n_used, x_sorted, w_gu, w_down)

    in_specs = [pl.BlockSpec((None, 2, TS), lambda b, s: (b * nt + s, 0, 0), memory_space=pltpu.SMEM),
                pl.BlockSpec((None, 2, TS), lambda b, s: (jnp.minimum(b * nt + s + 1, B * nt - 1), 0, 0),
                             memory_space=pltpu.SMEM),
                _row_spec(D, t0), _mod_spec(d6, t0), out_row(LANES), hbm]
    operands = [dest, dest, xs, modtab, wt, y_sorted]
    if final_g is None:
        out_spec, out_shape, aliases = _row_spec(D, t0), jax.ShapeDtypeStruct((B, S, D), F32), {2: 0}
    else:
        in_specs.append(_full_spec((1, D)))
        operands.append(final_g[None])
        out_spec, out_shape, aliases = out_row(D), jax.ShapeDtypeStruct((B, R, D), F32), {}
    out = pl.pallas_call(
        functools.partial(_combine_kernel, final_norm=final_g is not None),
        grid=(B, nt),
        in_specs=in_specs,
        out_specs=out_spec,
        out_shape=out_shape,
        scratch_shapes=[pltpu.VMEM((2 * 2 * TS * SUBLANES, LANES), F32), pltpu.SemaphoreType.DMA((2,))],
        input_output_aliases=aliases,
        compiler_params=_params("arbitrary", "arbitrary"),
        name="moe_combine",
    )(*operands)
    return out, x_sorted


def kernel(x, c, ctx, c_ctx, w_ada, b_ada, norm_mix_g, norm_ffn_g, final_norm_g, attn_w_qkv, attn_w_o, attn_q_norm_g, attn_k_norm_g, pool_w, pool_scale, ssm_w_in, ssm_conv_w, ssm_conv_b, ssm_dt_bias, ssm_a_log, ssm_d, ssm_norm_g, ssm_w_out, moe_w_router_group, moe_w_router_expert, moe_w_gate_up, moe_w_down):
    B, n, D = x.shape
    m = ctx.shape[1]
    depth = w_ada.shape[0]
    assert m == TS and n % TS == 0 and n % GRID_W == 0
    xs = jnp.concatenate([ctx, x], axis=1)
    mods = _ada(c, c_ctx, w_ada, b_ada)
    cos_t, sin_t = _rope_tables(n, m)
    sorted_buf = jnp.zeros((_moe_blocks(B * (m + n)) * MOE_BLOCK * SUBLANES, LANES), F32)
    for i in range(depth):
        kind, j = i % N_MIXERS, i // N_MIXERS
        ctx_out = i < depth - 1
        modtab = jnp.stack([jnp.broadcast_to(mods[i, B], (B, 6 * D)), mods[i, :B]], axis=1)[:, :, None, :]
        if kind == 0:
            xs = _attention_layer(xs, modtab, norm_mix_g[i], attn_w_qkv[j], attn_w_o[j], attn_q_norm_g[j],
                                  attn_k_norm_g[j], cos_t, sin_t, ctx_out)
        elif kind == 1:
            assert ctx_out
            xs = _pool_layer(xs, modtab, norm_mix_g[i], pool_w[j], pool_scale[j], m)
        else:
            assert ctx_out
            xs = _ssd_layer(xs, modtab, norm_mix_g[i], ssm_w_in[j], ssm_conv_w[j], ssm_conv_b[j],
                            ssm_dt_bias[j], ssm_a_log[j], ssm_d[j], ssm_norm_g[j], ssm_w_out[j], m)
        xs, sorted_buf = _moe_layer(xs, modtab, norm_ffn_g[i], moe_w_router_group[i], moe_w_router_expert[i],
                                    moe_w_gate_up, moe_w_down, i, ctx_out, sorted_buf,
                                    final_g=final_norm_g if i == depth - 1 else None)
    return xs
```

```python
import functools

import jax
import jax.numpy as jnp
from jax import lax
from jax.experimental import pallas as pl
from jax.experimental.pallas import tpu as pltpu

F32 = jnp.float32
BF16 = jnp.bfloat16
I32 = jnp.int32
HI = lax.Precision.HIGHEST
EPS = 1e-6
NEG_INF = float("-inf")
LOG2E = 1.4426950408889634

TS = 256
LANES = 128
SUBLANES = 8
GRID_W = 64
ROPE_THETA = 10000.0
N_MIXERS = 3

N_HEADS = 16
N_KV = 4
HEAD_DIM = 64
Q_PER_KV = N_HEADS // N_KV

POOL_WINDOWS = (2, 4, 8, 16)
POOL_PAD = 16
ROW_CHUNK = 256

SSM_HEADS = 32
SSM_HEAD_DIM = 64
SSM_GROUPS = 4
SSM_HPG = SSM_HEADS // SSM_GROUPS
SSM_STATE = 128
SSM_CONV = 4
SSM_CHUNK = 128
SSM_D_INNER = SSM_HEADS * SSM_HEAD_DIM
SSM_GN = SSM_GROUPS * SSM_STATE
SSM_CONV_DIM = SSM_D_INNER + 2 * SSM_GN
CONV_PAD = 8

MOE_GROUPS = 4
MOE_EPG = 8
MOE_EXPERTS = MOE_GROUPS * MOE_EPG
MOE_HIDDEN = 512
MOE_BLOCK = 256
DMA_UNROLL = 8
DISPATCH_TILES = 4


def _params(*sem):
    return pltpu.CompilerParams(dimension_semantics=sem)


def _silu(v):
    return v / (1.0 + jnp.exp(-v))


def _split(v):
    hi = v.astype(BF16)
    return hi, (v - hi.astype(F32)).astype(BF16)


def _dot_split_lhs(a, e):
    hi, lo = _split(a)
    return jnp.dot(hi, e, preferred_element_type=F32) + jnp.dot(lo, e, preferred_element_type=F32)


def _dot_split(a, b_hi, b_lo):
    hi, lo = _split(a)
    return jnp.dot(hi, b_hi, preferred_element_type=F32) + (
        jnp.dot(hi, b_lo, preferred_element_type=F32) + jnp.dot(lo, b_hi, preferred_element_type=F32))


def _normmod(x, g, shift, scale):
    ms = jnp.mean(x * x, axis=-1, keepdims=True)
    return (x * lax.rsqrt(ms + EPS) * g) * (1.0 + scale) + shift


def _ada_kernel(a_ref, w_ref, b_ref, o_ref):
    a = _silu(a_ref[...])
    o_ref[0] = jnp.dot(a, w_ref[0], precision=HI, preferred_element_type=F32) + b_ref[0]


def _ada(c, c_ctx, w_ada, b_ada):
    depth, d, d6 = w_ada.shape
    b = c.shape[0]
    assert b + 1 <= 8
    a = jnp.concatenate([c, c_ctx[None], jnp.zeros((8 - b - 1, d), F32)], axis=0)
    tn = 1536
    return pl.pallas_call(
        _ada_kernel,
        grid=(depth, d6 // tn),
        in_specs=[pl.BlockSpec((8, d), lambda i, j: (0, 0)),
                  pl.BlockSpec((1, d, tn), lambda i, j: (i, 0, j)),
                  pl.BlockSpec((1, 1, tn), lambda i, j: (i, 0, j))],
        out_specs=pl.BlockSpec((1, 8, tn), lambda i, j: (i, 0, j)),
        out_shape=jax.ShapeDtypeStruct((depth, 8, d6), F32),
        compiler_params=_params("arbitrary", "arbitrary"),
        name="ada",
    )(a, w_ada, b_ada.reshape(depth, 1, d6))


def _row_spec(width, t0, col=0):
    return pl.BlockSpec((None, TS, width), lambda b, s: (b, s + t0, col))


def _mod_spec(d6, t0):
    return pl.BlockSpec((None, None, 1, d6), lambda b, s: (b, jnp.minimum(s + t0, 1), 0, 0))


def _full_spec(shape):
    nd = len(shape)
    return pl.BlockSpec(shape, lambda b, s: (0,) * nd)


def _qkv_kernel(x_ref, mod_ref, g_ref, wT_ref, gqk_ref, cos_ref, sin_ref, qT_ref, k_ref, vT_ref):
    d = x_ref.shape[-1]
    nq = N_HEADS * HEAD_DIM
    nqk = nq + N_KV * HEAD_DIM
    h = _normmod(x_ref[...], g_ref[...], mod_ref[:, 0:d], mod_ref[:, d:2 * d])
    t = jnp.dot(wT_ref[...], h.T.astype(BF16), preferred_element_type=F32)
    cos = cos_ref[...]
    sin = sin_ref[...]
    row = lax.broadcasted_iota(I32, (HEAD_DIM, TS), 0)
    even = (row & 1) == 0
    for hh in range(nqk // HEAD_DIM):
        rows = slice(HEAD_DIM * hh, HEAD_DIM * (hh + 1))
        blk = t[rows, :]
        rinv = lax.rsqrt(jnp.mean(blk * blk, axis=0, keepdims=True) + EPS)
        y = blk * rinv * gqk_ref[rows, :]
        partner = jnp.where(even, pltpu.roll(y, HEAD_DIM - 1, 0), pltpu.roll(y, 1, 0))
        out = y * cos + partner * sin
        if hh < N_HEADS:
            qT_ref[rows, :] = out.astype(BF16)
        else:
            k_ref[hh - N_HEADS] = out.T.astype(BF16)
    vT_ref[...] = t[nqk:, :].astype(BF16)


def _attn_ctx_kernel(qT_ref, k_ref, vT_ref, o_ref):
    for g in range(Q_PER_KV):
        rows = slice(HEAD_DIM * g, HEAD_DIM * (g + 1))
        s = jnp.dot(k_ref[...], qT_ref[rows, :], preferred_element_type=F32)
        p = jnp.exp2(s - jnp.max(s, axis=0, keepdims=True))
        l = jnp.sum(p, axis=0, keepdims=True)
        o = jnp.dot(vT_ref[...], p.astype(BF16), preferred_element_type=F32)
        o_ref[rows, :] = (o / l).astype(BF16)


def _attn_main_kernel(qT_ref, k_ref, vT_ref, o_ref, s0_ref, s1_ref, m0_ref, m1_ref, *, n_units, q_col0):
    for ref in (s0_ref, s1_ref, m0_ref, m1_ref):
        ref[...] = jnp.zeros(ref.shape, F32)

    def unit(i):
        i = jnp.clip(i, 0, n_units - 1)
        row = pl.multiple_of((i % Q_PER_KV) * HEAD_DIM, HEAD_DIM)
        col = pl.multiple_of((i // Q_PER_KV) * TS, TS)
        return row, col

    def step(i, s_w, m_w, s_r, m_r):
        row, col = unit(i)
        q = qT_ref[pl.ds(row, HEAD_DIM), pl.ds(q_col0 + col, TS)]
        s = jnp.dot(k_ref[...], q, preferred_element_type=F32)
        s_w[...] = s
        m_w[...] = jnp.max(s, axis=0, keepdims=True)
        row, col = unit(i - 1)
        p = jnp.exp2(s_r[...] - m_r[...])
        l = jnp.sum(p, axis=0, keepdims=True)
        o = jnp.dot(vT_ref[...], p.astype(BF16), preferred_element_type=F32)
        o_ref[pl.ds(row, HEAD_DIM), pl.ds(col, TS)] = (o / l).astype(BF16)

    def body(j, carry):
        step(2 * j, s0_ref, m0_ref, s1_ref, m1_ref)
        step(2 * j + 1, s1_ref, m1_ref, s0_ref, m0_ref)
        return carry

    lax.fori_loop(0, n_units // 2 + 1, body, 0)


def _oproj_kernel(oT_ref, w_ref, x_ref, mod_ref, out_ref):
    d = x_ref.shape[-1]
    y = lax.dot_general(oT_ref[...], w_ref[...], (((0,), (0,)), ((), ())), preferred_element_type=F32)
    out_ref[...] = x_ref[...] + mod_ref[:, 2 * d:3 * d] * y


def _rope_tables(n, m):
    rows = n // GRID_W
    row = jnp.broadcast_to(jnp.arange(rows)[:, None], (rows, GRID_W)).reshape(-1).astype(F32)
    col = jnp.broadcast_to(jnp.arange(GRID_W)[None, :], (rows, GRID_W)).reshape(-1).astype(F32)
    n_freq = HEAD_DIM // 4
    inv_freq = ROPE_THETA ** (-jnp.arange(n_freq, dtype=F32) / n_freq)
    ang = jnp.concatenate([row[:, None] * inv_freq, col[:, None] * inv_freq], axis=-1)
    cos = jnp.repeat(jnp.cos(ang), 2, axis=1)
    sign = jnp.tile(jnp.array([-1.0, 1.0], F32), HEAD_DIM // 2)
    sin = jnp.repeat(jnp.sin(ang), 2, axis=1) * sign
    cos = jnp.concatenate([jnp.ones((m, HEAD_DIM), F32), cos], axis=0)
    sin = jnp.concatenate([jnp.zeros((m, HEAD_DIM), F32), sin], axis=0)
    return cos.T, sin.T


def _attention_layer(xs, modtab, norm_g, w_qkv, w_o, q_g, k_g, cos_t, sin_t, ctx_out):
    B, S, D = xs.shape
    nt = S // TS
    nq = N_HEADS * HEAD_DIM
    nkv = N_KV * HEAD_DIM
    nqk = nq + nkv
    d6 = modtab.shape[-1]
    q_scale = HEAD_DIM ** -0.5 * LOG2E
    gqk = jnp.concatenate([jnp.tile(q_g, N_HEADS) * q_scale, jnp.tile(k_g, N_KV)])
    gqk = jnp.broadcast_to(gqk[:, None], (nqk, TS))
    qT, k4, vT = pl.pallas_call(
        _qkv_kernel,
        grid=(B, nt),
        in_specs=[_row_spec(D, 0), _mod_spec(d6, 0), _full_spec((1, D)), _full_spec((nqk + nkv, D)),
                  _full_spec((nqk, TS)),
                  pl.BlockSpec((HEAD_DIM, TS), lambda b, s: (0, s)),
                  pl.BlockSpec((HEAD_DIM, TS), lambda b, s: (0, s))],
        out_specs=[pl.BlockSpec((None, nq, TS), lambda b, s: (b, 0, s)),
                   pl.BlockSpec((None, N_KV, TS, HEAD_DIM), lambda b, s: (b, 0, s, 0)),
                   pl.BlockSpec((None, nkv, TS), lambda b, s: (b, 0, s))],
        out_shape=[jax.ShapeDtypeStruct((B, nq, S), BF16),
                   jax.ShapeDtypeStruct((B, N_KV, S, HEAD_DIM), BF16),
                   jax.ShapeDtypeStruct((B, nkv, S), BF16)],
        compiler_params=_params("arbitrary", "arbitrary"),
        name="attn_qkv",
    )(xs, modtab, norm_g[None], w_qkv.T.astype(BF16), gqk, cos_t, sin_t)

    gw = Q_PER_KV * HEAD_DIM
    n_lat = S - TS
    w_o = w_o.astype(BF16)
    if ctx_out:
        oT_ctx = pl.pallas_call(
            _attn_ctx_kernel,
            grid=(B, N_KV),
            in_specs=[pl.BlockSpec((None, gw, TS), lambda b, kv: (b, kv, 0)),
                      pl.BlockSpec((None, None, TS, HEAD_DIM), lambda b, kv: (b, kv, 0, 0)),
                      pl.BlockSpec((None, HEAD_DIM, TS), lambda b, kv: (b, kv, 0))],
            out_specs=pl.BlockSpec((None, gw, TS), lambda b, kv: (b, kv, 0)),
            out_shape=jax.ShapeDtypeStruct((B, nq, TS), BF16),
            compiler_params=_params("arbitrary", "arbitrary"),
            name="attn_ctx",
        )(qT, k4, vT)
        xs = pl.pallas_call(
            _oproj_kernel,
            grid=(B, 1),
            in_specs=[pl.BlockSpec((None, nq, TS), lambda b, s: (b, 0, 0)), _full_spec((nq, D)),
                      _row_spec(D, 0), _mod_spec(d6, 0)],
            out_specs=_row_spec(D, 0),
            out_shape=jax.ShapeDtypeStruct((B, S, D), F32),
            input_output_aliases={2: 0},
            compiler_params=_params("arbitrary", "arbitrary"),
            name="attn_oproj_ctx",
        )(oT_ctx, w_o, xs, modtab)

    n_units = (n_lat // TS) * Q_PER_KV
    assert n_units % 2 == 0
    oT = pl.pallas_call(
        functools.partial(_attn_main_kernel, n_units=n_units, q_col0=TS),
        grid=(B, N_KV),
        in_specs=[pl.BlockSpec((None, gw, S), lambda b, kv: (b, kv, 0)),
                  pl.BlockSpec((None, None, S, HEAD_DIM), lambda b, kv: (b, kv, 0, 0)),
                  pl.BlockSpec((None, HEAD_DIM, S), lambda b, kv: (b, kv, 0))],
        out_specs=pl.BlockSpec((None, gw, n_lat), lambda b, kv: (b, kv, 0)),
        out_shape=jax.ShapeDtypeStruct((B, nq, n_lat), BF16),
        scratch_shapes=[pltpu.VMEM((S, TS), F32), pltpu.VMEM((S, TS), F32),
                        pltpu.VMEM((1, TS), F32), pltpu.VMEM((1, TS), F32)],
        compiler_params=_params("arbitrary", "arbitrary"),
        name="attn_core",
    )(qT, k4, vT)

    return pl.pallas_call(
        _oproj_kernel,
        grid=(B, n_lat // TS),
        in_specs=[pl.BlockSpec((None, nq, TS), lambda b, s: (b, 0, s)), _full_spec((nq, D)),
                  _row_spec(D, 1), _mod_spec(d6, 1)],
        out_specs=_row_spec(D, 1),
        out_shape=jax.ShapeDtypeStruct((B, S, D), F32),
        input_output_aliases={2: 0},
        compiler_params=_params("arbitrary", "arbitrary"),
        name="attn_oproj",
    )(oT, w_o, xs, modtab)


def _normmod_kernel(x_ref, mod_ref, g_ref, h_ref):
    d = x_ref.shape[-1]
    h_ref[...] = _normmod(x_ref[...], g_ref[...], mod_ref[:, 0:d], mod_ref[:, d:2 * d])


def _pool_kernel(h_ref, x_ref, w_ref, ps_ref, gate_ref, out_ref, pad_ref, *, segs):
    gi = pl.program_id(1)
    zeros = jnp.zeros((POOL_PAD, h_ref.shape[-1]), F32)
    for widx, win in enumerate(POOL_WINDOWS):
        half = win // 2

        @pl.when(gi == widx)
        def _(half=half):
            for si, (r0, n) in enumerate(segs):
                pad_ref[0:POOL_PAD, :] = zeros
                pad_ref[POOL_PAD:POOL_PAD + n, :] = h_ref[r0:r0 + n, :]
                pad_ref[POOL_PAD + n:2 * POOL_PAD + n, :] = zeros
                gate = gate_ref[si]
                for c0 in range(0, n, ROW_CHUNK):
                    base = POOL_PAD + c0
                    acc = pad_ref[base - half:base - half + ROW_CHUNK, :]
                    for j in range(-half + 1, half):
                        acc = acc + pad_ref[base + j:base + j + ROW_CHUNK, :]
                    t = c0 + lax.broadcasted_iota(I32, (ROW_CHUNK, 1), 0)
                    cnt = jnp.minimum(t + half, n) - jnp.maximum(t - half, 0)
                    diff = acc / cnt.astype(F32) - pad_ref[base:base + ROW_CHUNK, :]
                    y = jnp.dot(diff.astype(BF16), w_ref[...], preferred_element_type=F32) * ps_ref[...]
                    rows = slice(r0 + c0, r0 + c0 + ROW_CHUNK)
                    out_ref[rows, :] = x_ref[rows, :] + gate * y


def _pool_layer(xs, modtab, norm_g, w_pool, pool_scale, m):
    B, S, D = xs.shape
    nt = S // TS
    d6 = modtab.shape[-1]
    pg = D // len(POOL_WINDOWS)
    h = pl.pallas_call(
        _normmod_kernel,
        grid=(B, nt),
        in_specs=[_row_spec(D, 0), _mod_spec(d6, 0), _full_spec((1, D))],
        out_specs=_row_spec(D, 0),
        out_shape=jax.ShapeDtypeStruct((B, S, D), F32),
        compiler_params=_params("arbitrary", "arbitrary"),
        name="pool_normmod",
    )(xs, modtab, norm_g[None])
    segs = ((0, m), (m, S - m))
    seq_spec = pl.BlockSpec((None, S, pg), lambda b, g: (b, 0, g))
    return pl.pallas_call(
        functools.partial(_pool_kernel, segs=segs),
        grid=(B, len(POOL_WINDOWS)),
        in_specs=[seq_spec, seq_spec,
                  pl.BlockSpec((None, pg, pg), lambda b, g: (g, 0, 0)),
                  pl.BlockSpec((1, pg), lambda b, g: (0, g)),
                  pl.BlockSpec((None, 2, 1, pg), lambda b, g: (b, 0, 0, 2 * (D // pg) + g))],
        out_specs=seq_spec,
        out_shape=jax.ShapeDtypeStruct((B, S, D), F32),
        scratch_shapes=[pltpu.VMEM((S - m + 2 * POOL_PAD, pg), F32)],
        compiler_params=_params("arbitrary", "arbitrary"),
        name="pool_mix",
    )(h, xs, w_pool.astype(BF16), pool_scale[None], modtab)


def _ssm_in_kernel(x_ref, mod_ref, g_ref, wz_ref, wx_ref, wdt_hi_ref, wdt_lo_ref, z_ref, xbc_ref, dt_ref):
    d = x_ref.shape[-1]
    hf = _normmod(x_ref[...], g_ref[...], mod_ref[:, 0:d], mod_ref[:, d:2 * d])
    h = hf.astype(BF16)
    z_ref[...] = jnp.dot(h, wz_ref[...], preferred_element_type=F32)
    xbc_ref[...] = jnp.dot(h, wx_ref[...], preferred_element_type=F32)
    dt_ref[...] = _dot_split(hf, wdt_hi_ref[...], wdt_lo_ref[...])


def _conv_kernel(u_ref, w_ref, b_ref, o_ref, pad_ref, *, segs):
    zeros = jnp.zeros((CONV_PAD, u_ref.shape[-1]), F32)
    for r0, n in segs:
        pad_ref[0:CONV_PAD, :] = zeros
        pad_ref[CONV_PAD:CONV_PAD + n, :] = u_ref[r0:r0 + n, :]
        pad_ref[CONV_PAD + n:2 * CONV_PAD + n, :] = zeros
        for c0 in range(0, n, ROW_CHUNK):
            acc = b_ref[...]
            for k in range(SSM_CONV):
                lo = CONV_PAD + c0 - SSM_CONV // 2 + k
                acc = acc + w_ref[k:k + 1, :] * pad_ref[lo:lo + ROW_CHUNK, :]
            o_ref[r0 + c0:r0 + c0 + ROW_CHUNK, :] = _silu(acc)


def _ssd_kernel(xf_ref, dtf_ref, xb_ref, dtb_blk_ref, dtb_ref, a_ref, tri_ref, self_ref, selb_ref,
                yf_ref, yb_ref, hf_ref, hb_ref):
    @pl.when(pl.program_id(1) == 0)
    def _():
        hf_ref[...] = jnp.zeros(hf_ref.shape, F32)
        hb_ref[...] = jnp.zeros(hb_ref.shape, F32)

    _ssd_chunk(xf_ref, dtf_ref, dtb_ref, a_ref, tri_ref, self_ref, yf_ref, hf_ref, direction=0)
    _ssd_chunk(xb_ref, dtb_blk_ref, dtb_ref, a_ref, tri_ref, selb_ref, yb_ref, hb_ref, direction=1)


def _ssd_chunk(xbc_ref, dt_ref, dtb_ref, a_ref, tri_ref, sel_ref, y_ref, h_ref, *, direction):
    q = SSM_CHUNK
    lane0 = SSM_HEADS * direction
    raw = dt_ref[...] + dtb_ref[...]
    dt = jnp.maximum(raw, 0.0) + jnp.log1p(jnp.exp(-jnp.abs(raw)))
    la = dt * a_ref[...]
    cs = jnp.dot(tri_ref[...], la, precision=HI, preferred_element_type=F32)
    tot = cs[q - 1:q, :]
    if direction == 0:
        u = cs
        dec_in = jnp.exp(u)
        dec_end = jnp.exp(tot - u)
    else:
        u = la - cs
        dec_in = jnp.exp(tot + u)
        dec_end = jnp.exp(-u)
    uT = u.T
    dtT = dt.T
    stacked = jnp.concatenate([dec_in, dt * dec_end, jnp.broadcast_to(jnp.exp(tot), (8, LANES))], axis=0)
    spread = _dot_split_lhs(stacked, sel_ref[...])
    din_col = spread[0:q]
    dtot_col = spread[2 * q:2 * q + 1]
    x = xbc_ref[:, 0:SSM_D_INNER]
    x_b = x.astype(BF16)
    xd_b = (x * spread[q:2 * q]).astype(BF16)
    row = lax.broadcasted_iota(I32, (q, q), 0)
    col = lax.broadcasted_iota(I32, (q, q), 1)
    mask = (row >= col) if direction == 0 else (col >= row)
    gw = SSM_HPG * SSM_HEAD_DIM
    for g in range(SSM_GROUPS):
        b_lo = SSM_D_INNER + SSM_STATE * g
        c_lo = SSM_D_INNER + SSM_GN + SSM_STATE * g
        bT = xbc_ref[:, b_lo:b_lo + SSM_STATE].T.astype(BF16)
        cg = xbc_ref[:, c_lo:c_lo + SSM_STATE].astype(BF16)
        scores = jnp.dot(cg, bT, preferred_element_type=F32)
        hg = h_ref[g]
        cols = slice(gw * g, gw * (g + 1))
        y_off = jnp.dot(cg, hg.astype(BF16), preferred_element_type=F32) * din_col[:, cols]
        for r in range(SSM_HPG):
            hh = SSM_HPG * g + r
            hl = lane0 + hh
            seg = u[:, hl:hl + 1] - uT[hl:hl + 1, :]
            decay = jnp.exp(jnp.where(mask, seg, NEG_INF))
            hc = slice(SSM_HEAD_DIM * hh, SSM_HEAD_DIM * (hh + 1))
            mix = (scores * decay * dtT[hl:hl + 1, :]).astype(BF16)
            y_d = jnp.dot(mix, x_b[:, hc], preferred_element_type=F32)
            y_ref[:, hc] = y_d + y_off[:, SSM_HEAD_DIM * r:SSM_HEAD_DIM * (r + 1)]
        h_ref[g] = hg * dtot_col[:, cols] + jnp.dot(bT, xd_b[:, cols], preferred_element_type=F32)


def _ssm_out_kernel(y0_ref, y1_ref, xh_ref, z_ref, dsk_ref, ng_ref, w_ref, x_ref, mod_ref, out_ref):
    d = x_ref.shape[-1]
    y = y0_ref[...] + y1_ref[...] + dsk_ref[...] * xh_ref[...]
    g = y * _silu(z_ref[...])
    gsz = SSM_D_INNER // SSM_GROUPS
    parts = []
    for k in range(SSM_GROUPS):
        gk = g[:, gsz * k:gsz * (k + 1)]
        parts.append(gk * lax.rsqrt(jnp.mean(gk * gk, axis=-1, keepdims=True) + EPS))
    gn = (jnp.concatenate(parts, axis=1) * ng_ref[...]).astype(BF16)
    o = jnp.dot(gn, w_ref[...], preferred_element_type=F32)
    out_ref[...] = x_ref[...] + mod_ref[:, 2 * d:3 * d] * o


def _ssd_layer(xs, modtab, norm_g, w_in, conv_w, conv_b, dt_bias, a_log, d_skip, ssm_norm_g, w_out, m):
    B, S, D = xs.shape
    nt = S // TS
    d6 = modtab.shape[-1]
    di = SSM_D_INNER
    cd = SSM_CONV_DIM
    wz = w_in[:, :di].astype(BF16)
    wx = w_in[:, di:di + cd].astype(BF16)
    wdt_hi, wdt_lo = _split(jnp.pad(w_in[:, di + cd:], ((0, 0), (0, LANES - 2 * SSM_HEADS))))
    z, xbc_raw, dt_raw = pl.pallas_call(
        _ssm_in_kernel,
        grid=(B, nt),
        in_specs=[_row_spec(D, 0), _mod_spec(d6, 0), _full_spec((1, D)), _full_spec((D, di)),
                  _full_spec((D, cd)), _full_spec((D, LANES)), _full_spec((D, LANES))],
        out_specs=[_row_spec(di, 0), _row_spec(cd, 0), _row_spec(LANES, 0)],
        out_shape=[jax.ShapeDtypeStruct((B, S, di), F32), jax.ShapeDtypeStruct((B, S, cd), F32),
                   jax.ShapeDtypeStruct((B, S, LANES), F32)],
        compiler_params=_params("arbitrary", "arbitrary"),
        name="ssm_in",
    )(xs, modtab, norm_g[None], wz, wx, wdt_hi, wdt_lo)

    segs = ((0, m), (m, S - m))
    cw = 256
    seq_spec = pl.BlockSpec((None, S, cw), lambda b, j: (b, 0, j))
    xbc = pl.pallas_call(
        functools.partial(_conv_kernel, segs=segs),
        grid=(B, cd // cw),
        in_specs=[seq_spec, pl.BlockSpec((SSM_CONV, cw), lambda b, j: (0, j)),
                  pl.BlockSpec((1, cw), lambda b, j: (0, j))],
        out_specs=seq_spec,
        out_shape=jax.ShapeDtypeStruct((B, S, cd), F32),
        scratch_shapes=[pltpu.VMEM((S - m + 2 * CONV_PAD, cw), F32)],
        compiler_params=_params("arbitrary", "arbitrary"),
        name="ssm_conv",
    )(xbc_raw, conv_w, conv_b[None])

    q = SSM_CHUNK
    nc = S // q
    mc = m // q
    pad = LANES - 2 * SSM_HEADS
    dtb = jnp.pad(dt_bias.reshape(-1), (0, pad))[None]
    a_neg = jnp.pad(-jnp.exp(a_log.reshape(-1)), (0, pad))[None]
    tri = (jnp.arange(q)[:, None] >= jnp.arange(q)[None, :]).astype(F32)
    fmap = lambda c: c
    bmap = lambda c: jnp.where(c < mc, mc - 1 - c, nc - 1 - (c - mc))
    sels = [(jnp.arange(LANES)[:, None] == (SSM_HEADS * dr + jnp.arange(di) // SSM_HEAD_DIM)[None, :]).astype(BF16)
            for dr in range(2)]
    chunk_spec = lambda width, cmap: pl.BlockSpec((None, q, width), lambda b, c: (b, cmap(c), 0))
    state = pltpu.VMEM((SSM_GROUPS, SSM_STATE, SSM_HPG * SSM_HEAD_DIM), F32)
    ys = pl.pallas_call(
        _ssd_kernel,
        grid=(B, nc),
        in_specs=[chunk_spec(cd, fmap), chunk_spec(LANES, fmap), chunk_spec(cd, bmap), chunk_spec(LANES, bmap),
                  _full_spec((1, LANES)), _full_spec((1, LANES)), _full_spec((q, q)),
                  _full_spec((LANES, di)), _full_spec((LANES, di))],
        out_specs=[chunk_spec(di, fmap), chunk_spec(di, bmap)],
        out_shape=[jax.ShapeDtypeStruct((B, S, di), F32), jax.ShapeDtypeStruct((B, S, di), F32)],
        scratch_shapes=[state, state],
        compiler_params=_params("arbitrary", "arbitrary"),
        name="ssd_scan",
    )(xbc, dt_raw, xbc, dt_raw, dtb, a_neg, tri, sels[0], sels[1])

    dsk = jnp.repeat(d_skip[0] + d_skip[1], SSM_HEAD_DIM)[None]
    return pl.pallas_call(
        _ssm_out_kernel,
        grid=(B, nt),
        in_specs=[_row_spec(di, 0), _row_spec(di, 0), _row_spec(di, 0), _row_spec(di, 0),
                  _full_spec((1, di)), _full_spec((1, di)), _full_spec((di, D)),
                  _row_spec(D, 0), _mod_spec(d6, 0)],
        out_specs=_row_spec(D, 0),
        out_shape=jax.ShapeDtypeStruct((B, S, D), F32),
        input_output_aliases={7: 0},
        compiler_params=_params("arbitrary", "arbitrary"),
        name="ssm_out",
    )(ys[0], ys[1], xbc, z, dsk, ssm_norm_g[None], w_out.astype(BF16), xs, modtab)


def _store_slabs(ref, v):
    rows = v.shape[0]
    for s in range(SUBLANES):
        ref[pl.ds(s, rows, stride=SUBLANES), :] = v[:, LANES * s:LANES * (s + 1)]


def _load_slab_chunk(ref, rows, s, row0=0):
    return ref[pl.ds(SUBLANES * row0 + s, rows, stride=SUBLANES), :]


def _slab(ref, row):
    return ref.at[pl.ds(pl.multiple_of(row * SUBLANES, SUBLANES), SUBLANES)]


def _route(logits):
    lane = lax.broadcasted_iota(I32, logits.shape, 1).astype(F32)
    big = float(LANES)
    gl = jnp.where(lane < MOE_GROUPS, logits, NEG_INF)
    gmax = jnp.max(gl, axis=1, keepdims=True)
    gate = 1.0 / jnp.sum(jnp.exp(gl - gmax), axis=1, keepdims=True)
    grp = jnp.min(jnp.where(gl == gmax, lane, big), axis=1, keepdims=True)
    lo = MOE_GROUPS + MOE_EPG * grp
    el = jnp.where((lane >= lo) & (lane < lo + MOE_EPG), logits, NEG_INF)
    v1 = jnp.max(el, axis=1, keepdims=True)
    i1 = jnp.min(jnp.where(el == v1, lane, big), axis=1, keepdims=True)
    el2 = jnp.where(lane == i1, NEG_INF, el)
    v2 = jnp.max(el2, axis=1, keepdims=True)
    i2 = jnp.min(jnp.where(el2 == v2, lane, big), axis=1, keepdims=True)
    e2 = jnp.exp(v2 - v1)
    den = 1.0 + e2
    return i1, i2, gate * (1.0 / den), gate * (e2 / den)


def _router_kernel(x_ref, mod_ref, g_ref, wr_hi_ref, wr_lo_ref, ltri_ref, tok_ref, meta_ref, wt_ref, cnt_ref,
                   run_ref):
    d = x_ref.shape[-1]

    @pl.when((pl.program_id(0) == 0) & (pl.program_id(1) == 0))
    def _():
        run_ref[...] = jnp.zeros(run_ref.shape, F32)

    t = _normmod(x_ref[...], g_ref[...], mod_ref[:, 3 * d:4 * d], mod_ref[:, 4 * d:5 * d])
    _store_slabs(tok_ref, t)
    logits = _dot_split(t, wr_hi_ref[...], wr_lo_ref[...])
    i1, i2, w1, w2 = _route(logits)
    lane = lax.broadcasted_iota(I32, logits.shape, 1).astype(F32)
    wt_ref[...] = jnp.where(lane == 0.0, w1, jnp.where(lane == 1.0, w2, 0.0))
    oh1 = (lane == i1).astype(F32)
    oh2 = (lane == i2).astype(F32)
    oh = oh1 + oh2
    before = jnp.dot(ltri_ref[...], oh.astype(BF16), preferred_element_type=F32) + run_ref[...]
    r1 = jnp.sum(oh1 * before, axis=1, keepdims=True)
    r2 = jnp.sum(oh2 * before, axis=1, keepdims=True)
    run = run_ref[...] + jnp.sum(oh, axis=0, keepdims=True)
    run_ref[...] = run
    cnt_ref[...] = jnp.broadcast_to(run, cnt_ref.shape)
    cols = jnp.where(lane == 0.0, i1 - MOE_GROUPS,
                     jnp.where(lane == 1.0, i2 - MOE_GROUPS,
                               jnp.where(lane == 2.0, r1, jnp.where(lane == 3.0, r2, 0.0))))
    meta_ref[...] = cols.T[0:8, :].astype(I32)


def _row_copy(src_ref, src_row, dst_ref, dst_row, sem):
    return pltpu.make_async_copy(_slab(src_ref, src_row), _slab(dst_ref, dst_row), sem)


def _dispatch_kernel(dest_ref, tok_ref, init_hbm, xs_hbm, sem):
    del init_hbm

    for h in range(DISPATCH_TILES):
        def body(j, carry, h=h):
            _row_copy(tok_ref, h * TS + j, xs_hbm, dest_ref[h, 0, j], sem).start(priority=0)
            _row_copy(tok_ref, h * TS + j, xs_hbm, dest_ref[h, 1, j], sem).start(priority=1)
            return carry

        lax.fori_loop(0, TS, body, 0, unroll=DMA_UNROLL)
    for _ in range(2):
        pltpu.make_async_copy(tok_ref, xs_hbm.at[pl.ds(0, tok_ref.shape[0])], sem).wait()


def _expert_kernel(be_ref, nu_ref, xs_ref, wgu_ref, wd_ref, y_ref, wgu_b_ref, wd_b_ref):
    i = pl.program_id(0)
    rows = xs_ref.shape[0] // SUBLANES

    @pl.when((i == 0) | (be_ref[i] != be_ref[jnp.maximum(i - 1, 0)]))
    def _():
        wgu_b_ref[...] = wgu_ref[...].astype(BF16)
        wd_b_ref[...] = wd_ref[...].astype(BF16)

    @pl.when(i < nu_ref[0])
    def _():
        x = jnp.concatenate([_load_slab_chunk(xs_ref, rows, s) for s in range(SUBLANES)], axis=1)
        gu = jnp.dot(x.astype(BF16), wgu_b_ref[...], preferred_element_type=F32)
        hid = _silu(gu[:, :MOE_HIDDEN]) * gu[:, MOE_HIDDEN:]
        _store_slabs(y_ref, jnp.dot(hid.astype(BF16), wd_b_ref[...], preferred_element_type=F32))

    @pl.when(i >= nu_ref[0])
    def _():
        y_ref[...] = jnp.zeros(y_ref.shape, F32)


def _combine_kernel(dest_ref, next_dest_ref, x_ref, mod_ref, wt_ref, y_hbm, *rest, final_norm):
    if final_norm:
        fg_ref, out_ref, buf_ref, sem = rest
    else:
        out_ref, buf_ref, sem = rest
    d = x_ref.shape[-1]
    step = pl.program_id(0) * pl.num_programs(1) + pl.program_id(1)
    n_steps = pl.num_programs(0) * pl.num_programs(1)
    slot = step % 2
    slot_rows = 2 * TS

    def gather(d_ref, to_slot):
        base = to_slot * slot_rows

        def body(j, carry):
            _row_copy(y_hbm, d_ref[0, j], buf_ref, base + j, sem.at[to_slot]).start(priority=0)
            _row_copy(y_hbm, d_ref[1, j], buf_ref, base + TS + j, sem.at[to_slot]).start(priority=1)
            return carry

        lax.fori_loop(0, TS, body, 0, unroll=DMA_UNROLL)

    @pl.when(step == 0)
    def _():
        gather(dest_ref, 0)

    @pl.when(step + 1 < n_steps)
    def _():
        gather(next_dest_ref, 1 - slot)

    row0 = slot * slot_rows
    slot_view = buf_ref.at[pl.ds(pl.multiple_of(row0 * SUBLANES, SUBLANES), slot_rows * SUBLANES)]
    pltpu.make_async_copy(y_hbm.at[pl.ds(0, slot_rows * SUBLANES)], slot_view, sem.at[slot]).wait()
    w1 = wt_ref[:, 0:1]
    w2 = wt_ref[:, 1:2]
    for s in range(SUBLANES):
        cols = slice(LANES * s, LANES * (s + 1))
        f = (w1 * _load_slab_chunk(buf_ref, TS, s, row0=row0)
             + w2 * _load_slab_chunk(buf_ref, TS, s, row0=row0 + TS))
        out_ref[:, cols] = x_ref[:, cols] + mod_ref[:, 5 * d + LANES * s:5 * d + LANES * (s + 1)] * f
    if final_norm:
        v = out_ref[...]
        out_ref[...] = v * lax.rsqrt(jnp.mean(v * v, axis=-1, keepdims=True) + EPS) * fg_ref[...]


def _moe_blocks(n_tokens):
    return -(-2 * n_tokens // MOE_BLOCK) + MOE_EXPERTS


def _moe_layer(xs, modtab, norm_g, w_rg, w_re, w_gu, w_down, layer, ctx_out, sorted_init, final_g=None):
    assert final_g is None or not ctx_out
    B, S, D = xs.shape
    assert D == SUBLANES * LANES
    d6 = modtab.shape[-1]
    t0 = 0 if ctx_out else 1
    nt = S // TS - t0
    R = nt * TS
    wr_hi, wr_lo = _split(
        jnp.pad(jnp.concatenate([w_rg, w_re], axis=1), ((0, 0), (0, LANES - MOE_GROUPS - MOE_EXPERTS))))
    out_row = lambda width: pl.BlockSpec((None, TS, width), lambda b, s: (b, s, 0))
    ltri = (jnp.arange(TS)[:, None] > jnp.arange(TS)[None, :]).astype(BF16)
    tok, meta, wt, cnt = pl.pallas_call(
        _router_kernel,
        grid=(B, nt),
        in_specs=[_row_spec(D, t0), _mod_spec(d6, t0), _full_spec((1, D)), _full_spec((D, LANES)),
                  _full_spec((D, LANES)), _full_spec((TS, TS))],
        out_specs=[pl.BlockSpec((TS * SUBLANES, LANES), lambda b, s: (b * nt + s, 0)),
                   pl.BlockSpec((None, 8, TS), lambda b, s: (b * nt + s, 0, 0)), out_row(LANES),
                   _full_spec((8, LANES))],
        out_shape=[jax.ShapeDtypeStruct((B * R * SUBLANES, LANES), F32), jax.ShapeDtypeStruct((B * nt, 8, TS), I32),
                   jax.ShapeDtypeStruct((B, R, LANES), F32), jax.ShapeDtypeStruct((8, LANES), F32)],
        scratch_shapes=[pltpu.VMEM((1, LANES), F32)],
        compiler_params=_params("arbitrary", "arbitrary"),
        name="moe_router",
    )(xs, modtab, norm_g[None], wr_hi, wr_lo, ltri)

    T = B * R
    n_rows = sorted_init.shape[0] // SUBLANES
    n_blocks = n_rows // MOE_BLOCK
    assert n_blocks >= _moe_blocks(T)
    counts = cnt[0, MOE_GROUPS:MOE_GROUPS + MOE_EXPERTS].astype(I32)
    padded = (counts + MOE_BLOCK - 1) // MOE_BLOCK * MOE_BLOCK
    pends = jnp.cumsum(padded)
    pstarts = pends - padded
    blk_start = jnp.arange(n_blocks, dtype=I32) * MOE_BLOCK
    block_expert = jnp.minimum(jnp.sum(pends[None, :] <= blk_start[:, None], axis=1),
                               MOE_EXPERTS - 1).astype(I32)
    n_used = (pends[-1:] // MOE_BLOCK).astype(I32)

    hbm = pl.BlockSpec(memory_space=pltpu.MemorySpace.HBM)
    expert_start = jnp.sum(jnp.where(meta[:, 0:2, :, None] == jnp.arange(MOE_EXPERTS), pstarts, 0), axis=-1)
    dest = expert_start + meta[:, 2:4, :]
    assert (B * nt) % DISPATCH_TILES == 0
    x_sorted = pl.pallas_call(
        _dispatch_kernel,
        grid=(B * nt // DISPATCH_TILES,),
        in_specs=[pl.BlockSpec((DISPATCH_TILES, 2, TS), lambda i: (i, 0, 0), memory_space=pltpu.SMEM),
                  pl.BlockSpec((DISPATCH_TILES * TS * SUBLANES, LANES), lambda i: (i, 0)), hbm],
        out_specs=hbm,
        out_shape=jax.ShapeDtypeStruct((n_rows * SUBLANES, LANES), F32),
        scratch_shapes=[pltpu.SemaphoreType.DMA(())],
        input_output_aliases={2: 0},
        compiler_params=_params("arbitrary"),
        name="moe_dispatch",
    )(dest, tok, sorted_init)

    blk_spec = pl.BlockSpec((MOE_BLOCK * SUBLANES, LANES), lambda i, be, nu: (i, 0))
    y_sorted = pl.pallas_call(
        _expert_kernel,
        grid_spec=pltpu.PrefetchScalarGridSpec(
            num_scalar_prefetch=2,
            grid=(n_blocks,),
            in_specs=[blk_spec,
                      pl.BlockSpec((None, None, D, 2 * MOE_HIDDEN), lambda i, be, nu: (layer, be[i], 0, 0)),
                      pl.BlockSpec((None, None, MOE_HIDDEN, D), lambda i, be, nu: (layer, be[i], 0, 0))],
            out_specs=blk_spec,
            scratch_shapes=[pltpu.VMEM((D, 2 * MOE_HIDDEN), BF16), pltpu.VMEM((MOE_HIDDEN, D), BF16)],
        ),
        out_shape=jax.ShapeDtypeStruct((n_rows * SUBLANES, LANES), F32),
        compiler_params=_params("arbitrary"),
        name="moe_experts",
    )(block_expert, n_used, x_sorted, w_gu, w_down)

    in_specs = [pl.BlockSpec((None, 2, TS), lambda b, s: (b * nt + s, 0, 0), memory_space=pltpu.SMEM),
                pl.BlockSpec((None, 2, TS), lambda b, s: (jnp.minimum(b * nt + s + 1, B * nt - 1), 0, 0),
                             memory_space=pltpu.SMEM),
                _row_spec(D, t0), _mod_spec(d6, t0), out_row(LANES), hbm]
    operands = [dest, dest, xs, modtab, wt, y_sorted]
    if final_g is None:
        out_spec, out_shape, aliases = _row_spec(D, t0), jax.ShapeDtypeStruct((B, S, D), F32), {2: 0}
    else:
        in_specs.append(_full_spec((1, D)))
        operands.append(final_g[None])
        out_spec, out_shape, aliases = out_row(D), jax.ShapeDtypeStruct((B, R, D), F32), {}
    out = pl.pallas_call(
        functools.partial(_combine_kernel, final_norm=final_g is not None),
        grid=(B, nt),
        in_specs=in_specs,
        out_specs=out_spec,
        out_shape=out_shape,
        scratch_shapes=[pltpu.VMEM((2 * 2 * TS * SUBLANES, LANES), F32), pltpu.SemaphoreType.DMA((2,))],
        input_output_aliases=aliases,
        compiler_params=_params("arbitrary", "arbitrary"),
        name="moe_combine",
    )(*operands)
    return out, x_sorted


def kernel(x, c, ctx, c_ctx, w_ada, b_ada, norm_mix_g, norm_ffn_g, final_norm_g, attn_w_qkv, attn_w_o, attn_q_norm_g, attn_k_norm_g, pool_w, pool_scale, ssm_w_in, ssm_conv_w, ssm_conv_b, ssm_dt_bias, ssm_a_log, ssm_d, ssm_norm_g, ssm_w_out, moe_w_router_group, moe_w_router_expert, moe_w_gate_up, moe_w_down):
    B, n, D = x.shape
    m = ctx.shape[1]
    depth = w_ada.shape[0]
    assert m == TS and n % TS == 0 and n % GRID_W == 0
    xs = jnp.concatenate([ctx, x], axis=1)
    mods = _ada(c, c_ctx, w_ada, b_ada)
    cos_t, sin_t = _rope_tables(n, m)
    sorted_buf = jnp.zeros((_moe_blocks(B * (m + n)) * MOE_BLOCK * SUBLANES, LANES), F32)
    for i in range(depth):
        kind, j = i % N_MIXERS, i // N_MIXERS
        ctx_out = i < depth - 1
        modtab = jnp.stack([jnp.broadcast_to(mods[i, B], (B, 6 * D)), mods[i, :B]], axis=1)[:, :, None, :]
        if kind == 0:
            xs = _attention_layer(xs, modtab, norm_mix_g[i], attn_w_qkv[j], attn_w_o[j], attn_q_norm_g[j],
                                  attn_k_norm_g[j], cos_t, sin_t, ctx_out)
        elif kind == 1:
            assert ctx_out
            xs = _pool_layer(xs, modtab, norm_mix_g[i], pool_w[j], pool_scale[j], m)
        else:
            assert ctx_out
            xs = _ssd_layer(xs, modtab, norm_mix_g[i], ssm_w_in[j], ssm_conv_w[j], ssm_conv_b[j],
                            ssm_dt_bias[j], ssm_a_log[j], ssm_d[j], ssm_norm_g[j], ssm_w_out[j], m)
        xs, sorted_buf = _moe_layer(xs, modtab, norm_ffn_g[i], moe_w_router_group[i], moe_w_router_expert[i],
                                    moe_w_gate_up, moe_w_down, i, ctx_out, sorted_buf,
                                    final_g=final_norm_g if i == depth - 1 else None)
    return xs
```
